```python
import math
import jax, jax.numpy as jnp
from jax import lax
import numpy as np

D_MODEL = 2048
BATCH = 2
SEQ = 4096
DEPTH = 4

HEAD_DIM = 128
N_HEADS = D_MODEL // HEAD_DIM
WIDTH = N_HEADS * HEAD_DIM
N_MIXERS = 3
ROPE_THETA = 10000.0
EPS = 1e-6
ATTN_SCALE = HEAD_DIM ** -0.5
NEG = -1e30
FORCE = 1e30
TINY = 1e-30
MOBA_BLOCK = 256
MOBA_TOPK = 3
MOBA_Q_CHUNK = 16
SB_Q_BLOCK = 128
NSA_KV_HEADS = 4
NSA_HPG = N_HEADS // NSA_KV_HEADS
KV_WIDTH = NSA_KV_HEADS * HEAD_DIM
CMP_BLOCK = 32
CMP_STRIDE = 16
SLC_BLOCK = 64
SLC_TOPK = 16
WINDOW = 512
NSA_Q_CHUNK = 32
WIN_Q_BLOCK = 128
NSA_IN_COLS = WIDTH + 6 * KV_WIDTH + WIDTH + 3 * N_HEADS

kernel_name = 'hybrid_moba_stickbreak_nsa_trunk'


def rmsnorm(x, g):
    x32 = x.astype(jnp.float32)
    y = x32 * lax.rsqrt(jnp.mean(x32 * x32, axis=-1, keepdims=True) + EPS)
    return (y * g.astype(jnp.float32)).astype(x.dtype)


def rope(x, pos):
    half = x.shape[-1] // 2
    inv_freq = jnp.exp(-math.log(ROPE_THETA) * jnp.arange(half, dtype=jnp.float32) / half)
    ang = pos.astype(jnp.float32)[:, None] * inv_freq[None, :]
    cos, sin = jnp.cos(ang), jnp.sin(ang)
    x32 = x.astype(jnp.float32)
    x1, x2 = x32[..., :half], x32[..., half:]
    return jnp.concatenate([x1 * cos - x2 * sin, x2 * cos + x1 * sin], axis=-1).astype(x.dtype)


def split_heads(t, n):
    B, S, _ = t.shape
    return t.reshape(B, S, n, HEAD_DIM).transpose(0, 2, 1, 3)


def merge_heads(o):
    B, H, S, D = o.shape
    return o.transpose(0, 2, 1, 3).reshape(B, S, H * D)


def masked_softmax(logits, mask):
    logits = jnp.where(mask, logits, NEG)
    m = jnp.max(logits, axis=-1, keepdims=True)
    e = jnp.where(mask, jnp.exp(logits - m), 0.0)
    return e / jnp.maximum(jnp.sum(e, axis=-1, keepdims=True), TINY)


def gated_output(x, o, gate, w_out):
    y = merge_heads(o).astype(x.dtype) * jax.nn.silu(gate)
    return x + y @ w_out


def moba_attention(q, k, v):
    B, H, S, D = q.shape
    nb = -(-S // MOBA_BLOCK)
    pad = nb * MOBA_BLOCK - S
    kb = jnp.pad(k, ((0, 0), (0, 0), (0, pad), (0, 0))).reshape(B, H, nb, MOBA_BLOCK, D)
    vb = jnp.pad(v, ((0, 0), (0, 0), (0, pad), (0, 0))).reshape(B, H, nb, MOBA_BLOCK, D)
    k_mean = jnp.mean(kb.astype(jnp.float32), axis=3)
    n_sel = min(MOBA_TOPK, nb - 1)
    nc = S // MOBA_Q_CHUNK
    q_chunks = jnp.moveaxis(q.reshape(B, H, nc, MOBA_Q_CHUNK, D), 2, 0)
    bi = jnp.arange(B)[:, None, None, None]
    hi = jnp.arange(H)[None, :, None, None]
    blk_pos = jnp.arange(MOBA_BLOCK)
    blk_ids = jnp.arange(nb)

    def one_chunk(args):
        c, qc = args
        t = c * MOBA_Q_CHUNK + jnp.arange(MOBA_Q_CHUNK)
        cur = (c * MOBA_Q_CHUNK) // MOBA_BLOCK
        k_own = lax.dynamic_index_in_dim(kb, cur, axis=2, keepdims=False)
        v_own = lax.dynamic_index_in_dim(vb, cur, axis=2, keepdims=False)
        s_own = jnp.einsum('bhqd,bhkd->bhqk', qc, k_own).astype(jnp.float32) * ATTN_SCALE
        own_mask = (cur * MOBA_BLOCK + blk_pos)[None, :] <= t[:, None]
        s_own = jnp.where(own_mask, s_own, NEG)
        if n_sel == 0:
            p = jax.nn.softmax(s_own, axis=-1)
            return jnp.einsum('bhqk,bhkd->bhqd', p.astype(v.dtype), v_own)
        gate = jnp.einsum('bhqd,bhnd->bhqn', qc.astype(jnp.float32), k_mean)
        gate = jnp.where(blk_ids < cur, gate, NEG)
        _, idx = lax.top_k(gate, n_sel)
        sel_valid = idx < cur
        k_sel = kb[bi, hi, idx]
        v_sel = vb[bi, hi, idx]
        s_sel = jnp.einsum('bhqd,bhqnkd->bhqnk', qc, k_sel).astype(jnp.float32) * ATTN_SCALE
        s_sel = jnp.where(sel_valid[..., None], s_sel, NEG).reshape(B, H, MOBA_Q_CHUNK, n_sel * MOBA_BLOCK)
        p = jax.nn.softmax(jnp.concatenate([s_sel, s_own], axis=-1), axis=-1)
        p_sel = p[..., :n_sel * MOBA_BLOCK].reshape(B, H, MOBA_Q_CHUNK, n_sel, MOBA_BLOCK)
        p_own = p[..., n_sel * MOBA_BLOCK:]
        return (jnp.einsum('bhqnk,bhqnkd->bhqd', p_sel.astype(v.dtype), v_sel)
                + jnp.einsum('bhqk,bhkd->bhqd', p_own.astype(v.dtype), v_own))

    out = lax.map(one_chunk, (jnp.arange(nc), q_chunks))
    return jnp.moveaxis(out, 0, 2).reshape(B, H, S, D)


def moba_layer(x, norm, w_in, q_norm, k_norm, w_out):
    pos = jnp.arange(x.shape[1])
    h = rmsnorm(x, norm) @ w_in
    q, k, v, gate = jnp.split(h, [WIDTH, 2 * WIDTH, 3 * WIDTH], axis=-1)
    q = rope(rmsnorm(split_heads(q, N_HEADS), q_norm), pos)
    k = rope(rmsnorm(split_heads(k, N_HEADS), k_norm), pos)
    o = moba_attention(q, k, split_heads(v, N_HEADS))
    return gated_output(x, o, gate, w_out)


def stick_breaking_attention(q, k, v):
    B, H, S, D = q.shape
    nqb = S // SB_Q_BLOCK
    q_blocks = jnp.moveaxis(q.reshape(B, H, nqb, SB_Q_BLOCK, D), 2, 0)
    s_pos = jnp.arange(S)

    def one_block(args):
        c, qb = args
        t = c * SB_Q_BLOCK + jnp.arange(SB_Q_BLOCK)
        z = jnp.einsum('bhqd,bhkd->bhqk', qb, k).astype(jnp.float32) * ATTN_SCALE
        past = s_pos[None, :] < t[:, None]
        log_keep = jnp.where(past, jax.nn.log_sigmoid(-z), 0.0)
        log_after = lax.cumsum(log_keep, axis=3, reverse=True) - log_keep
        a = jnp.where(past, jnp.exp(jax.nn.log_sigmoid(z) + log_after), 0.0)
        return jnp.einsum('bhqk,bhkd->bhqd', a.astype(v.dtype), v)

    out = lax.map(one_block, (jnp.arange(nqb), q_blocks))
    return jnp.moveaxis(out, 0, 2).reshape(B, H, S, D)


def stick_breaking_layer(x, norm, w_in, w_out):
    h = rmsnorm(x, norm) @ w_in
    q, k, v, gate = jnp.split(h, [WIDTH, 2 * WIDTH, 3 * WIDTH], axis=-1)
    o = stick_breaking_attention(split_heads(q, N_HEADS), split_heads(k, N_HEADS), split_heads(v, N_HEADS))
    return gated_output(x, o, gate, w_out)


def nsa_compressed(qg, kc, vc, pos_c):
    S = qg.shape[3]
    t = jnp.arange(S)
    logits = jnp.einsum('bgrqd,bgnd->bgrqn', qg, kc).astype(jnp.float32) * ATTN_SCALE
    p = masked_softmax(logits, pos_c[None, :] <= t[:, None])
    o = jnp.einsum('bgrqn,bgnd->bgrqd', p.astype(vc.dtype), vc)
    return o, jnp.sum(p, axis=2)


def nsa_selected(qg, ks, vs, p_cmp):
    B, G, R, S, D = qg.shape
    n_cmp = p_cmp.shape[-1]
    n_slc = S // SLC_BLOCK
    c_start = jnp.arange(n_cmp) * CMP_STRIDE
    s_start = jnp.arange(n_slc) * SLC_BLOCK
    overlap = ((c_start[:, None] < s_start[None, :] + SLC_BLOCK)
               & (c_start[:, None] + CMP_BLOCK > s_start[None, :])).astype(jnp.float32)
    imp = jnp.einsum('bgqn,nj->bgqj', p_cmp, overlap)
    t = jnp.arange(S)
    cur = t // SLC_BLOCK
    j = jnp.arange(n_slc)
    forced = (j[None, :] == 0) | (j[None, :] == cur[:, None]) | (j[None, :] == cur[:, None] - 1)
    imp = jnp.where(forced, FORCE, imp)
    imp = jnp.where(j[None, :] <= cur[:, None], imp, NEG)
    n_sel = min(SLC_TOPK, n_slc)
    _, idx = lax.top_k(imp, n_sel)
    ksb = ks.reshape(B, G, n_slc, SLC_BLOCK, D)
    vsb = vs.reshape(B, G, n_slc, SLC_BLOCK, D)
    nc = S // NSA_Q_CHUNK
    q_chunks = jnp.moveaxis(qg.reshape(B, G, R, nc, NSA_Q_CHUNK, D), 3, 0)
    idx_chunks = jnp.moveaxis(idx.reshape(B, G, nc, NSA_Q_CHUNK, n_sel), 2, 0)
    bi = jnp.arange(B)[:, None, None, None]
    gi = jnp.arange(G)[None, :, None, None]
    blk_pos = jnp.arange(SLC_BLOCK)

    def one_chunk(args):
        c, qc, ic = args
        tc = c * NSA_Q_CHUNK + jnp.arange(NSA_Q_CHUNK)
        k_sel = ksb[bi, gi, ic]
        v_sel = vsb[bi, gi, ic]
        key_pos = ic[..., None] * SLC_BLOCK + blk_pos
        mask = key_pos <= tc[None, None, :, None, None]
        logits = jnp.einsum('bgrqd,bgqnkd->bgrqnk', qc, k_sel).astype(jnp.float32) * ATTN_SCALE
        logits = jnp.where(mask[:, :, None], logits, NEG).reshape(B, G, R, NSA_Q_CHUNK, n_sel * SLC_BLOCK)
        p = jax.nn.softmax(logits, axis=-1).reshape(B, G, R, NSA_Q_CHUNK, n_sel, SLC_BLOCK)
        return jnp.einsum('bgrqnk,bgqnkd->bgrqd', p.astype(vs.dtype), v_sel)

    out = lax.map(one_chunk, (jnp.arange(nc), q_chunks, idx_chunks))
    return jnp.moveaxis(out, 0, 3).reshape(B, G, R, S, D)


def nsa_window(qg, kw, vw):
    B, G, R, S, D = qg.shape
    nqb = S // WIN_Q_BLOCK
    span = WINDOW + WIN_Q_BLOCK
    kp = jnp.pad(kw, ((0, 0), (0, 0), (WINDOW, 0), (0, 0)))
    vp = jnp.pad(vw, ((0, 0), (0, 0), (WINDOW, 0), (0, 0)))
    widx = jnp.arange(nqb)[:, None] * WIN_Q_BLOCK + jnp.arange(span)[None, :]
    k_blk = kp[:, :, widx]
    v_blk = vp[:, :, widx]
    qb = qg.reshape(B, G, R, nqb, WIN_Q_BLOCK, D)
    logits = jnp.einsum('bgrcqd,bgckd->bgrcqk', qb, k_blk).astype(jnp.float32) * ATTN_SCALE
    s_pos = widx - WINDOW
    t_pos = jnp.arange(nqb)[:, None] * WIN_Q_BLOCK + jnp.arange(WIN_Q_BLOCK)[None, :]
    diff = t_pos[:, :, None] - s_pos[:, None, :]
    mask = (diff >= 0) & (diff < WINDOW) & (s_pos[:, None, :] >= 0)
    p = jax.nn.softmax(jnp.where(mask, logits, NEG), axis=-1)
    o = jnp.einsum('bgrcqk,bgckd->bgrcqd', p.astype(vw.dtype), v_blk)
    return o.reshape(B, G, R, S, D)


def nsa_layer(x, norm, w_in, q_norm, kc_norm, ks_norm, kw_norm, cmp_wk, cmp_wv, cmp_pos, w_out):
    B, S, _ = x.shape
    pos = jnp.arange(S)
    h = rmsnorm(x, norm) @ w_in
    cuts = [WIDTH + i * KV_WIDTH for i in range(7)] + [2 * WIDTH + 6 * KV_WIDTH]
    q, kc, vc, ks, vs, kw, vw, gate, bgate = jnp.split(h, cuts, axis=-1)
    q = rope(rmsnorm(split_heads(q, N_HEADS), q_norm), pos)
    ks = rope(rmsnorm(split_heads(ks, NSA_KV_HEADS), ks_norm), pos)
    kw = rope(rmsnorm(split_heads(kw, NSA_KV_HEADS), kw_norm), pos)
    kc, vc = split_heads(kc, NSA_KV_HEADS), split_heads(vc, NSA_KV_HEADS)
    vs, vw = split_heads(vs, NSA_KV_HEADS), split_heads(vw, NSA_KV_HEADS)
    n_cmp = (S - CMP_BLOCK) // CMP_STRIDE + 1
    cidx = jnp.arange(n_cmp)[:, None] * CMP_STRIDE + jnp.arange(CMP_BLOCK)[None, :]
    kc_c = jnp.einsum('bgnld,lde->bgne', kc[:, :, cidx] + cmp_pos, cmp_wk)
    vc_c = jnp.einsum('bgnld,lde->bgne', vc[:, :, cidx] + cmp_pos, cmp_wv)
    pos_c = cidx[:, -1]
    kc_c = rope(rmsnorm(kc_c, kc_norm), pos_c)
    qg = q.reshape(B, NSA_KV_HEADS, NSA_HPG, S, HEAD_DIM)
    o_cmp, p_cmp = nsa_compressed(qg, kc_c, vc_c, pos_c)
    o_slc = nsa_selected(qg, ks, vs, p_cmp)
    o_win = nsa_window(qg, kw, vw)
    g = jax.nn.sigmoid(bgate.astype(jnp.float32)).reshape(B, S, 3, N_HEADS).transpose(2, 0, 3, 1)[..., None]
    shp = (B, N_HEADS, S, HEAD_DIM)
    o = g[0] * o_cmp.reshape(shp) + g[1] * o_slc.reshape(shp) + g[2] * o_win.reshape(shp)
    return gated_output(x, o, gate, w_out)


def setup_inputs(seed: int = 0) -> dict:
    key = jax.random.key(seed)
    keys = iter(jax.random.split(key, 64))

    def normal(shape, scale):
        return jax.random.normal(next(keys), shape, jnp.float32) * scale

    def gain(n):
        return 1.0 + 0.1 * normal((n,), 1.0)

    inputs = {'x': normal((BATCH, SEQ, D_MODEL), 1.0)}
    for i in range(DEPTH):
        p = 'l%d_' % i
        kind = i % N_MIXERS
        inputs[p + 'norm'] = gain(D_MODEL)
        if kind == 0:
            inputs[p + 'w_in'] = normal((D_MODEL, 4 * WIDTH), D_MODEL ** -0.5)
            inputs[p + 'q_norm'] = gain(HEAD_DIM)
            inputs[p + 'k_norm'] = gain(HEAD_DIM)
        elif kind == 1:
            inputs[p + 'w_in'] = normal((D_MODEL, 4 * WIDTH), D_MODEL ** -0.5)
        else:
            inputs[p + 'w_in'] = normal((D_MODEL, NSA_IN_COLS), D_MODEL ** -0.5)
            inputs[p + 'q_norm'] = gain(HEAD_DIM)
            inputs[p + 'kc_norm'] = gain(HEAD_DIM)
            inputs[p + 'ks_norm'] = gain(HEAD_DIM)
            inputs[p + 'kw_norm'] = gain(HEAD_DIM)
            inputs[p + 'cmp_wk'] = normal((CMP_BLOCK, HEAD_DIM, HEAD_DIM), (CMP_BLOCK * HEAD_DIM) ** -0.5)
            inputs[p + 'cmp_wv'] = normal((CMP_BLOCK, HEAD_DIM, HEAD_DIM), (CMP_BLOCK * HEAD_DIM) ** -0.5)
            inputs[p + 'cmp_pos'] = normal((CMP_BLOCK, HEAD_DIM), 0.1)
        inputs[p + 'w_out'] = normal((WIDTH, D_MODEL), WIDTH ** -0.5)
    return inputs


def reference(x,
              l0_norm, l0_w_in, l0_q_norm, l0_k_norm, l0_w_out,
              l1_norm, l1_w_in, l1_w_out,
              l2_norm, l2_w_in, l2_q_norm, l2_kc_norm, l2_ks_norm, l2_kw_norm,
              l2_cmp_wk, l2_cmp_wv, l2_cmp_pos, l2_w_out,
              l3_norm, l3_w_in, l3_q_norm, l3_k_norm, l3_w_out):
    layer_params = [
        dict(norm=l0_norm, w_in=l0_w_in, q_norm=l0_q_norm, k_norm=l0_k_norm, w_out=l0_w_out),
        dict(norm=l1_norm, w_in=l1_w_in, w_out=l1_w_out),
        dict(norm=l2_norm, w_in=l2_w_in, q_norm=l2_q_norm, kc_norm=l2_kc_norm, ks_norm=l2_ks_norm,
             kw_norm=l2_kw_norm, cmp_wk=l2_cmp_wk, cmp_wv=l2_cmp_wv, cmp_pos=l2_cmp_pos, w_out=l2_w_out),
        dict(norm=l3_norm, w_in=l3_w_in, q_norm=l3_q_norm, k_norm=l3_k_norm, w_out=l3_w_out),
    ]
    mixers = (moba_layer, stick_breaking_layer, nsa_layer)
    for i in range(DEPTH):
        x = mixers[i % N_MIXERS](x, **layer_params[i])
    return x
```

```python
import functools
import math

import jax
import jax.numpy as jnp
from jax import lax
from jax.experimental import pallas as pl
from jax.experimental.pallas import tpu as pltpu

F32 = jnp.float32
BF16 = jnp.bfloat16
HIGHEST = lax.Precision.HIGHEST

D_MODEL = 2048
HEAD_DIM = 128
N_HEADS = 16
WIDTH = N_HEADS * HEAD_DIM
ROPE_THETA = 10000.0
EPS = 1e-6
ATTN_SCALE = HEAD_DIM ** -0.5
NEG = -1e30
LOWEST = -3e38
FORCE = 1e30
TINY = 1e-30
MOBA_BLOCK = 256
MOBA_TOPK = 3
NSA_KV_HEADS = 4
NSA_HPG = N_HEADS // NSA_KV_HEADS
KV_WIDTH = NSA_KV_HEADS * HEAD_DIM
CMP_BLOCK = 32
CMP_STRIDE = 16
SLC_BLOCK = 64
SLC_TOPK = 16
WINDOW = 512

LANES = 128
AUG = 2 * HEAD_DIM
ATT_TILE = 256
VMEM_LIMIT = 48 * 1024 * 1024

_NT = (((1,), (1,)), ((), ()))


def _params(*sem):
    return pltpu.CompilerParams(dimension_semantics=sem, vmem_limit_bytes=VMEM_LIMIT)


def _rope_tables(pos):
    half = HEAD_DIM // 2
    inv_freq = jnp.exp(-math.log(ROPE_THETA) * jnp.arange(half, dtype=F32) / half)
    ang = pos.astype(F32)[:, None] * inv_freq[None, :]
    cos, sin = jnp.cos(ang), jnp.sin(ang)
    return jnp.concatenate([cos, cos], axis=-1), jnp.concatenate([-sin, sin], axis=-1)


def _rms_rope(x, g, cos2, sin2):
    y = x * lax.rsqrt(jnp.mean(x * x, axis=-1, keepdims=True) + EPS) * g
    return y * cos2 + pltpu.roll(y, HEAD_DIM // 2, 1) * sin2


def _iota(shape, dim):
    return lax.broadcasted_iota(jnp.int32, shape, dim)


def _log2(n):
    s = n.bit_length() - 1
    assert 1 << s == n
    return s


def _in_proj_body(x_ref, g_ref, w_ref, o_ref, xn_ref):
    @pl.when(pl.program_id(1) == 0)
    def _():
        x = x_ref[...]
        r = lax.rsqrt(jnp.mean(x * x, axis=-1, keepdims=True) + EPS)
        xn_ref[...] = (x * r * g_ref[...]).astype(xn_ref.dtype)

    o_ref[...] = jnp.dot(xn_ref[...], w_ref[...], preferred_element_type=F32).astype(o_ref.dtype)


def _in_proj(x2, g, w, out_dtype, tm, tn):
    M, K = x2.shape
    N = w.shape[1]
    return pl.pallas_call(
        _in_proj_body,
        grid=(M // tm, N // tn),
        in_specs=[pl.BlockSpec((tm, K), lambda i, j: (i, 0)),
                  pl.BlockSpec((1, K), lambda i, j: (0, 0)),
                  pl.BlockSpec((K, tn), lambda i, j: (0, j))],
        out_specs=pl.BlockSpec((tm, tn), lambda i, j: (i, j)),
        out_shape=jax.ShapeDtypeStruct((M, N), out_dtype),
        scratch_shapes=[pltpu.VMEM((tm, K), BF16)],
        compiler_params=_params("parallel", "arbitrary"),
        name="in_proj",
    )(x2, g.reshape(1, K), w)


def _out_proj_body(*refs, n_o):
    x_ref, gate_ref, w_ref = refs[0], refs[1], refs[2]
    o_refs = refs[3:3 + n_o]
    out_ref = refs[3 + n_o]
    o = o_refs[0][...].astype(F32)
    for r in o_refs[1:]:
        o = o + r[...].astype(F32)
    gate = gate_ref[...].astype(F32)
    y = o * (gate / (1.0 + jnp.exp(-gate)))
    out_ref[...] = x_ref[...] + jnp.dot(y.astype(BF16), w_ref[...], preferred_element_type=F32)


def _out_proj(x2, h2, gate_blk, w, os, tm=256):
    M, D = x2.shape
    n_o = len(os)
    row = pl.BlockSpec((tm, D), lambda i: (i, 0))
    return pl.pallas_call(
        functools.partial(_out_proj_body, n_o=n_o),
        grid=(M // tm,),
        in_specs=[row, pl.BlockSpec((tm, D), lambda i: (i, gate_blk)),
                  pl.BlockSpec((D, D), lambda i: (0, 0))] + [row] * n_o,
        out_specs=row,
        out_shape=jax.ShapeDtypeStruct((M, D), F32),
        compiler_params=_params("parallel"),
        name="out_proj",
    )(x2, h2, w, *os)


def _branch_gate(bg_ref, col):
    bg = bg_ref[0]
    g = 1.0 / (1.0 + jnp.exp(-bg))
    return jnp.sum(jnp.where(_iota(bg.shape, 1) == col, g, 0.0), axis=1, keepdims=True)


def _sel_attn_body(*refs, t, gate_base):
    if gate_base is None:
        q_ref, k_ref, v_ref, o_ref = refs
    else:
        q_ref, k_ref, v_ref, bg_ref, o_ref = refs
    qi = pl.program_id(2)
    q = q_ref[0]
    row = _iota((t, t), 0)
    col = _iota((t, t), 1)

    def kv(j):
        off = pl.multiple_of(j * t, t)
        return k_ref[0, pl.ds(off, t), :], v_ref[0, pl.ds(off, t), :]

    kd, vd = kv(qi)
    s = lax.dot_general(q, kd, _NT, preferred_element_type=F32)
    s = jnp.where(col <= row, s, NEG)
    m = jnp.max(s, axis=1, keepdims=True)
    p = jnp.exp(s - m)
    l = jnp.sum(p, axis=1, keepdims=True)
    acc = jnp.dot(p.astype(BF16), vd, preferred_element_type=F32)

    def body(j, carry):
        m, l, acc = carry
        kj, vj = kv(j)
        s = lax.dot_general(q, kj, _NT, preferred_element_type=F32)
        m_new = jnp.maximum(m, jnp.max(s, axis=1, keepdims=True))
        alpha = jnp.exp(m - m_new)
        p = jnp.exp(s - m_new)
        l = alpha * l + jnp.sum(p, axis=1, keepdims=True)
        acc = alpha * acc + jnp.dot(p.astype(BF16), vj, preferred_element_type=F32)
        return m_new, l, acc

    m, l, acc = lax.fori_loop(0, qi, body, (m, l, acc))
    o = acc / l
    if gate_base is not None:
        o = o * _branch_gate(bg_ref, gate_base + pl.program_id(1))
    o_ref[0] = o.astype(o_ref.dtype)


def _sel_attn(q_aug, k_aug, v, hpg, bg=None, gate_base=None):
    B, S, _ = q_aug.shape
    t = ATT_TILE
    in_specs = [pl.BlockSpec((1, t, AUG), lambda b, h, i: (b, i, h)),
                pl.BlockSpec((1, S, AUG), lambda b, h, i: (b, 0, h // hpg)),
                pl.BlockSpec((1, S, HEAD_DIM), lambda b, h, i: (b, 0, h // hpg))]
    args = [q_aug, k_aug, v]
    if bg is not None:
        in_specs.append(pl.BlockSpec((1, t, LANES), lambda b, h, i: (b, i, 0)))
        args.append(bg)
    return pl.pallas_call(
        functools.partial(_sel_attn_body, t=t, gate_base=gate_base),
        grid=(B, N_HEADS, S // t),
        in_specs=in_specs,
        out_specs=pl.BlockSpec((1, t, HEAD_DIM), lambda b, h, i: (b, i, h)),
        out_shape=jax.ShapeDtypeStruct((B, S, WIDTH), BF16),
        compiler_params=_params("parallel", "parallel", "arbitrary"),
        name="sel_attn",
    )(*args)


def _onehot_block(S, shift):
    return jnp.where((_iota((S, LANES), 0) >> shift) == _iota((S, LANES), 1), 1.0, 0.0).astype(BF16)


def _moba_prep_k_body(k_ref, v_ref, kn_ref, cos_ref, sin_ref, kaug_ref, vb_ref, kmean_ref, *, nb):
    kn = _rms_rope(k_ref[0], kn_ref[...], cos_ref[...], sin_ref[...])
    S = kn.shape[0]
    kaug_ref[0, :, :HEAD_DIM] = kn.astype(BF16)
    kaug_ref[0, :, HEAD_DIM:] = _onehot_block(S, _log2(MOBA_BLOCK))
    vb_ref[0] = v_ref[0].astype(BF16)
    kmean_ref[0, 0] = jnp.zeros((LANES, HEAD_DIM), F32)
    kmean_ref[0, 0, :nb, :] = jnp.mean(kn.reshape(nb, MOBA_BLOCK, HEAD_DIM), axis=1)


def _moba_prep_k(h, k_norm, cos2, sin2):
    B, S, _ = h.shape
    nb = S // MOBA_BLOCK
    assert 1 < nb <= LANES
    full = pl.BlockSpec((S, HEAD_DIM), lambda b, hh: (0, 0))
    return pl.pallas_call(
        functools.partial(_moba_prep_k_body, nb=nb),
        grid=(B, N_HEADS),
        in_specs=[pl.BlockSpec((1, S, HEAD_DIM), lambda b, hh: (b, 0, N_HEADS + hh)),
                  pl.BlockSpec((1, S, HEAD_DIM), lambda b, hh: (b, 0, 2 * N_HEADS + hh)),
                  pl.BlockSpec((1, HEAD_DIM), lambda b, hh: (0, 0)), full, full],
        out_specs=[pl.BlockSpec((1, S, AUG), lambda b, hh: (b, 0, hh)),
                   pl.BlockSpec((1, S, HEAD_DIM), lambda b, hh: (b, 0, hh)),
                   pl.BlockSpec((1, 1, LANES, HEAD_DIM), lambda b, hh: (b, hh, 0, 0))],
        out_shape=[jax.ShapeDtypeStruct((B, S, N_HEADS * AUG), BF16),
                   jax.ShapeDtypeStruct((B, S, WIDTH), BF16),
                   jax.ShapeDtypeStruct((B, N_HEADS, LANES, HEAD_DIM), F32)],
        compiler_params=_params("parallel", "parallel"),
        name="moba_prep_k",
    )(h, h, k_norm.reshape(1, HEAD_DIM), cos2, sin2)


def _moba_prep_q_body(q_ref, qn_ref, cos_ref, sin_ref, kmean_ref, qaug_ref, *, n_sel):
    cur = pl.program_id(2)
    qn = _rms_rope(q_ref[0], qn_ref[...], cos_ref[...], sin_ref[...])
    gate = lax.dot_general(qn, kmean_ref[0, 0], _NT, precision=HIGHEST, preferred_element_type=F32)
    lane = _iota(gate.shape, 1)
    lane_f = lane.astype(F32)
    g = jnp.where(lane < cur, gate, NEG)
    picked = jnp.zeros(gate.shape, F32)
    for _ in range(n_sel):
        mx = jnp.max(g, axis=1, keepdims=True)
        first = jnp.min(jnp.where(g == mx, lane_f, float(LANES)), axis=1, keepdims=True)
        hit = lane_f == first
        picked = jnp.where(hit, 1.0, picked)
        g = jnp.where(hit, LOWEST, g)
    attend = jnp.where(lane < cur, picked, jnp.where(lane == cur, 1.0, 0.0))
    qaug_ref[0, :, :HEAD_DIM] = (qn * ATTN_SCALE).astype(BF16)
    qaug_ref[0, :, HEAD_DIM:] = jnp.where(attend > 0.0, 0.0, NEG).astype(BF16)


def _moba_prep_q(h, q_norm, cos2, sin2, kmean):
    B, S, _ = h.shape
    t = MOBA_BLOCK
    assert t == ATT_TILE
    n_sel = min(MOBA_TOPK, S // MOBA_BLOCK - 1)
    tab = pl.BlockSpec((t, HEAD_DIM), lambda b, hh, i: (i, 0))
    return pl.pallas_call(
        functools.partial(_moba_prep_q_body, n_sel=n_sel),
        grid=(B, N_HEADS, S // t),
        in_specs=[pl.BlockSpec((1, t, HEAD_DIM), lambda b, hh, i: (b, i, hh)),
                  pl.BlockSpec((1, HEAD_DIM), lambda b, hh, i: (0, 0)), tab, tab,
                  pl.BlockSpec((1, 1, LANES, HEAD_DIM), lambda b, hh, i: (b, hh, 0, 0))],
        out_specs=pl.BlockSpec((1, t, AUG), lambda b, hh, i: (b, i, hh)),
        out_shape=jax.ShapeDtypeStruct((B, S, N_HEADS * AUG), BF16),
        compiler_params=_params("parallel", "parallel", "arbitrary"),
        name="moba_prep_q",
    )(h, q_norm.reshape(1, HEAD_DIM), cos2, sin2, kmean)


def _moba_layer(x, norm, w_in, q_norm, k_norm, w_out, cos2, sin2):
    B, S, D = x.shape
    x2 = x.reshape(B * S, D)
    h2 = _in_proj(x2, norm, w_in.astype(BF16), F32, 512, 1024)
    h = h2.reshape(B, S, -1)
    k_aug, v_b, kmean = _moba_prep_k(h, k_norm, cos2, sin2)
    q_aug = _moba_prep_q(h, q_norm, cos2, sin2, kmean)
    o = _sel_attn(q_aug, k_aug, v_b, 1)
    out = _out_proj(x2, h2, 3, w_out.astype(BF16), [o.reshape(B * S, WIDTH)])
    return out.reshape(B, S, D)


def _split3_dot(a, u):
    hi = a.astype(BF16)
    r = a - hi.astype(F32)
    mid = r.astype(BF16)
    lo = (r - mid.astype(F32)).astype(BF16)
    return (jnp.dot(hi, u, preferred_element_type=F32) + jnp.dot(mid, u, preferred_element_type=F32)
            + jnp.dot(lo, u, preferred_element_type=F32))


def _sb_attn_body(q_ref, k_ref, v_ref, o_ref, *, t):
    qi = pl.program_id(2)
    q = q_ref[0]
    row = _iota((t, t), 0)
    col = _iota((t, t), 1)
    past = col < row
    after = jnp.where(row > col, 1.0, 0.0).astype(BF16)

    def tile(j, carry, acc, diag):
        off = pl.multiple_of(j * t, t)
        kj = k_ref[0, pl.ds(off, t), :]
        vj = v_ref[0, pl.ds(off, t), :]
        z = lax.dot_general(q, kj, _NT, preferred_element_type=F32) * ATTN_SCALE
        log_keep = -(jnp.maximum(z, 0.0) + jnp.log(1.0 + jnp.exp(-jnp.abs(z))))
        if diag:
            log_keep = jnp.where(past, log_keep, 0.0)
        later = _split3_dot(log_keep, after)
        a = jnp.exp(z + log_keep + later + carry)
        if diag:
            a = jnp.where(past, a, 0.0)
        acc = acc + jnp.dot(a.astype(BF16), vj, preferred_element_type=F32)
        carry = carry + jnp.sum(log_keep, axis=1, keepdims=True)
        return carry, acc

    carry, acc = tile(qi, jnp.zeros((t, 1), F32), jnp.zeros((t, HEAD_DIM), F32), True)

    def body(i, c):
        return tile(qi - 1 - i, c[0], c[1], False)

    carry, acc = lax.fori_loop(0, qi, body, (carry, acc))
    o_ref[0] = acc.astype(o_ref.dtype)


def _sb_attn(h):
    B, S, _ = h.shape
    t = ATT_TILE
    return pl.pallas_call(
        functools.partial(_sb_attn_body, t=t),
        grid=(B, N_HEADS, S // t),
        in_specs=[pl.BlockSpec((1, t, HEAD_DIM), lambda b, hh, i: (b, i, hh)),
                  pl.BlockSpec((1, S, HEAD_DIM), lambda b, hh, i: (b, 0, N_HEADS + hh)),
                  pl.BlockSpec((1, S, HEAD_DIM), lambda b, hh, i: (b, 0, 2 * N_HEADS + hh))],
        out_specs=pl.BlockSpec((1, t, HEAD_DIM), lambda b, hh, i: (b, i, hh)),
        out_shape=jax.ShapeDtypeStruct((B, S, WIDTH), BF16),
        compiler_params=_params("parallel", "parallel", "arbitrary"),
        name="sb_attn",
    )(h, h, h)


def _sb_layer(x, norm, w_in, w_out):
    B, S, D = x.shape
    x2 = x.reshape(B * S, D)
    h2 = _in_proj(x2, norm, w_in.astype(BF16), BF16, 512, 1024)
    o = _sb_attn(h2.reshape(B, S, -1))
    out = _out_proj(x2, h2, 3, w_out.astype(BF16), [o.reshape(B * S, WIDTH)])
    return out.reshape(B, S, D)


NSA_KV_BLK0 = 2 * N_HEADS


def _nsa_compress_body(kc_ref, vc_ref, wk_ref, wv_ref, pos_ref, kcn_ref, cosc_ref, sinc_ref,
                       kcc_ref, vcc_ref, *, n_rows):
    half = CMP_BLOCK // CMP_STRIDE
    assert half == 2

    def compress(src_ref, w_ref):
        first = jnp.zeros((n_rows, HEAD_DIM), F32)
        second = jnp.zeros((n_rows, HEAD_DIM), F32)
        for l in range(CMP_STRIDE):
            rows = src_ref[0, pl.ds(l, n_rows, stride=CMP_STRIDE), :]
            first = first + jnp.dot(rows + pos_ref[l:l + 1, :], w_ref[l],
                                    precision=HIGHEST, preferred_element_type=F32)
            second = second + jnp.dot(rows + pos_ref[CMP_STRIDE + l:CMP_STRIDE + l + 1, :],
                                      w_ref[CMP_STRIDE + l], precision=HIGHEST, preferred_element_type=F32)
        return first + pltpu.roll(second, n_rows - 1, 0)

    kc = compress(kc_ref, wk_ref)
    kcc_ref[0, 0] = _rms_rope(kc, kcn_ref[...], cosc_ref[...], sinc_ref[...])
    vcc_ref[0, 0] = compress(vc_ref, wv_ref)


def _nsa_compress(h, cmp_wk, cmp_wv, cmp_pos, kc_norm, cos_c, sin_c):
    B, S, _ = h.shape
    n_rows = S // CMP_STRIDE
    wspec = pl.BlockSpec((CMP_BLOCK, HEAD_DIM, HEAD_DIM), lambda b, g: (0, 0, 0))
    tab = pl.BlockSpec((n_rows, HEAD_DIM), lambda b, g: (0, 0))
    out = pl.BlockSpec((1, 1, n_rows, HEAD_DIM), lambda b, g: (b, g, 0, 0))
    return pl.pallas_call(
        functools.partial(_nsa_compress_body, n_rows=n_rows),
        grid=(B, NSA_KV_HEADS),
        in_specs=[pl.BlockSpec((1, S, HEAD_DIM), lambda b, g: (b, 0, NSA_KV_BLK0 + g)),
                  pl.BlockSpec((1, S, HEAD_DIM), lambda b, g: (b, 0, NSA_KV_BLK0 + NSA_KV_HEADS + g)),
                  wspec, wspec,
                  pl.BlockSpec((CMP_BLOCK, HEAD_DIM), lambda b, g: (0, 0)),
                  pl.BlockSpec((1, HEAD_DIM), lambda b, g: (0, 0)), tab, tab],
        out_specs=[out, out],
        out_shape=[jax.ShapeDtypeStruct((B, NSA_KV_HEADS, n_rows, HEAD_DIM), F32)] * 2,
        compiler_params=_params("parallel", "parallel"),
        name="nsa_compress",
    )(h, h, cmp_wk, cmp_wv, cmp_pos, kc_norm.reshape(1, HEAD_DIM), cos_c, sin_c)


def _nsa_prep_kv_body(ks_ref, vs_ref, kw_ref, vw_ref, ksn_ref, kwn_ref, cos_ref, sin_ref,
                      ksaug_ref, vsb_ref, kwb_ref, vwb_ref):
    cos2, sin2 = cos_ref[...], sin_ref[...]
    S = cos2.shape[0]
    ksaug_ref[0, :, :HEAD_DIM] = _rms_rope(ks_ref[0], ksn_ref[...], cos2, sin2).astype(BF16)
    ksaug_ref[0, :, HEAD_DIM:] = _onehot_block(S, _log2(SLC_BLOCK))
    kwb_ref[0] = _rms_rope(kw_ref[0], kwn_ref[...], cos2, sin2).astype(BF16)
    vsb_ref[0] = vs_ref[0].astype(BF16)
    vwb_ref[0] = vw_ref[0].astype(BF16)


def _nsa_prep_kv(h, ks_norm, kw_norm, cos2, sin2):
    B, S, _ = h.shape
    G = NSA_KV_HEADS

    def src(i):
        return pl.BlockSpec((1, S, HEAD_DIM), lambda b, g: (b, 0, NSA_KV_BLK0 + i * G + g))

    vec = pl.BlockSpec((1, HEAD_DIM), lambda b, g: (0, 0))
    full = pl.BlockSpec((S, HEAD_DIM), lambda b, g: (0, 0))
    dst = pl.BlockSpec((1, S, HEAD_DIM), lambda b, g: (b, 0, g))
    kv = jax.ShapeDtypeStruct((B, S, KV_WIDTH), BF16)
    return pl.pallas_call(
        _nsa_prep_kv_body,
        grid=(B, G),
        in_specs=[src(2), src(3), src(4), src(5), vec, vec, full, full],
        out_specs=[pl.BlockSpec((1, S, AUG), lambda b, g: (b, 0, g)), dst, dst, dst],
        out_shape=[jax.ShapeDtypeStruct((B, S, G * AUG), BF16), kv, kv, kv],
        compiler_params=_params("parallel", "parallel"),
        name="nsa_prep_kv",
    )(h, h, h, h, ks_norm.reshape(1, HEAD_DIM), kw_norm.reshape(1, HEAD_DIM), cos2, sin2)


def _nsa_cmp_sel_body(q_ref, qn_ref, cos_ref, sin_ref, kcc_ref, vcc_ref, bg_ref, ocmp_ref, qaug_ref,
                      *, t, n_cmp, n_slc, n_sel):
    g_idx = pl.program_id(1)
    q0 = pl.program_id(2) * t
    cos2, sin2 = cos_ref[...], sin_ref[...]
    kcc = kcc_ref[0, 0]
    vcc = vcc_ref[0, 0].astype(BF16)
    tpos = q0 + _iota((t, n_cmp), 0)
    cmp_ok = (_iota((t, n_cmp), 1) * CMP_STRIDE + (CMP_BLOCK - 1)) <= tpos

    p_sum = jnp.zeros((t, n_cmp), F32)
    qs = []
    for r in range(NSA_HPG):
        qn = _rms_rope(q_ref[0, :, r * HEAD_DIM:(r + 1) * HEAD_DIM], qn_ref[...], cos2, sin2)
        qs.append((qn * ATTN_SCALE).astype(BF16))
        logits = lax.dot_general(qn, kcc, _NT, precision=HIGHEST, preferred_element_type=F32) * ATTN_SCALE
        logits = jnp.where(cmp_ok, logits, NEG)
        mx = jnp.max(logits, axis=1, keepdims=True)
        e = jnp.where(cmp_ok, jnp.exp(logits - mx), 0.0)
        p = e / jnp.maximum(jnp.sum(e, axis=1, keepdims=True), TINY)
        p_sum = p_sum + p
        o = jnp.dot(p.astype(BF16), vcc, preferred_element_type=F32)
        o = o * _branch_gate(bg_ref, g_idx * NSA_HPG + r)
        ocmp_ref[0, :, r * HEAD_DIM:(r + 1) * HEAD_DIM] = o.astype(ocmp_ref.dtype)

    c_start = _iota((n_cmp, LANES), 0) * CMP_STRIDE
    s_start = _iota((n_cmp, LANES), 1) * SLC_BLOCK
    overlap = jnp.where((c_start < s_start + SLC_BLOCK) & (c_start + CMP_BLOCK > s_start), 1.0, 0.0)
    imp = jnp.dot(p_sum, overlap, precision=HIGHEST, preferred_element_type=F32)
    lane = _iota((t, LANES), 1)
    cur = (q0 + _iota((t, LANES), 0)) >> _log2(SLC_BLOCK)
    forced = (lane == 0) | (lane == cur) | (lane == cur - 1)
    imp = jnp.where(forced, FORCE, imp)
    imp = jnp.where(lane <= cur, imp, NEG)
    imp = jnp.where(lane < n_slc, imp, LOWEST)
    rank = jnp.zeros((t, LANES), F32)
    for k in range(n_slc):
        c = imp[:, k:k + 1]
        rank = rank + jnp.where(lane > k, jnp.where(c >= imp, 1.0, 0.0), jnp.where(c > imp, 1.0, 0.0))
    bias = jnp.where(rank < float(n_sel), 0.0, NEG).astype(BF16)
    for r in range(NSA_HPG):
        qaug_ref[0, :, r * AUG:r * AUG + HEAD_DIM] = qs[r]
        qaug_ref[0, :, r * AUG + HEAD_DIM:(r + 1) * AUG] = bias


def _nsa_cmp_sel(h, bg, q_norm, cos2, sin2, kc_c, vc_c):
    B, S, _ = h.shape
    t = ATT_TILE
    G, R = NSA_KV_HEADS, NSA_HPG
    n_cmp = S // CMP_STRIDE
    n_slc = S // SLC_BLOCK
    assert n_slc <= LANES
    tab = pl.BlockSpec((t, HEAD_DIM), lambda b, g, i: (i, 0))
    cmp = pl.BlockSpec((1, 1, n_cmp, HEAD_DIM), lambda b, g, i: (b, g, 0, 0))
    return pl.pallas_call(
        functools.partial(_nsa_cmp_sel_body, t=t, n_cmp=n_cmp, n_slc=n_slc, n_sel=min(SLC_TOPK, n_slc)),
        grid=(B, G, S // t),
        in_specs=[pl.BlockSpec((1, t, R * HEAD_DIM), lambda b, g, i: (b, i, g)),
                  pl.BlockSpec((1, HEAD_DIM), lambda b, g, i: (0, 0)), tab, tab, cmp, cmp,
                  pl.BlockSpec((1, t, LANES), lambda b, g, i: (b, i, 0))],
        out_specs=[pl.BlockSpec((1, t, R * HEAD_DIM), lambda b, g, i: (b, i, g)),
                   pl.BlockSpec((1, t, R * AUG), lambda b, g, i: (b, i, g))],
        out_shape=[jax.ShapeDtypeStruct((B, S, WIDTH), BF16),
                   jax.ShapeDtypeStruct((B, S, N_HEADS * AUG), BF16)],
        compiler_params=_params("parallel", "parallel", "arbitrary"),
        name="nsa_cmp_sel",
    )(h, q_norm.reshape(1, HEAD_DIM), cos2, sin2, kc_c, vc_c, bg)


def _win_attn_body(q_ref, k_ref, v_ref, bg_ref, o_ref, *, t, n_back, gate_base):
    qi = pl.program_id(2)
    q = q_ref[0]
    row = _iota((t, t), 0)
    col = _iota((t, t), 1)
    m = jnp.full((t, 1), NEG, F32)
    l = jnp.zeros((t, 1), F32)
    acc = jnp.zeros((t, HEAD_DIM), F32)
    for d in range(n_back + 1):
        j = jnp.maximum(qi - d, 0)
        off = pl.multiple_of(j * t, t)
        s = lax.dot_general(q, k_ref[0, pl.ds(off, t), :], _NT, preferred_element_type=F32)
        if d == 0:
            s = jnp.where(col <= row, s, NEG)
        else:
            first_ok = jnp.where(qi >= d, 0, t)
            if d == n_back:
                s = jnp.where(col > row + first_ok, s, NEG)
            else:
                s = jnp.where(col >= first_ok, s, NEG)
        m_new = jnp.maximum(m, jnp.max(s, axis=1, keepdims=True))
        alpha = jnp.exp(m - m_new)
        p = jnp.exp(s - m_new)
        l = alpha * l + jnp.sum(p, axis=1, keepdims=True)
        acc = alpha * acc + jnp.dot(p.astype(BF16), v_ref[0, pl.ds(off, t), :], preferred_element_type=F32)
        m = m_new
    o = acc / l * _branch_gate(bg_ref, gate_base + pl.program_id(1))
    o_ref[0] = o.astype(o_ref.dtype)


def _win_attn(q_aug, kw, vw, bg, gate_base):
    B, S, _ = q_aug.shape
    t = ATT_TILE
    assert WINDOW % t == 0
    kv = pl.BlockSpec((1, S, HEAD_DIM), lambda b, h, i: (b, 0, h // NSA_HPG))
    return pl.pallas_call(
        functools.partial(_win_attn_body, t=t, n_back=WINDOW // t, gate_base=gate_base),
        grid=(B, N_HEADS, S // t),
        in_specs=[pl.BlockSpec((1, t, HEAD_DIM), lambda b, h, i: (b, i, 2 * h)), kv, kv,
                  pl.BlockSpec((1, t, LANES), lambda b, h, i: (b, i, 0))],
        out_specs=pl.BlockSpec((1, t, HEAD_DIM), lambda b, h, i: (b, i, h)),
        out_shape=jax.ShapeDtypeStruct((B, S, WIDTH), BF16),
        compiler_params=_params("parallel", "parallel", "arbitrary"),
        name="win_attn",
    )(q_aug, kw, vw, bg)


def _nsa_layer(x, norm, w_in, q_norm, kc_norm, ks_norm, kw_norm, cmp_wk, cmp_wv, cmp_pos, w_out, cos2, sin2):
    B, S, D = x.shape
    x2 = x.reshape(B * S, D)
    kv_end = WIDTH + 6 * KV_WIDTH
    gate_end = kv_end + WIDTH
    w_main = jnp.concatenate([w_in[:, :WIDTH], w_in[:, kv_end:gate_end], w_in[:, WIDTH:kv_end]], axis=1)
    w_bg = jnp.pad(w_in[:, gate_end:], ((0, 0), (0, LANES - 3 * N_HEADS)))
    h2 = _in_proj(x2, norm, w_main.astype(BF16), F32, 512, 1024)
    bg = _in_proj(x2, norm, w_bg.astype(BF16), F32, 512, LANES).reshape(B, S, LANES)
    h = h2.reshape(B, S, -1)
    cos_c, sin_c = _rope_tables(jnp.arange(S // CMP_STRIDE) * CMP_STRIDE + (CMP_BLOCK - 1))
    kc_c, vc_c = _nsa_compress(h, cmp_wk, cmp_wv, cmp_pos, kc_norm, cos_c, sin_c)
    ks_aug, vs_b, kw_b, vw_b = _nsa_prep_kv(h, ks_norm, kw_norm, cos2, sin2)
    o_cmp, q_aug = _nsa_cmp_sel(h, bg, q_norm, cos2, sin2, kc_c, vc_c)
    o_slc = _sel_attn(q_aug, ks_aug, vs_b, NSA_HPG, bg, N_HEADS)
    o_win = _win_attn(q_aug, kw_b, vw_b, bg, 2 * N_HEADS)
    os = [o.reshape(B * S, WIDTH) for o in (o_cmp, o_slc, o_win)]
    out = _out_proj(x2, h2, 1, w_out.astype(BF16), os)
    return out.reshape(B, S, D)


def kernel(x, l0_norm, l0_w_in, l0_q_norm, l0_k_norm, l0_w_out, l1_norm, l1_w_in, l1_w_out, l2_norm, l2_w_in, l2_q_norm, l2_kc_norm, l2_ks_norm, l2_kw_norm, l2_cmp_wk, l2_cmp_wv, l2_cmp_pos, l2_w_out, l3_norm, l3_w_in, l3_q_norm, l3_k_norm, l3_w_out):
    cos2, sin2 = _rope_tables(jnp.arange(x.shape[1]))
    x = _moba_layer(x, l0_norm, l0_w_in, l0_q_norm, l0_k_norm, l0_w_out, cos2, sin2)
    x = _sb_layer(x, l1_norm, l1_w_in, l1_w_out)
    x = _nsa_layer(x, l2_norm, l2_w_in, l2_q_norm, l2_kc_norm, l2_ks_norm, l2_kw_norm,
                   l2_cmp_wk, l2_cmp_wv, l2_cmp_pos, l2_w_out, cos2, sin2)
    x = _moba_layer(x, l3_norm, l3_w_in, l3_q_norm, l3_k_norm, l3_w_out, cos2, sin2)
    return x
```

```python
import functools
import math

import jax
import jax.numpy as jnp
from jax import lax
from jax.experimental import pallas as pl
from jax.experimental.pallas import tpu as pltpu

F32 = jnp.float32
BF16 = jnp.bfloat16
HIGHEST = lax.Precision.HIGHEST

D_MODEL = 2048
HEAD_DIM = 128
N_HEADS = 16
WIDTH = N_HEADS * HEAD_DIM
ROPE_THETA = 10000.0
EPS = 1e-6
ATTN_SCALE = HEAD_DIM ** -0.5
LOG2E = math.log2(math.e)
NEG = -1e30
LOWEST = -3e38
FORCE = 1e30
TINY = 1e-30
MOBA_BLOCK = 256
MOBA_TOPK = 3
NSA_KV_HEADS = 4
NSA_HPG = N_HEADS // NSA_KV_HEADS
KV_WIDTH = NSA_KV_HEADS * HEAD_DIM
CMP_BLOCK = 32
CMP_STRIDE = 16
SLC_BLOCK = 64
SLC_TOPK = 16
WINDOW = 512

LANES = 128
AUG = 2 * HEAD_DIM
ATT_TILE = 256
HEADS_PER_STEP = 4
VMEM_LIMIT = 48 * 1024 * 1024

_NT = (((1,), (1,)), ((), ()))


def _params(*sem):
    return pltpu.CompilerParams(dimension_semantics=sem, vmem_limit_bytes=VMEM_LIMIT)


def _rope_tables(pos):
    half = HEAD_DIM // 2
    inv_freq = jnp.exp(-math.log(ROPE_THETA) * jnp.arange(half, dtype=F32) / half)
    ang = pos.astype(F32)[:, None] * inv_freq[None, :]
    cos, sin = jnp.cos(ang), jnp.sin(ang)
    return jnp.concatenate([cos, cos], axis=-1), jnp.concatenate([-sin, sin], axis=-1)


def _rms_rope(x, g, cos2, sin2):
    y = x * lax.rsqrt(jnp.mean(x * x, axis=-1, keepdims=True) + EPS) * g
    return y * cos2 + pltpu.roll(y, HEAD_DIM // 2, 1) * sin2


def _iota(shape, dim):
    return lax.broadcasted_iota(jnp.int32, shape, dim)


def _log2(n):
    s = n.bit_length() - 1
    assert 1 << s == n
    return s


def _store_tiles_t(dst_ref, lead, x, t):
    for j in range(x.shape[0] // t):
        dst_ref[lead + (j,)] = x[j * t:(j + 1) * t, :].T.astype(dst_ref.dtype)


def _in_proj_body(x_ref, g_ref, w_ref, o_ref, xn_ref):
    @pl.when(pl.program_id(1) == 0)
    def _():
        x = x_ref[...]
        r = lax.rsqrt(jnp.mean(x * x, axis=-1, keepdims=True) + EPS)
        xn_ref[...] = (x * r * g_ref[...]).astype(xn_ref.dtype)

    o_ref[...] = jnp.dot(xn_ref[...], w_ref[...], preferred_element_type=F32).astype(o_ref.dtype)


def _in_proj(x2, g, w, out_dtype, tm, tn):
    M, K = x2.shape
    N = w.shape[1]
    return pl.pallas_call(
        _in_proj_body,
        grid=(M // tm, N // tn),
        in_specs=[pl.BlockSpec((tm, K), lambda i, j: (i, 0)),
                  pl.BlockSpec((1, K), lambda i, j: (0, 0)),
                  pl.BlockSpec((K, tn), lambda i, j: (0, j))],
        out_specs=pl.BlockSpec((tm, tn), lambda i, j: (i, j)),
        out_shape=jax.ShapeDtypeStruct((M, N), out_dtype),
        scratch_shapes=[pltpu.VMEM((tm, K), BF16)],
        compiler_params=_params("parallel", "arbitrary"),
        name="in_proj",
    )(x2, g.reshape(1, K), w)


def _out_proj_body(*refs, n_o):
    x_ref, gate_ref, w_ref = refs[0], refs[1], refs[2]
    o_refs = refs[3:3 + n_o]
    out_ref = refs[3 + n_o]
    o = o_refs[0][...].astype(F32)
    for r in o_refs[1:]:
        o = o + r[...].astype(F32)
    gate = gate_ref[...].astype(F32)
    y = o * (gate / (1.0 + jnp.exp(-gate)))
    out_ref[...] = x_ref[...] + jnp.dot(y.astype(BF16), w_ref[...], preferred_element_type=F32)


def _out_proj(x2, h2, gate_blk, w, os, tm=256):
    M, D = x2.shape
    n_o = len(os)
    row = pl.BlockSpec((tm, D), lambda i: (i, 0))
    return pl.pallas_call(
        functools.partial(_out_proj_body, n_o=n_o),
        grid=(M // tm,),
        in_specs=[row, pl.BlockSpec((tm, D), lambda i: (i, gate_blk)),
                  pl.BlockSpec((D, D), lambda i: (0, 0))] + [row] * n_o,
        out_specs=row,
        out_shape=jax.ShapeDtypeStruct((M, D), F32),
        compiler_params=_params("parallel"),
        name="out_proj",
    )(x2, h2, w, *os)


def _branch_gate(bg_ref, col):
    bg = bg_ref[0]
    g = 1.0 / (1.0 + jnp.exp(-bg))
    return jnp.sum(jnp.where(_iota(bg.shape, 1) == col, g, 0.0), axis=1, keepdims=True)


def _softmax_step(state, s, vt):
    m, l, acc = state
    m_new = jnp.maximum(m, jnp.max(s, axis=0, keepdims=True))
    alpha = jnp.exp2(m - m_new)
    p = jnp.exp2(s - m_new)
    l = alpha * l + jnp.sum(p, axis=0, keepdims=True)
    acc = alpha * acc + jnp.dot(vt, p.astype(BF16), preferred_element_type=F32)
    return m_new, l, acc


def _softmax_init(t):
    return jnp.full((1, t), NEG, F32), jnp.zeros((1, t), F32), jnp.zeros((HEAD_DIM, t), F32)


def _store_heads(o_ref, states, bg_ref, gate_col0):
    for r, (_, l, acc) in enumerate(states):
        o = (acc * (1.0 / l)).T
        if bg_ref is not None:
            o = o * _branch_gate(bg_ref, gate_col0 + r)
        o_ref[0, :, r * HEAD_DIM:(r + 1) * HEAD_DIM] = o.astype(o_ref.dtype)


def _sel_attn_body(*refs, t, nh, shared, gate_base):
    if gate_base is None:
        qt_ref, k_ref, vt_ref, o_ref = refs
        bg_ref = None
    else:
        qt_ref, k_ref, vt_ref, bg_ref, o_ref = refs
    qi = pl.program_id(2)
    causal = _iota((t, t), 0) <= _iota((t, t), 1)
    qts = [qt_ref[0, r, 0] for r in range(nh)]

    def kv(j, r):
        c = 0 if shared else r
        return k_ref[0, pl.ds(pl.multiple_of(j * t, t), t), c * AUG:(c + 1) * AUG], vt_ref[0, c, j]

    def step(j, states, diag):
        ss = [jnp.dot(kv(j, r)[0], qts[r], preferred_element_type=F32) for r in range(nh)]
        out = []
        for r in range(nh):
            s = jnp.where(causal, ss[r], NEG) if diag else ss[r]
            out.append(_softmax_step(states[r], s, kv(j, r)[1]))
        return tuple(out)

    states = step(qi, tuple(_softmax_init(t) for _ in range(nh)), True)
    states = lax.fori_loop(0, qi, lambda j, st: step(j, st, False), states)
    _store_heads(o_ref, states, bg_ref, None if gate_base is None else gate_base + pl.program_id(1) * nh)


def _sel_attn(q_t, k_aug, v_t, hpg, bg=None, gate_base=None):
    B, H, n_t, _, t = q_t.shape
    S = n_t * t
    nh = HEADS_PER_STEP
    shared = hpg > 1
    assert hpg in (1, nh)
    nk = 1 if shared else nh
    in_specs = [pl.BlockSpec((1, nh, 1, AUG, t), lambda b, h, i: (b, h, i, 0, 0)),
                pl.BlockSpec((1, S, nk * AUG), lambda b, h, i: (b, 0, h)),
                pl.BlockSpec((1, nk, n_t, HEAD_DIM, t), lambda b, h, i: (b, h, 0, 0, 0))]
    args = [q_t, k_aug, v_t]
    if bg is not None:
        in_specs.append(pl.BlockSpec((1, t, LANES), lambda b, h, i: (b, i, 0)))
        args.append(bg)
    return pl.pallas_call(
        functools.partial(_sel_attn_body, t=t, nh=nh, shared=shared, gate_base=gate_base),
        grid=(B, H // nh, n_t),
        in_specs=in_specs,
        out_specs=pl.BlockSpec((1, t, nh * HEAD_DIM), lambda b, h, i: (b, i, h)),
        out_shape=jax.ShapeDtypeStruct((B, S, WIDTH), BF16),
        compiler_params=_params("parallel", "parallel", "arbitrary"),
        name="sel_attn",
    )(*args)


def _onehot_block(S, shift):
    return jnp.where((_iota((S, LANES), 0) >> shift) == _iota((S, LANES), 1), 1.0, 0.0).astype(BF16)


def _moba_prep_k_body(k_ref, v_ref, kn_ref, cos_ref, sin_ref, kaug_ref, vt_ref, kmean_ref, *, nb, t):
    kn = _rms_rope(k_ref[0], kn_ref[...], cos_ref[...], sin_ref[...])
    S = kn.shape[0]
    kaug_ref[0, :, :HEAD_DIM] = kn.astype(BF16)
    kaug_ref[0, :, HEAD_DIM:] = _onehot_block(S, _log2(MOBA_BLOCK))
    _store_tiles_t(vt_ref, (0, 0), v_ref[0], t)
    kmean_ref[0, 0] = jnp.zeros((LANES, HEAD_DIM), F32)
    kmean_ref[0, 0, :nb, :] = jnp.mean(kn.reshape(nb, MOBA_BLOCK, HEAD_DIM), axis=1)


def _moba_prep_k(h, k_norm, cos2, sin2):
    B, S, _ = h.shape
    nb = S // MOBA_BLOCK
    t = ATT_TILE
    assert 1 < nb <= LANES
    full = pl.BlockSpec((S, HEAD_DIM), lambda b, hh: (0, 0))
    return pl.pallas_call(
        functools.partial(_moba_prep_k_body, nb=nb, t=t),
        grid=(B, N_HEADS),
        in_specs=[pl.BlockSpec((1, S, HEAD_DIM), lambda b, hh: (b, 0, N_HEADS + hh)),
                  pl.BlockSpec((1, S, HEAD_DIM), lambda b, hh: (b, 0, 2 * N_HEADS + hh)),
                  pl.BlockSpec((1, HEAD_DIM), lambda b, hh: (0, 0)), full, full],
        out_specs=[pl.BlockSpec((1, S, AUG), lambda b, hh: (b, 0, hh)),
                   pl.BlockSpec((1, 1, S // t, HEAD_DIM, t), lambda b, hh: (b, hh, 0, 0, 0)),
                   pl.BlockSpec((1, 1, LANES, HEAD_DIM), lambda b, hh: (b, hh, 0, 0))],
        out_shape=[jax.ShapeDtypeStruct((B, S, N_HEADS * AUG), BF16),
                   jax.ShapeDtypeStruct((B, N_HEADS, S // t, HEAD_DIM, t), BF16),
                   jax.ShapeDtypeStruct((B, N_HEADS, LANES, HEAD_DIM), F32)],
        compiler_params=_params("parallel", "parallel"),
        name="moba_prep_k",
    )(h, h, k_norm.reshape(1, HEAD_DIM), cos2, sin2)


def _moba_prep_q_body(q_ref, qn_ref, cos_ref, sin_ref, kmean_ref, qt_ref, *, n_sel):
    cur = pl.program_id(2)
    qn = _rms_rope(q_ref[0], qn_ref[...], cos_ref[...], sin_ref[...])
    gate = lax.dot_general(qn, kmean_ref[0, 0], _NT, precision=HIGHEST, preferred_element_type=F32)
    lane = _iota(gate.shape, 1)
    lane_f = lane.astype(F32)
    g = jnp.where(lane < cur, gate, NEG)
    picked = jnp.zeros(gate.shape, F32)
    for _ in range(n_sel):
        mx = jnp.max(g, axis=1, keepdims=True)
        first = jnp.min(jnp.where(g == mx, lane_f, float(LANES)), axis=1, keepdims=True)
        hit = lane_f == first
        picked = jnp.where(hit, 1.0, picked)
        g = jnp.where(hit, LOWEST, g)
    attend = jnp.where(lane < cur, picked, jnp.where(lane == cur, 1.0, 0.0))
    qt_ref[0, 0, 0, :HEAD_DIM, :] = (qn * (ATTN_SCALE * LOG2E)).T.astype(BF16)
    qt_ref[0, 0, 0, HEAD_DIM:, :] = jnp.where(attend > 0.0, 0.0, NEG).T.astype(BF16)


def _moba_prep_q(h, q_norm, cos2, sin2, kmean):
    B, S, _ = h.shape
    t = MOBA_BLOCK
    assert t == ATT_TILE
    n_sel = min(MOBA_TOPK, S // MOBA_BLOCK - 1)
    tab = pl.BlockSpec((t, HEAD_DIM), lambda b, hh, i: (i, 0))
    return pl.pallas_call(
        functools.partial(_moba_prep_q_body, n_sel=n_sel),
        grid=(B, N_HEADS, S // t),
        in_specs=[pl.BlockSpec((1, t, HEAD_DIM), lambda b, hh, i: (b, i, hh)),
                  pl.BlockSpec((1, HEAD_DIM), lambda b, hh, i: (0, 0)), tab, tab,
                  pl.BlockSpec((1, 1, LANES, HEAD_DIM), lambda b, hh, i: (b, hh, 0, 0))],
        out_specs=pl.BlockSpec((1, 1, 1, AUG, t), lambda b, hh, i: (b, hh, i, 0, 0)),
        out_shape=jax.ShapeDtypeStruct((B, N_HEADS, S // t, AUG, t), BF16),
        compiler_params=_params("parallel", "parallel", "arbitrary"),
        name="moba_prep_q",
    )(h, q_norm.reshape(1, HEAD_DIM), cos2, sin2, kmean)


def _moba_layer(x, norm, w_in, q_norm, k_norm, w_out, cos2, sin2):
    B, S, D = x.shape
    x2 = x.reshape(B * S, D)
    h2 = _in_proj(x2, norm, w_in.astype(BF16), F32, 512, 1024)
    h = h2.reshape(B, S, -1)
    k_aug, v_t, kmean = _moba_prep_k(h, k_norm, cos2, sin2)
    q_t = _moba_prep_q(h, q_norm, cos2, sin2, kmean)
    o = _sel_attn(q_t, k_aug, v_t, 1)
    out = _out_proj(x2, h2, 3, w_out.astype(BF16), [o.reshape(B * S, WIDTH)])
    return out.reshape(B, S, D)


def _split2_dot(a, u):
    hi = a.astype(BF16)
    lo = (a - hi.astype(F32)).astype(BF16)
    return jnp.dot(hi, u, preferred_element_type=F32) + jnp.dot(lo, u, preferred_element_type=F32)


def _sb_attn_body(q_ref, k_ref, v_ref, o_ref, kt_ref, *, t, nh):
    qi = pl.program_id(2)

    @pl.when(qi == 0)
    def _():
        for r in range(nh):
            _store_tiles_t(kt_ref, (r,), k_ref[0, :, r * HEAD_DIM:(r + 1) * HEAD_DIM].astype(F32), t)

    row = _iota((t, t), 0)
    col = _iota((t, t), 1)
    past = col < row
    after = jnp.where(row > col, 1.0, 0.0).astype(BF16)
    qs = [(q_ref[0, :, r * HEAD_DIM:(r + 1) * HEAD_DIM].astype(F32) * ATTN_SCALE).astype(BF16)
          for r in range(nh)]

    def step(j, states, diag):
        zs = [jnp.dot(qs[r], kt_ref[r, j], preferred_element_type=F32) for r in range(nh)]
        log_keeps, laters = [], []
        for r in range(nh):
            z = zs[r]
            log_keep = -(jnp.maximum(z, 0.0) + jnp.log(1.0 + jnp.exp(-jnp.abs(z))))
            if diag:
                log_keep = jnp.where(past, log_keep, 0.0)
            log_keeps.append(log_keep)
            laters.append(_split2_dot(log_keep, after))
        out = []
        for r in range(nh):
            carry, acc = states[r]
            vj = v_ref[0, pl.ds(pl.multiple_of(j * t, t), t), r * HEAD_DIM:(r + 1) * HEAD_DIM]
            a = jnp.exp(zs[r] + log_keeps[r] + laters[r] + carry)
            if diag:
                a = jnp.where(past, a, 0.0)
            acc = acc + jnp.dot(a.astype(BF16), vj, preferred_element_type=F32)
            carry = carry + jnp.sum(log_keeps[r], axis=1, keepdims=True)
            out.append((carry, acc))
        return tuple(out)

    init = tuple((jnp.zeros((t, 1), F32), jnp.zeros((t, HEAD_DIM), F32)) for _ in range(nh))
    states = step(qi, init, True)
    states = lax.fori_loop(0, qi, lambda i, st: step(qi - 1 - i, st, False), states)
    for r in range(nh):
        o_ref[0, :, r * HEAD_DIM:(r + 1) * HEAD_DIM] = states[r][1].astype(o_ref.dtype)


def _sb_attn(h):
    B, S, _ = h.shape
    t = ATT_TILE
    nh = HEADS_PER_STEP
    nhb = N_HEADS // nh
    w = nh * HEAD_DIM
    return pl.pallas_call(
        functools.partial(_sb_attn_body, t=t, nh=nh),
        grid=(B, nhb, S // t),
        in_specs=[pl.BlockSpec((1, t, w), lambda b, hh, i: (b, i, hh)),
                  pl.BlockSpec((1, S, w), lambda b, hh, i: (b, 0, nhb + hh)),
                  pl.BlockSpec((1, S, w), lambda b, hh, i: (b, 0, 2 * nhb + hh))],
        out_specs=pl.BlockSpec((1, t, w), lambda b, hh, i: (b, i, hh)),
        out_shape=jax.ShapeDtypeStruct((B, S, WIDTH), BF16),
        scratch_shapes=[pltpu.VMEM((nh, S // t, HEAD_DIM, t), BF16)],
        compiler_params=_params("parallel", "parallel", "arbitrary"),
        name="sb_attn",
    )(h, h, h)


def _sb_layer(x, norm, w_in, w_out):
    B, S, D = x.shape
    x2 = x.reshape(B * S, D)
    h2 = _in_proj(x2, norm, w_in.astype(BF16), BF16, 512, 1024)
    o = _sb_attn(h2.reshape(B, S, -1))
    out = _out_proj(x2, h2, 3, w_out.astype(BF16), [o.reshape(B * S, WIDTH)])
    return out.reshape(B, S, D)


NSA_KV_BLK0 = 2 * N_HEADS


def _nsa_compress_body(kc_ref, vc_ref, wk_ref, wv_ref, pos_ref, kcn_ref, cosc_ref, sinc_ref,
                       kcc_ref, vcc_ref, *, n_rows):
    half = CMP_BLOCK // CMP_STRIDE
    assert half == 2

    def compress(src_ref, w_ref):
        first = jnp.zeros((n_rows, HEAD_DIM), F32)
        second = jnp.zeros((n_rows, HEAD_DIM), F32)
        for l in range(CMP_STRIDE):
            rows = src_ref[0, pl.ds(l, n_rows, stride=CMP_STRIDE), :]
            first = first + jnp.dot(rows + pos_ref[l:l + 1, :], w_ref[l],
                                    precision=HIGHEST, preferred_element_type=F32)
            second = second + jnp.dot(rows + pos_ref[CMP_STRIDE + l:CMP_STRIDE + l + 1, :],
                                      w_ref[CMP_STRIDE + l], precision=HIGHEST, preferred_element_type=F32)
        return first + pltpu.roll(second, n_rows - 1, 0)

    kc = compress(kc_ref, wk_ref)
    kcc_ref[0, 0] = _rms_rope(kc, kcn_ref[...], cosc_ref[...], sinc_ref[...])
    vcc_ref[0, 0] = compress(vc_ref, wv_ref)


def _nsa_compress(h, cmp_wk, cmp_wv, cmp_pos, kc_norm, cos_c, sin_c):
    B, S, _ = h.shape
    n_rows = S // CMP_STRIDE
    wspec = pl.BlockSpec((CMP_BLOCK, HEAD_DIM, HEAD_DIM), lambda b, g: (0, 0, 0))
    tab = pl.BlockSpec((n_rows, HEAD_DIM), lambda b, g: (0, 0))
    out = pl.BlockSpec((1, 1, n_rows, HEAD_DIM), lambda b, g: (b, g, 0, 0))
    return pl.pallas_call(
        functools.partial(_nsa_compress_body, n_rows=n_rows),
        grid=(B, NSA_KV_HEADS),
        in_specs=[pl.BlockSpec((1, S, HEAD_DIM), lambda b, g: (b, 0, NSA_KV_BLK0 + g)),
                  pl.BlockSpec((1, S, HEAD_DIM), lambda b, g: (b, 0, NSA_KV_BLK0 + NSA_KV_HEADS + g)),
                  wspec, wspec,
                  pl.BlockSpec((CMP_BLOCK, HEAD_DIM), lambda b, g: (0, 0)),
                  pl.BlockSpec((1, HEAD_DIM), lambda b, g: (0, 0)), tab, tab],
        out_specs=[out, out],
        out_shape=[jax.ShapeDtypeStruct((B, NSA_KV_HEADS, n_rows, HEAD_DIM), F32)] * 2,
        compiler_params=_params("parallel", "parallel"),
        name="nsa_compress",
    )(h, h, cmp_wk, cmp_wv, cmp_pos, kc_norm.reshape(1, HEAD_DIM), cos_c, sin_c)


def _nsa_prep_kv_body(ks_ref, vs_ref, kw_ref, vw_ref, ksn_ref, kwn_ref, cos_ref, sin_ref,
                      ksaug_ref, vst_ref, kwb_ref, vwt_ref, *, t):
    cos2, sin2 = cos_ref[...], sin_ref[...]
    S = cos2.shape[0]
    ksaug_ref[0, :, :HEAD_DIM] = _rms_rope(ks_ref[0], ksn_ref[...], cos2, sin2).astype(BF16)
    ksaug_ref[0, :, HEAD_DIM:] = _onehot_block(S, _log2(SLC_BLOCK))
    kwb_ref[0] = _rms_rope(kw_ref[0], kwn_ref[...], cos2, sin2).astype(BF16)
    _store_tiles_t(vst_ref, (0, 0), vs_ref[0], t)
    _store_tiles_t(vwt_ref, (0, 0), vw_ref[0], t)


def _nsa_prep_kv(h, ks_norm, kw_norm, cos2, sin2):
    B, S, _ = h.shape
    G = NSA_KV_HEADS
    t = ATT_TILE

    def src(i):
        return pl.BlockSpec((1, S, HEAD_DIM), lambda b, g: (b, 0, NSA_KV_BLK0 + i * G + g))

    vec = pl.BlockSpec((1, HEAD_DIM), lambda b, g: (0, 0))
    full = pl.BlockSpec((S, HEAD_DIM), lambda b, g: (0, 0))
    vt = pl.BlockSpec((1, 1, S // t, HEAD_DIM, t), lambda b, g: (b, g, 0, 0, 0))
    vt_shape = jax.ShapeDtypeStruct((B, G, S // t, HEAD_DIM, t), BF16)
    return pl.pallas_call(
        functools.partial(_nsa_prep_kv_body, t=t),
        grid=(B, G),
        in_specs=[src(2), src(3), src(4), src(5), vec, vec, full, full],
        out_specs=[pl.BlockSpec((1, S, AUG), lambda b, g: (b, 0, g)), vt,
                   pl.BlockSpec((1, S, HEAD_DIM), lambda b, g: (b, 0, g)), vt],
        out_shape=[jax.ShapeDtypeStruct((B, S, G * AUG), BF16), vt_shape,
                   jax.ShapeDtypeStruct((B, S, KV_WIDTH), BF16), vt_shape],
        compiler_params=_params("parallel", "parallel"),
        name="nsa_prep_kv",
    )(h, h, h, h, ks_norm.reshape(1, HEAD_DIM), kw_norm.reshape(1, HEAD_DIM), cos2, sin2)


def _nsa_cmp_sel_body(q_ref, qn_ref, cos_ref, sin_ref, kcc_ref, vcc_ref, bg_ref, ocmp_ref, qt_ref,
                      *, t, n_cmp, n_slc, n_sel):
    g_idx = pl.program_id(1)
    q0 = pl.program_id(2) * t
    cos2, sin2 = cos_ref[...], sin_ref[...]
    kcc = kcc_ref[0, 0]
    vcc = vcc_ref[0, 0].astype(BF16)
    tpos = q0 + _iota((t, n_cmp), 0)
    cmp_ok = (_iota((t, n_cmp), 1) * CMP_STRIDE + (CMP_BLOCK - 1)) <= tpos

    p_sum = jnp.zeros((t, n_cmp), F32)
    for r in range(NSA_HPG):
        qn = _rms_rope(q_ref[0, :, r * HEAD_DIM:(r + 1) * HEAD_DIM], qn_ref[...], cos2, sin2)
        qt_ref[0, r, 0, :HEAD_DIM, :] = (qn * (ATTN_SCALE * LOG2E)).T.astype(BF16)
        logits = lax.dot_general(qn, kcc, _NT, precision=HIGHEST, preferred_element_type=F32) * ATTN_SCALE
        logits = jnp.where(cmp_ok, logits, NEG)
        mx = jnp.max(logits, axis=1, keepdims=True)
        e = jnp.where(cmp_ok, jnp.exp(logits - mx), 0.0)
        p = e / jnp.maximum(jnp.sum(e, axis=1, keepdims=True), TINY)
        p_sum = p_sum + p
        o = jnp.dot(p.astype(BF16), vcc, preferred_element_type=F32)
        o = o * _branch_gate(bg_ref, g_idx * NSA_HPG + r)
        ocmp_ref[0, :, r * HEAD_DIM:(r + 1) * HEAD_DIM] = o.astype(ocmp_ref.dtype)

    c_start = _iota((n_cmp, LANES), 0) * CMP_STRIDE
    s_start = _iota((n_cmp, LANES), 1) * SLC_BLOCK
    overlap = jnp.where((c_start < s_start + SLC_BLOCK) & (c_start + CMP_BLOCK > s_start), 1.0, 0.0)
    imp = jnp.dot(p_sum, overlap, precision=HIGHEST, preferred_element_type=F32)
    lane = _iota((t, LANES), 1)
    cur = (q0 + _iota((t, LANES), 0)) >> _log2(SLC_BLOCK)
    forced = (lane == 0) | (lane == cur) | (lane == cur - 1)
    imp = jnp.where(forced, FORCE, imp)
    imp = jnp.where(lane <= cur, imp, NEG)
    imp = jnp.where(lane < n_slc, imp, LOWEST)
    rank = jnp.zeros((t, LANES), F32)
    for k in range(n_slc):
        c = imp[:, k:k + 1]
        rank = rank + jnp.where(lane > k, jnp.where(c >= imp, 1.0, 0.0), jnp.where(c > imp, 1.0, 0.0))
    bias_t = jnp.where(rank < float(n_sel), 0.0, NEG).T.astype(BF16)
    for r in range(NSA_HPG):
        qt_ref[0, r, 0, HEAD_DIM:, :] = bias_t


def _nsa_cmp_sel(h, bg, q_norm, cos2, sin2, kc_c, vc_c):
    B, S, _ = h.shape
    t = ATT_TILE
    G, R = NSA_KV_HEADS, NSA_HPG
    n_cmp = S // CMP_STRIDE
    n_slc = S // SLC_BLOCK
    assert n_slc <= LANES
    tab = pl.BlockSpec((t, HEAD_DIM), lambda b, g, i: (i, 0))
    cmp = pl.BlockSpec((1, 1, n_cmp, HEAD_DIM), lambda b, g, i: (b, g, 0, 0))
    return pl.pallas_call(
        functools.partial(_nsa_cmp_sel_body, t=t, n_cmp=n_cmp, n_slc=n_slc, n_sel=min(SLC_TOPK, n_slc)),
        grid=(B, G, S // t),
        in_specs=[pl.BlockSpec((1, t, R * HEAD_DIM), lambda b, g, i: (b, i, g)),
                  pl.BlockSpec((1, HEAD_DIM), lambda b, g, i: (0, 0)), tab, tab, cmp, cmp,
                  pl.BlockSpec((1, t, LANES), lambda b, g, i: (b, i, 0))],
        out_specs=[pl.BlockSpec((1, t, R * HEAD_DIM), lambda b, g, i: (b, i, g)),
                   pl.BlockSpec((1, R, 1, AUG, t), lambda b, g, i: (b, g, i, 0, 0))],
        out_shape=[jax.ShapeDtypeStruct((B, S, WIDTH), BF16),
                   jax.ShapeDtypeStruct((B, N_HEADS, S // t, AUG, t), BF16)],
        compiler_params=_params("parallel", "parallel", "arbitrary"),
        name="nsa_cmp_sel",
    )(h, q_norm.reshape(1, HEAD_DIM), cos2, sin2, kc_c, vc_c, bg)


def _win_attn_body(qt_ref, k_ref, vt_ref, bg_ref, o_ref, *, t, nh, n_back, gate_base):
    qi = pl.program_id(2)
    row = _iota((t, t), 0)
    col = _iota((t, t), 1)
    qts = [qt_ref[0, r, 0] for r in range(nh)]
    states = [_softmax_init(t) for _ in range(nh)]
    for d in range(n_back + 1):
        j = jnp.maximum(qi - d, 0)
        k = k_ref[0, pl.ds(pl.multiple_of(j * t, t), t), :]
        vt = vt_ref[0, 0, j]
        first_ok = jnp.where(qi >= d, 0, t)
        ss = [jnp.dot(k, qts[r], preferred_element_type=F32) for r in range(nh)]
        for r in range(nh):
            s = ss[r]
            if d == 0:
                s = jnp.where(row <= col, s, NEG)
            elif d == n_back:
                s = jnp.where(row > col + first_ok, s, NEG)
            else:
                s = jnp.where(row >= first_ok, s, NEG)
            states[r] = _softmax_step(states[r], s, vt)
    _store_heads(o_ref, states, bg_ref, gate_base + pl.program_id(1) * nh)


def _win_attn(q_t, kw, vw_t, bg, gate_base):
    B, H, n_t, _, t = q_t.shape
    S = n_t * t
    nh = NSA_HPG
    assert WINDOW % t == 0
    return pl.pallas_call(
        functools.partial(_win_attn_body, t=t, nh=nh, n_back=WINDOW // t, gate_base=gate_base),
        grid=(B, H // nh, n_t),
        in_specs=[pl.BlockSpec((1, nh, 1, HEAD_DIM, t), lambda b, g, i: (b, g, i, 0, 0)),
                  pl.BlockSpec((1, S, HEAD_DIM), lambda b, g, i: (b, 0, g)),
                  pl.BlockSpec((1, 1, n_t, HEAD_DIM, t), lambda b, g, i: (b, g, 0, 0, 0)),
                  pl.BlockSpec((1, t, LANES), lambda b, g, i: (b, i, 0))],
        out_specs=pl.BlockSpec((1, t, nh * HEAD_DIM), lambda b, g, i: (b, i, g)),
        out_shape=jax.ShapeDtypeStruct((B, S, WIDTH), BF16),
        compiler_params=_params("parallel", "parallel", "arbitrary"),
        name="win_attn",
    )(q_t, kw, vw_t, bg)


def _nsa_layer(x, norm, w_in, q_norm, kc_norm, ks_norm, kw_norm, cmp_wk, cmp_wv, cmp_pos, w_out, cos2, sin2):
    B, S, D = x.shape
    x2 = x.reshape(B * S, D)
    kv_end = WIDTH + 6 * KV_WIDTH
    gate_end = kv_end + WIDTH
    w_main = jnp.concatenate([w_in[:, :WIDTH], w_in[:, kv_end:gate_end], w_in[:, WIDTH:kv_end]], axis=1)
    w_bg = jnp.pad(w_in[:, gate_end:], ((0, 0), (0, LANES - 3 * N_HEADS)))
    h2 = _in_proj(x2, norm, w_main.astype(BF16), F32, 512, 1024)
    bg = _in_proj(x2, norm, w_bg.astype(BF16), F32, 512, LANES).reshape(B, S, LANES)
    h = h2.reshape(B, S, -1)
    cos_c, sin_c = _rope_tables(jnp.arange(S // CMP_STRIDE) * CMP_STRIDE + (CMP_BLOCK - 1))
    kc_c, vc_c = _nsa_compress(h, cmp_wk, cmp_wv, cmp_pos, kc_norm, cos_c, sin_c)
    ks_aug, vs_t, kw_b, vw_t = _nsa_prep_kv(h, ks_norm, kw_norm, cos2, sin2)
    o_cmp, q_t = _nsa_cmp_sel(h, bg, q_norm, cos2, sin2, kc_c, vc_c)
    o_slc = _sel_attn(q_t, ks_aug, vs_t, NSA_HPG, bg, N_HEADS)
    o_win = _win_attn(q_t, kw_b, vw_t, bg, 2 * N_HEADS)
    os = [o.reshape(B * S, WIDTH) for o in (o_cmp, o_slc, o_win)]
    out = _out_proj(x2, h2, 1, w_out.astype(BF16), os)
    return out.reshape(B, S, D)


def kernel(x, l0_norm, l0_w_in, l0_q_norm, l0_k_norm, l0_w_out, l1_norm, l1_w_in, l1_w_out, l2_norm, l2_w_in, l2_q_norm, l2_kc_norm, l2_ks_norm, l2_kw_norm, l2_cmp_wk, l2_cmp_wv, l2_cmp_pos, l2_w_out, l3_norm, l3_w_in, l3_q_norm, l3_k_norm, l3_w_out):
    cos2, sin2 = _rope_tables(jnp.arange(x.shape[1]))
    x = _moba_layer(x, l0_norm, l0_w_in, l0_q_norm, l0_k_norm, l0_w_out, cos2, sin2)
    x = _sb_layer(x, l1_norm, l1_w_in, l1_w_out)
    x = _nsa_layer(x, l2_norm, l2_w_in, l2_q_norm, l2_kc_norm, l2_ks_norm, l2_kw_norm,
                   l2_cmp_wk, l2_cmp_wv, l2_cmp_pos, l2_w_out, cos2, sin2)
    x = _moba_layer(x, l3_norm, l3_w_in, l3_q_norm, l3_k_norm, l3_w_out, cos2, sin2)
    return x
```

```python
import functools
import math

import jax
import jax.numpy as jnp
from jax import lax
from jax.experimental import pallas as pl
from jax.experimental.pallas import tpu as pltpu

F32 = jnp.float32
BF16 = jnp.bfloat16
HIGHEST = lax.Precision.HIGHEST

D_MODEL = 2048
HEAD_DIM = 128
N_HEADS = 16
WIDTH = N_HEADS * HEAD_DIM
ROPE_THETA = 10000.0
EPS = 1e-6
ATTN_SCALE = HEAD_DIM ** -0.5
LOG2E = math.log2(math.e)
NEG = -1e30
LOWEST = -3e38
FORCE = 1e30
TINY = 1e-30
MOBA_BLOCK = 256
MOBA_TOPK = 3
NSA_KV_HEADS = 4
NSA_HPG = N_HEADS // NSA_KV_HEADS
KV_WIDTH = NSA_KV_HEADS * HEAD_DIM
CMP_BLOCK = 32
CMP_STRIDE = 16
SLC_BLOCK = 64
SLC_TOPK = 16
WINDOW = 512

LANES = 128
AUG = 2 * HEAD_DIM
ATT_TILE = 256
HEADS_PER_STEP = 4
VMEM_LIMIT = 48 * 1024 * 1024

_NT = (((1,), (1,)), ((), ()))


def _params(*sem):
    return pltpu.CompilerParams(dimension_semantics=sem, vmem_limit_bytes=VMEM_LIMIT)


def _rope_tables(pos):
    half = HEAD_DIM // 2
    inv_freq = jnp.exp(-math.log(ROPE_THETA) * jnp.arange(half, dtype=F32) / half)
    ang = pos.astype(F32)[:, None] * inv_freq[None, :]
    cos, sin = jnp.cos(ang), jnp.sin(ang)
    return jnp.concatenate([cos, cos], axis=-1), jnp.concatenate([-sin, sin], axis=-1)


def _rms_rope(x, g, cos2, sin2):
    y = x * lax.rsqrt(jnp.mean(x * x, axis=-1, keepdims=True) + EPS) * g
    return y * cos2 + pltpu.roll(y, HEAD_DIM // 2, 1) * sin2


def _iota(shape, dim):
    return lax.broadcasted_iota(jnp.int32, shape, dim)


def _log2(n):
    s = n.bit_length() - 1
    assert 1 << s == n
    return s


def _store_tiles_t(dst_ref, lead, x, t):
    for j in range(x.shape[0] // t):
        dst_ref[lead + (j,)] = x[j * t:(j + 1) * t, :].T.astype(dst_ref.dtype)


def _in_proj_body(x_ref, g_ref, w_ref, o_ref, xn_ref):
    @pl.when(pl.program_id(1) == 0)
    def _():
        x = x_ref[...]
        r = lax.rsqrt(jnp.mean(x * x, axis=-1, keepdims=True) + EPS)
        xn_ref[...] = (x * r * g_ref[...]).astype(xn_ref.dtype)

    o_ref[...] = jnp.dot(xn_ref[...], w_ref[...], preferred_element_type=F32).astype(o_ref.dtype)


def _in_proj(x2, g, w, out_dtype, tm, tn):
    M, K = x2.shape
    N = w.shape[1]
    return pl.pallas_call(
        _in_proj_body,
        grid=(M // tm, N // tn),
        in_specs=[pl.BlockSpec((tm, K), lambda i, j: (i, 0)),
                  pl.BlockSpec((1, K), lambda i, j: (0, 0)),
                  pl.BlockSpec((K, tn), lambda i, j: (0, j))],
        out_specs=pl.BlockSpec((tm, tn), lambda i, j: (i, j)),
        out_shape=jax.ShapeDtypeStruct((M, N), out_dtype),
        scratch_shapes=[pltpu.VMEM((tm, K), BF16)],
        compiler_params=_params("parallel", "arbitrary"),
        name="in_proj",
    )(x2, g.reshape(1, K), w)


def _out_proj_body(*refs, n_o):
    x_ref, gate_ref, w_ref = refs[0], refs[1], refs[2]
    o_refs = refs[3:3 + n_o]
    out_ref = refs[3 + n_o]
    o = o_refs[0][...].astype(F32)
    for r in o_refs[1:]:
        o = o + r[...].astype(F32)
    gate = gate_ref[...].astype(F32)
    y = o * (gate / (1.0 + jnp.exp(-gate)))
    out_ref[...] = x_ref[...] + jnp.dot(y.astype(BF16), w_ref[...], preferred_element_type=F32)


def _out_proj(x2, h2, gate_blk, w, os, tm=256):
    M, D = x2.shape
    n_o = len(os)
    row = pl.BlockSpec((tm, D), lambda i: (i, 0))
    return pl.pallas_call(
        functools.partial(_out_proj_body, n_o=n_o),
        grid=(M // tm,),
        in_specs=[row, pl.BlockSpec((tm, D), lambda i: (i, gate_blk)),
                  pl.BlockSpec((D, D), lambda i: (0, 0))] + [row] * n_o,
        out_specs=row,
        out_shape=jax.ShapeDtypeStruct((M, D), F32),
        compiler_params=_params("parallel"),
        name="out_proj",
    )(x2, h2, w, *os)


def _branch_gate(bg_ref, col):
    bg = bg_ref[0]
    g = 1.0 / (1.0 + jnp.exp(-bg))
    return jnp.sum(jnp.where(_iota(bg.shape, 1) == col, g, 0.0), axis=1, keepdims=True)


def _softmax_step(state, s, vt):
    m, l, acc = state
    m_new = jnp.maximum(m, jnp.max(s, axis=0, keepdims=True))
    alpha = jnp.exp2(m - m_new)
    p = jnp.exp2(s - m_new)
    l = alpha * l + jnp.sum(p, axis=0, keepdims=True)
    acc = alpha * acc + jnp.dot(vt, p.astype(BF16), preferred_element_type=F32)
    return m_new, l, acc


def _softmax_init(t):
    return jnp.full((1, t), NEG, F32), jnp.zeros((1, t), F32), jnp.zeros((HEAD_DIM, t), F32)


def _store_heads(o_ref, states, bg_ref, gate_col0):
    for r, (_, l, acc) in enumerate(states):
        o = (acc * (1.0 / l)).T
        if bg_ref is not None:
            o = o * _branch_gate(bg_ref, gate_col0 + r)
        o_ref[0, :, r * HEAD_DIM:(r + 1) * HEAD_DIM] = o.astype(o_ref.dtype)


def _attend_causal(qts, k_ref, vt_ref, qi, t, shared):
    nh = len(qts)
    row = _iota((2 * t, t), 0)
    col = _iota((2 * t, t), 1)

    def block(j0, states, bounds):
        off = pl.multiple_of(j0 * t, t)
        ss = [jnp.dot(k_ref[0, pl.ds(off, 2 * t), (0 if shared else r) * AUG:(1 if shared else r + 1) * AUG],
                      qts[r], preferred_element_type=F32) for r in range(nh)]
        out = []
        for r in range(nh):
            s = ss[r]
            if bounds is not None:
                s = jnp.where(row <= col + bounds[1], s, NEG)
                s = jnp.where(row >= bounds[0], s, NEG)
            c = 0 if shared else r
            vt = jnp.concatenate([vt_ref[0, c, j0], vt_ref[0, c, j0 + 1]], axis=1)
            out.append(_softmax_step(states[r], s, vt))
        return tuple(out)

    j0 = jnp.maximum(qi - 1, 0)
    first_key = (qi - (qi & 1) - j0) * t
    states = block(j0, tuple(_softmax_init(t) for _ in range(nh)), (first_key, (qi - j0) * t))
    return lax.fori_loop(0, qi >> 1, lambda j2, st: block(2 * j2, st, None), states)


def _sel_attn_body(qt_ref, k_ref, vt_ref, bg_ref, o_ref, *, t, nh, gate_base):
    qts = [qt_ref[0, r, 0] for r in range(nh)]
    states = _attend_causal(qts, k_ref, vt_ref, pl.program_id(2), t, True)
    _store_heads(o_ref, states, bg_ref, gate_base + pl.program_id(1) * nh)


def _sel_attn(q_t, k_aug, v_t, bg, gate_base):
    B, H, n_t, _, t = q_t.shape
    S = n_t * t
    nh = NSA_HPG
    assert n_t >= 2
    return pl.pallas_call(
        functools.partial(_sel_attn_body, t=t, nh=nh, gate_base=gate_base),
        grid=(B, H // nh, n_t),
        in_specs=[pl.BlockSpec((1, nh, 1, AUG, t), lambda b, h, i: (b, h, i, 0, 0)),
                  pl.BlockSpec((1, S, AUG), lambda b, h, i: (b, 0, h)),
                  pl.BlockSpec((1, 1, n_t, HEAD_DIM, t), lambda b, h, i: (b, h, 0, 0, 0)),
                  pl.BlockSpec((1, t, LANES), lambda b, h, i: (b, i, 0))],
        out_specs=pl.BlockSpec((1, t, nh * HEAD_DIM), lambda b, h, i: (b, i, h)),
        out_shape=jax.ShapeDtypeStruct((B, S, WIDTH), BF16),
        compiler_params=_params("parallel", "parallel", "arbitrary"),
        name="sel_attn",
    )(q_t, k_aug, v_t, bg)


def _onehot_block(S, shift):
    return jnp.where((_iota((S, LANES), 0) >> shift) == _iota((S, LANES), 1), 1.0, 0.0).astype(BF16)


def _moba_prep_k_body(k_ref, v_ref, kn_ref, cos_ref, sin_ref, kaug_ref, vt_ref, kmean_ref, *, nb, t):
    kn = _rms_rope(k_ref[0], kn_ref[...], cos_ref[...], sin_ref[...])
    S = kn.shape[0]
    kaug_ref[0, :, :HEAD_DIM] = kn.astype(BF16)
    kaug_ref[0, :, HEAD_DIM:] = _onehot_block(S, _log2(MOBA_BLOCK))
    _store_tiles_t(vt_ref, (0, 0), v_ref[0], t)
    kmean_ref[0, 0] = jnp.mean(kn.reshape(nb, MOBA_BLOCK, HEAD_DIM), axis=1)


def _moba_prep_k(h, k_norm, cos2, sin2):
    B, S, _ = h.shape
    nb = S // MOBA_BLOCK
    t = ATT_TILE
    assert 1 < nb <= HEAD_DIM
    full = pl.BlockSpec((S, HEAD_DIM), lambda b, hh: (0, 0))
    return pl.pallas_call(
        functools.partial(_moba_prep_k_body, nb=nb, t=t),
        grid=(B, N_HEADS),
        in_specs=[pl.BlockSpec((1, S, HEAD_DIM), lambda b, hh: (b, 0, N_HEADS + hh)),
                  pl.BlockSpec((1, S, HEAD_DIM), lambda b, hh: (b, 0, 2 * N_HEADS + hh)),
                  pl.BlockSpec((1, HEAD_DIM), lambda b, hh: (0, 0)), full, full],
        out_specs=[pl.BlockSpec((1, S, AUG), lambda b, hh: (b, 0, hh)),
                   pl.BlockSpec((1, 1, S // t, HEAD_DIM, t), lambda b, hh: (b, hh, 0, 0, 0)),
                   pl.BlockSpec((1, 1, nb, HEAD_DIM), lambda b, hh: (b, hh, 0, 0))],
        out_shape=[jax.ShapeDtypeStruct((B, S, N_HEADS * AUG), BF16),
                   jax.ShapeDtypeStruct((B, N_HEADS, S // t, HEAD_DIM, t), BF16),
                   jax.ShapeDtypeStruct((B, N_HEADS, nb, HEAD_DIM), F32)],
        compiler_params=_params("parallel", "parallel"),
        name="moba_prep_k",
    )(h, h, k_norm.reshape(1, HEAD_DIM), cos2, sin2)


def _moba_attn_body(q_ref, qn_ref, cos_ref, sin_ref, kmean_ref, k_ref, vt_ref, o_ref, *, t, nh, nb, n_sel):
    cur = pl.program_id(2)
    cos2, sin2 = cos_ref[...], sin_ref[...]
    blk = _iota((nb, t), 0)
    blk_f = blk.astype(F32)
    qts = []
    for r in range(nh):
        qn = _rms_rope(q_ref[0, :, r * HEAD_DIM:(r + 1) * HEAD_DIM], qn_ref[...], cos2, sin2)
        qnt = qn.T
        gate = jnp.dot(kmean_ref[0, r], qnt, precision=HIGHEST, preferred_element_type=F32)
        g = jnp.where(blk < cur, gate, NEG)
        picked = jnp.zeros((nb, t), F32)
        for _ in range(n_sel):
            mx = jnp.max(g, axis=0, keepdims=True)
            first = jnp.min(jnp.where(g == mx, blk_f, float(nb)), axis=0, keepdims=True)
            hit = blk_f == first
            picked = jnp.where(hit, 1.0, picked)
            g = jnp.where(hit, LOWEST, g)
        attend = jnp.where(blk < cur, picked, jnp.where(blk == cur, 1.0, 0.0))
        bias = jnp.where(attend > 0.0, 0.0, NEG)
        qts.append(jnp.concatenate([qnt * (ATTN_SCALE * LOG2E), bias, jnp.zeros((HEAD_DIM - nb, t), F32)],
                                   axis=0).astype(BF16))
    states = _attend_causal(qts, k_ref, vt_ref, cur, t, False)
    _store_heads(o_ref, states, None, None)


def _moba_attn(h, q_norm, cos2, sin2, kmean, k_aug, v_t):
    B, S, _ = h.shape
    t = MOBA_BLOCK
    assert t == ATT_TILE
    nb = S // MOBA_BLOCK
    nh = HEADS_PER_STEP
    n_sel = min(MOBA_TOPK, nb - 1)
    tab = pl.BlockSpec((t, HEAD_DIM), lambda b, hh, i: (i, 0))
    return pl.pallas_call(
        functools.partial(_moba_attn_body, t=t, nh=nh, nb=nb, n_sel=n_sel),
        grid=(B, N_HEADS // nh, nb),
        in_specs=[pl.BlockSpec((1, t, nh * HEAD_DIM), lambda b, hh, i: (b, i, hh)),
                  pl.BlockSpec((1, HEAD_DIM), lambda b, hh, i: (0, 0)), tab, tab,
                  pl.BlockSpec((1, nh, nb, HEAD_DIM), lambda b, hh, i: (b, hh, 0, 0)),
                  pl.BlockSpec((1, S, nh * AUG), lambda b, hh, i: (b, 0, hh)),
                  pl.BlockSpec((1, nh, nb, HEAD_DIM, t), lambda b, hh, i: (b, hh, 0, 0, 0))],
        out_specs=pl.BlockSpec((1, t, nh * HEAD_DIM), lambda b, hh, i: (b, i, hh)),
        out_shape=jax.ShapeDtypeStruct((B, S, WIDTH), BF16),
        compiler_params=_params("parallel", "parallel", "arbitrary"),
        name="moba_attn",
    )(h, q_norm.reshape(1, HEAD_DIM), cos2, sin2, kmean, k_aug, v_t)


def _moba_layer(x, norm, w_in, q_norm, k_norm, w_out, cos2, sin2):
    B, S, D = x.shape
    x2 = x.reshape(B * S, D)
    h2 = _in_proj(x2, norm, w_in.astype(BF16), F32, 512, 1024)
    h = h2.reshape(B, S, -1)
    k_aug, v_t, kmean = _moba_prep_k(h, k_norm, cos2, sin2)
    o = _moba_attn(h, q_norm, cos2, sin2, kmean, k_aug, v_t)
    out = _out_proj(x2, h2, 3, w_out.astype(BF16), [o.reshape(B * S, WIDTH)])
    return out.reshape(B, S, D)


def _split2_dot(a, u2):
    hi = a.astype(BF16)
    lo = (a - hi.astype(F32)).astype(BF16)
    return jnp.dot(jnp.concatenate([hi, lo], axis=1), u2, preferred_element_type=F32)


def _sb_attn_body(q_ref, k_ref, v_ref, o_ref, kt_ref, *, t, nh):
    qi = pl.program_id(2)

    @pl.when(qi == 0)
    def _():
        for r in range(nh):
            _store_tiles_t(kt_ref, (r,), k_ref[0, :, r * HEAD_DIM:(r + 1) * HEAD_DIM].astype(F32), t)

    after = jnp.where(_iota((t, t), 0) > _iota((t, t), 1), 1.0, 0.0).astype(BF16)
    after2 = jnp.concatenate([after, after], axis=0)
    qs = [(q_ref[0, :, r * HEAD_DIM:(r + 1) * HEAD_DIM].astype(F32) * ATTN_SCALE).astype(BF16)
          for r in range(nh)]

    def block(j0, states, valid):
        zs = [jnp.dot(qs[r], jnp.concatenate([kt_ref[r, j0], kt_ref[r, j0 + 1]], axis=1),
                      preferred_element_type=F32) for r in range(nh)]
        staged = []
        for r in range(nh):
            z = zs[r]
            log_keep = -(jnp.maximum(z, 0.0) + jnp.log(1.0 + jnp.exp(-jnp.abs(z))))
            if valid is not None:
                log_keep = jnp.where(valid, log_keep, 0.0)
            tiles = []
            for lk in (log_keep[:, :t], log_keep[:, t:]):
                suffix = _split2_dot(lk, after2)
                tiles.append((suffix, suffix[:, 0:1] + lk[:, 0:1]))
            staged.append((z + log_keep, tiles))
        out = []
        for r in range(nh):
            carry, acc = states[r]
            log_beta, ((suf_old, sum_old), (suf_new, sum_new)) = staged[r]
            v2 = v_ref[0, pl.ds(pl.multiple_of(j0 * t, t), 2 * t), r * HEAD_DIM:(r + 1) * HEAD_DIM]
            carry_old = carry + sum_new
            a = jnp.exp(log_beta + jnp.concatenate([suf_old + carry_old, suf_new + carry], axis=1))
            if valid is not None:
                a = jnp.where(valid, a, 0.0)
            acc = acc + jnp.dot(a.astype(BF16), v2, preferred_element_type=F32)
            out.append((carry_old + sum_old, acc))
        return tuple(out)

    j0 = jnp.maximum(qi - 1, 0)
    key = _iota((t, 2 * t), 1)
    valid = (key < _iota((t, 2 * t), 0) + (qi - j0) * t) & (key >= (qi - (qi & 1) - j0) * t)
    init = tuple((jnp.zeros((t, 1), F32), jnp.zeros((t, HEAD_DIM), F32)) for _ in range(nh))
    states = block(j0, init, valid)
    n_pairs = qi >> 1
    states = lax.fori_loop(0, n_pairs, lambda i, st: block(2 * (n_pairs - 1 - i), st, None), states)
    for r in range(nh):
        o_ref[0, :, r * HEAD_DIM:(r + 1) * HEAD_DIM] = states[r][1].astype(o_ref.dtype)


def _sb_attn(h):
    B, S, _ = h.shape
    t = ATT_TILE
    nh = HEADS_PER_STEP
    nhb = N_HEADS // nh
    w = nh * HEAD_DIM
    return pl.pallas_call(
        functools.partial(_sb_attn_body, t=t, nh=nh),
        grid=(B, nhb, S // t),
        in_specs=[pl.BlockSpec((1, t, w), lambda b, hh, i: (b, i, hh)),
                  pl.BlockSpec((1, S, w), lambda b, hh, i: (b, 0, nhb + hh)),
                  pl.BlockSpec((1, S, w), lambda b, hh, i: (b, 0, 2 * nhb + hh))],
        out_specs=pl.BlockSpec((1, t, w), lambda b, hh, i: (b, i, hh)),
        out_shape=jax.ShapeDtypeStruct((B, S, WIDTH), BF16),
        scratch_shapes=[pltpu.VMEM((nh, S // t, HEAD_DIM, t), BF16)],
        compiler_params=_params("parallel", "parallel", "arbitrary"),
        name="sb_attn",
    )(h, h, h)


def _sb_layer(x, norm, w_in, w_out):
    B, S, D = x.shape
    x2 = x.reshape(B * S, D)
    h2 = _in_proj(x2, norm, w_in.astype(BF16), BF16, 512, 1024)
    o = _sb_attn(h2.reshape(B, S, -1))
    out = _out_proj(x2, h2, 3, w_out.astype(BF16), [o.reshape(B * S, WIDTH)])
    return out.reshape(B, S, D)


NSA_KV_BLK0 = 2 * N_HEADS


def _nsa_compress_body(kc_ref, vc_ref, wk_ref, wv_ref, pos_ref, kcn_ref, cosc_ref, sinc_ref,
                       kcc_ref, vcc_ref, *, n_rows):
    half = CMP_BLOCK // CMP_STRIDE
    assert half == 2

    def compress(src_ref, w_ref):
        first = jnp.zeros((n_rows, HEAD_DIM), F32)
        second = jnp.zeros((n_rows, HEAD_DIM), F32)
        for l in range(CMP_STRIDE):
            rows = src_ref[0, pl.ds(l, n_rows, stride=CMP_STRIDE), :]
            first = first + jnp.dot(rows + pos_ref[l:l + 1, :], w_ref[l],
                                    precision=HIGHEST, preferred_element_type=F32)
            second = second + jnp.dot(rows + pos_ref[CMP_STRIDE + l:CMP_STRIDE + l + 1, :],
                                      w_ref[CMP_STRIDE + l], precision=HIGHEST, preferred_element_type=F32)
        return first + pltpu.roll(second, n_rows - 1, 0)

    kc = compress(kc_ref, wk_ref)
    kcc_ref[0, 0] = _rms_rope(kc, kcn_ref[...], cosc_ref[...], sinc_ref[...])
    vcc_ref[0, 0] = compress(vc_ref, wv_ref)


def _nsa_compress(h, cmp_wk, cmp_wv, cmp_pos, kc_norm, cos_c, sin_c):
    B, S, _ = h.shape
    n_rows = S // CMP_STRIDE
    wspec = pl.BlockSpec((CMP_BLOCK, HEAD_DIM, HEAD_DIM), lambda b, g: (0, 0, 0))
    tab = pl.BlockSpec((n_rows, HEAD_DIM), lambda b, g: (0, 0))
    out = pl.BlockSpec((1, 1, n_rows, HEAD_DIM), lambda b, g: (b, g, 0, 0))
    return pl.pallas_call(
        functools.partial(_nsa_compress_body, n_rows=n_rows),
        grid=(B, NSA_KV_HEADS),
        in_specs=[pl.BlockSpec((1, S, HEAD_DIM), lambda b, g: (b, 0, NSA_KV_BLK0 + g)),
                  pl.BlockSpec((1, S, HEAD_DIM), lambda b, g: (b, 0, NSA_KV_BLK0 + NSA_KV_HEADS + g)),
                  wspec, wspec,
                  pl.BlockSpec((CMP_BLOCK, HEAD_DIM), lambda b, g: (0, 0)),
                  pl.BlockSpec((1, HEAD_DIM), lambda b, g: (0, 0)), tab, tab],
        out_specs=[out, out],
        out_shape=[jax.ShapeDtypeStruct((B, NSA_KV_HEADS, n_rows, HEAD_DIM), F32)] * 2,
        compiler_params=_params("parallel", "parallel"),
        name="nsa_compress",
    )(h, h, cmp_wk, cmp_wv, cmp_pos, kc_norm.reshape(1, HEAD_DIM), cos_c, sin_c)


def _nsa_prep_kv_body(ks_ref, vs_ref, kw_ref, vw_ref, ksn_ref, kwn_ref, cos_ref, sin_ref,
                      ksaug_ref, vst_ref, kwb_ref, vwt_ref, *, t):
    cos2, sin2 = cos_ref[...], sin_ref[...]
    S = cos2.shape[0]
    ksaug_ref[0, :, :HEAD_DIM] = _rms_rope(ks_ref[0], ksn_ref[...], cos2, sin2).astype(BF16)
    ksaug_ref[0, :, HEAD_DIM:] = _onehot_block(S, _log2(SLC_BLOCK))
    kwb_ref[0] = _rms_rope(kw_ref[0], kwn_ref[...], cos2, sin2).astype(BF16)
    _store_tiles_t(vst_ref, (0, 0), vs_ref[0], t)
    _store_tiles_t(vwt_ref, (0, 0), vw_ref[0], t)


def _nsa_prep_kv(h, ks_norm, kw_norm, cos2, sin2):
    B, S, _ = h.shape
    G = NSA_KV_HEADS
    t = ATT_TILE

    def src(i):
        return pl.BlockSpec((1, S, HEAD_DIM), lambda b, g: (b, 0, NSA_KV_BLK0 + i * G + g))

    vec = pl.BlockSpec((1, HEAD_DIM), lambda b, g: (0, 0))
    full = pl.BlockSpec((S, HEAD_DIM), lambda b, g: (0, 0))
    vt = pl.BlockSpec((1, 1, S // t, HEAD_DIM, t), lambda b, g: (b, g, 0, 0, 0))
    vt_shape = jax.ShapeDtypeStruct((B, G, S // t, HEAD_DIM, t), BF16)
    return pl.pallas_call(
        functools.partial(_nsa_prep_kv_body, t=t),
        grid=(B, G),
        in_specs=[src(2), src(3), src(4), src(5), vec, vec, full, full],
        out_specs=[pl.BlockSpec((1, S, AUG), lambda b, g: (b, 0, g)), vt,
                   pl.BlockSpec((1, S, HEAD_DIM), lambda b, g: (b, 0, g)), vt],
        out_shape=[jax.ShapeDtypeStruct((B, S, G * AUG), BF16), vt_shape,
                   jax.ShapeDtypeStruct((B, S, KV_WIDTH), BF16), vt_shape],
        compiler_params=_params("parallel", "parallel"),
        name="nsa_prep_kv",
    )(h, h, h, h, ks_norm.reshape(1, HEAD_DIM), kw_norm.reshape(1, HEAD_DIM), cos2, sin2)


def _nsa_cmp_sel_body(q_ref, qn_ref, cos_ref, sin_ref, kcc_ref, vcc_ref, bg_ref, ocmp_ref, qt_ref,
                      *, t, n_cmp, n_slc, n_sel):
    g_idx = pl.program_id(1)
    q0 = pl.program_id(2) * t
    cos2, sin2 = cos_ref[...], sin_ref[...]
    kcc = kcc_ref[0, 0]
    vcc = vcc_ref[0, 0].astype(BF16)
    tpos = q0 + _iota((t, n_cmp), 0)
    cmp_ok = (_iota((t, n_cmp), 1) * CMP_STRIDE + (CMP_BLOCK - 1)) <= tpos

    p_sum = jnp.zeros((t, n_cmp), F32)
    for r in range(NSA_HPG):
        qn = _rms_rope(q_ref[0, :, r * HEAD_DIM:(r + 1) * HEAD_DIM], qn_ref[...], cos2, sin2)
        qt_ref[0, r, 0, :HEAD_DIM, :] = (qn * (ATTN_SCALE * LOG2E)).T.astype(BF16)
        logits = lax.dot_general(qn, kcc, _NT, precision=HIGHEST, preferred_element_type=F32) * ATTN_SCALE
        logits = jnp.where(cmp_ok, logits, NEG)
        mx = jnp.max(logits, axis=1, keepdims=True)
        e = jnp.where(cmp_ok, jnp.exp(logits - mx), 0.0)
        p = e / jnp.maximum(jnp.sum(e, axis=1, keepdims=True), TINY)
        p_sum = p_sum + p
        o = jnp.dot(p.astype(BF16), vcc, preferred_element_type=F32)
        o = o * _branch_gate(bg_ref, g_idx * NSA_HPG + r)
        ocmp_ref[0, :, r * HEAD_DIM:(r + 1) * HEAD_DIM] = o.astype(ocmp_ref.dtype)

    c_start = _iota((n_cmp, LANES), 0) * CMP_STRIDE
    s_start = _iota((n_cmp, LANES), 1) * SLC_BLOCK
    overlap = jnp.where((c_start < s_start + SLC_BLOCK) & (c_start + CMP_BLOCK > s_start), 1.0, 0.0)
    imp = jnp.dot(p_sum, overlap, precision=HIGHEST, preferred_element_type=F32)
    lane = _iota((t, LANES), 1)
    cur = (q0 + _iota((t, LANES), 0)) >> _log2(SLC_BLOCK)
    forced = (lane == 0) | (lane == cur) | (lane == cur - 1)
    imp = jnp.where(forced, FORCE, imp)
    imp = jnp.where(lane <= cur, imp, NEG)
    imp = jnp.where(lane < n_slc, imp, LOWEST)
    rank = jnp.zeros((t, LANES), F32)
    for k in range(n_slc):
        c = imp[:, k:k + 1]
        rank = rank + jnp.where(lane > k, jnp.where(c >= imp, 1.0, 0.0), jnp.where(c > imp, 1.0, 0.0))
    bias_t = jnp.where(rank < float(n_sel), 0.0, NEG).T.astype(BF16)
    for r in range(NSA_HPG):
        qt_ref[0, r, 0, HEAD_DIM:, :] = bias_t


def _nsa_cmp_sel(h, bg, q_norm, cos2, sin2, kc_c, vc_c):
    B, S, _ = h.shape
    t = ATT_TILE
    G, R = NSA_KV_HEADS, NSA_HPG
    n_cmp = S // CMP_STRIDE
    n_slc = S // SLC_BLOCK
    assert n_slc <= LANES
    tab = pl.BlockSpec((t, HEAD_DIM), lambda b, g, i: (i, 0))
    cmp = pl.BlockSpec((1, 1, n_cmp, HEAD_DIM), lambda b, g, i: (b, g, 0, 0))
    return pl.pallas_call(
        functools.partial(_nsa_cmp_sel_body, t=t, n_cmp=n_cmp, n_slc=n_slc, n_sel=min(SLC_TOPK, n_slc)),
        grid=(B, G, S // t),
        in_specs=[pl.BlockSpec((1, t, R * HEAD_DIM), lambda b, g, i: (b, i, g)),
                  pl.BlockSpec((1, HEAD_DIM), lambda b, g, i: (0, 0)), tab, tab, cmp, cmp,
                  pl.BlockSpec((1, t, LANES), lambda b, g, i: (b, i, 0))],
        out_specs=[pl.BlockSpec((1, t, R * HEAD_DIM), lambda b, g, i: (b, i, g)),
                   pl.BlockSpec((1, R, 1, AUG, t), lambda b, g, i: (b, g, i, 0, 0))],
        out_shape=[jax.ShapeDtypeStruct((B, S, WIDTH), BF16),
                   jax.ShapeDtypeStruct((B, N_HEADS, S // t, AUG, t), BF16)],
        compiler_params=_params("parallel", "parallel", "arbitrary"),
        name="nsa_cmp_sel",
    )(h, q_norm.reshape(1, HEAD_DIM), cos2, sin2, kc_c, vc_c, bg)


def _win_attn_body(qt_ref, k_ref, vt_ref, bg_ref, o_ref, *, t, nh, n_back, gate_base):
    qi = pl.program_id(2)
    n_blk = n_back + 1
    j0 = jnp.maximum(qi - n_back, 0)
    k = k_ref[0, pl.ds(pl.multiple_of(j0 * t, t), n_blk * t), :]
    vt = jnp.concatenate([vt_ref[0, 0, j0 + d] for d in range(n_blk)], axis=1)
    dist = _iota((n_blk * t, t), 1) - _iota((n_blk * t, t), 0) + (qi - j0) * t
    ss = [jnp.dot(k, qt_ref[0, r, 0], preferred_element_type=F32) for r in range(nh)]
    states = []
    for r in range(nh):
        s = jnp.where(dist >= 0, ss[r], NEG)
        s = jnp.where(dist < WINDOW, s, NEG)
        m = jnp.max(s, axis=0, keepdims=True)
        p = jnp.exp2(s - m)
        states.append((m, jnp.sum(p, axis=0, keepdims=True), jnp.dot(vt, p.astype(BF16), preferred_element_type=F32)))
    _store_heads(o_ref, states, bg_ref, gate_base + pl.program_id(1) * nh)


def _win_attn(q_t, kw, vw_t, bg, gate_base):
    B, H, n_t, _, t = q_t.shape
    S = n_t * t
    nh = NSA_HPG
    assert WINDOW % t == 0 and n_t > WINDOW // t
    return pl.pallas_call(
        functools.partial(_win_attn_body, t=t, nh=nh, n_back=WINDOW // t, gate_base=gate_base),
        grid=(B, H // nh, n_t),
        in_specs=[pl.BlockSpec((1, nh, 1, HEAD_DIM, t), lambda b, g, i: (b, g, i, 0, 0)),
                  pl.BlockSpec((1, S, HEAD_DIM), lambda b, g, i: (b, 0, g)),
                  pl.BlockSpec((1, 1, n_t, HEAD_DIM, t), lambda b, g, i: (b, g, 0, 0, 0)),
                  pl.BlockSpec((1, t, LANES), lambda b, g, i: (b, i, 0))],
        out_specs=pl.BlockSpec((1, t, nh * HEAD_DIM), lambda b, g, i: (b, i, g)),
        out_shape=jax.ShapeDtypeStruct((B, S, WIDTH), BF16),
        compiler_params=_params("parallel", "parallel", "arbitrary"),
        name="win_attn",
    )(q_t, kw, vw_t, bg)


def _nsa_layer(x, norm, w_in, q_norm, kc_norm, ks_norm, kw_norm, cmp_wk, cmp_wv, cmp_pos, w_out, cos2, sin2):
    B, S, D = x.shape
    x2 = x.reshape(B * S, D)
    kv_end = WIDTH + 6 * KV_WIDTH
    gate_end = kv_end + WIDTH
    w_main = jnp.concatenate([w_in[:, :WIDTH], w_in[:, kv_end:gate_end], w_in[:, WIDTH:kv_end]], axis=1)
    w_bg = jnp.pad(w_in[:, gate_end:], ((0, 0), (0, LANES - 3 * N_HEADS)))
    h2 = _in_proj(x2, norm, w_main.astype(BF16), F32, 512, 1024)
    bg = _in_proj(x2, norm, w_bg.astype(BF16), F32, 512, LANES).reshape(B, S, LANES)
    h = h2.reshape(B, S, -1)
    cos_c, sin_c = _rope_tables(jnp.arange(S // CMP_STRIDE) * CMP_STRIDE + (CMP_BLOCK - 1))
    kc_c, vc_c = _nsa_compress(h, cmp_wk, cmp_wv, cmp_pos, kc_norm, cos_c, sin_c)
    ks_aug, vs_t, kw_b, vw_t = _nsa_prep_kv(h, ks_norm, kw_norm, cos2, sin2)
    o_cmp, q_t = _nsa_cmp_sel(h, bg, q_norm, cos2, sin2, kc_c, vc_c)
    o_slc = _sel_attn(q_t, ks_aug, vs_t, bg, N_HEADS)
    o_win = _win_attn(q_t, kw_b, vw_t, bg, 2 * N_HEADS)
    os = [o.reshape(B * S, WIDTH) for o in (o_cmp, o_slc, o_win)]
    out = _out_proj(x2, h2, 1, w_out.astype(BF16), os)
    return out.reshape(B, S, D)


def kernel(x, l0_norm, l0_w_in, l0_q_norm, l0_k_norm, l0_w_out, l1_norm, l1_w_in, l1_w_out, l2_norm, l2_w_in, l2_q_norm, l2_kc_norm, l2_ks_norm, l2_kw_norm, l2_cmp_wk, l2_cmp_wv, l2_cmp_pos, l2_w_out, l3_norm, l3_w_in, l3_q_norm, l3_k_norm, l3_w_out):
    cos2, sin2 = _rope_tables(jnp.arange(x.shape[1]))
    x = _moba_layer(x, l0_norm, l0_w_in, l0_q_norm, l0_k_norm, l0_w_out, cos2, sin2)
    x = _sb_layer(x, l1_norm, l1_w_in, l1_w_out)
    x = _nsa_layer(x, l2_norm, l2_w_in, l2_q_norm, l2_kc_norm, l2_ks_norm, l2_kw_norm,
                   l2_cmp_wk, l2_cmp_wv, l2_cmp_pos, l2_w_out, cos2, sin2)
    x = _moba_layer(x, l3_norm, l3_w_in, l3_q_norm, l3_k_norm, l3_w_out, cos2, sin2)
    return x
```

```python
import functools
import math

import jax
import jax.numpy as jnp
from jax import lax
from jax.experimental import pallas as pl
from jax.experimental.pallas import tpu as pltpu

F32 = jnp.float32
BF16 = jnp.bfloat16
HIGHEST = lax.Precision.HIGHEST

D_MODEL = 2048
HEAD_DIM = 128
N_HEADS = 16
WIDTH = N_HEADS * HEAD_DIM
ROPE_THETA = 10000.0
EPS = 1e-6
ATTN_SCALE = HEAD_DIM ** -0.5
LOG2E = math.log2(math.e)
NEG = -1e30
LOWEST = -3e38
FORCE = 1e30
TINY = 1e-30
MOBA_BLOCK = 256
MOBA_TOPK = 3
NSA_KV_HEADS = 4
NSA_HPG = N_HEADS // NSA_KV_HEADS
KV_WIDTH = NSA_KV_HEADS * HEAD_DIM
CMP_BLOCK = 32
CMP_STRIDE = 16
SLC_BLOCK = 64
SLC_TOPK = 16
WINDOW = 512

LANES = 128
AUG = 2 * HEAD_DIM
ATT_TILE = 256
HEADS_PER_STEP = 4
VMEM_LIMIT = 48 * 1024 * 1024

_NT = (((1,), (1,)), ((), ()))


def _params(*sem):
    return pltpu.CompilerParams(dimension_semantics=sem, vmem_limit_bytes=VMEM_LIMIT)


def _rope_tables(pos):
    half = HEAD_DIM // 2
    inv_freq = jnp.exp(-math.log(ROPE_THETA) * jnp.arange(half, dtype=F32) / half)
    ang = pos.astype(F32)[:, None] * inv_freq[None, :]
    cos, sin = jnp.cos(ang), jnp.sin(ang)
    return jnp.concatenate([cos, cos], axis=-1), jnp.concatenate([-sin, sin], axis=-1)


def _rms_rope(x, g, cos2, sin2):
    y = x * lax.rsqrt(jnp.mean(x * x, axis=-1, keepdims=True) + EPS) * g
    return y * cos2 + pltpu.roll(y, HEAD_DIM // 2, 1) * sin2


def _iota(shape, dim):
    return lax.broadcasted_iota(jnp.int32, shape, dim)


def _log2(n):
    s = n.bit_length() - 1
    assert 1 << s == n
    return s


def _store_tiles_t(dst_ref, lead, x, t):
    for j in range(x.shape[0] // t):
        dst_ref[lead + (j,)] = x[j * t:(j + 1) * t, :].T.astype(dst_ref.dtype)


def _in_proj_body(x_ref, g_ref, w_ref, o_ref, xn_ref):
    @pl.when(pl.program_id(1) == 0)
    def _():
        x = x_ref[...]
        r = lax.rsqrt(jnp.mean(x * x, axis=-1, keepdims=True) + EPS)
        xn_ref[...] = (x * r * g_ref[...]).astype(xn_ref.dtype)

    o_ref[...] = jnp.dot(xn_ref[...], w_ref[...], preferred_element_type=F32).astype(o_ref.dtype)


def _in_proj(x2, g, w, out_dtype, tm, tn):
    M, K = x2.shape
    N = w.shape[1]
    return pl.pallas_call(
        _in_proj_body,
        grid=(M // tm, N // tn),
        in_specs=[pl.BlockSpec((tm, K), lambda i, j: (i, 0)),
                  pl.BlockSpec((1, K), lambda i, j: (0, 0)),
                  pl.BlockSpec((K, tn), lambda i, j: (0, j))],
        out_specs=pl.BlockSpec((tm, tn), lambda i, j: (i, j)),
        out_shape=jax.ShapeDtypeStruct((M, N), out_dtype),
        scratch_shapes=[pltpu.VMEM((tm, K), BF16)],
        compiler_params=_params("parallel", "arbitrary"),
        name="in_proj",
    )(x2, g.reshape(1, K), w)


def _out_proj_body(*refs, n_o):
    x_ref, gate_ref, w_ref = refs[0], refs[1], refs[2]
    o_refs = refs[3:3 + n_o]
    out_ref = refs[3 + n_o]
    o = o_refs[0][...].astype(F32)
    for r in o_refs[1:]:
        o = o + r[...].astype(F32)
    gate = gate_ref[...].astype(F32)
    y = o * (gate / (1.0 + jnp.exp(-gate)))
    out_ref[...] = x_ref[...] + jnp.dot(y.astype(BF16), w_ref[...], preferred_element_type=F32)


def _out_proj(x2, h2, gate_blk, w, os, tm=256):
    M, D = x2.shape
    n_o = len(os)
    row = pl.BlockSpec((tm, D), lambda i: (i, 0))
    return pl.pallas_call(
        functools.partial(_out_proj_body, n_o=n_o),
        grid=(M // tm,),
        in_specs=[row, pl.BlockSpec((tm, D), lambda i: (i, gate_blk)),
                  pl.BlockSpec((D, D), lambda i: (0, 0))] + [row] * n_o,
        out_specs=row,
        out_shape=jax.ShapeDtypeStruct((M, D), F32),
        compiler_params=_params("parallel"),
        name="out_proj",
    )(x2, h2, w, *os)


def _branch_gate(bg_ref, col):
    bg = bg_ref[0]
    g = 1.0 / (1.0 + jnp.exp(-bg))
    return jnp.sum(jnp.where(_iota(bg.shape, 1) == col, g, 0.0), axis=1, keepdims=True)


def _softmax_step(state, s, vt):
    m, l, acc = state
    m_new = jnp.maximum(m, jnp.max(s, axis=0, keepdims=True))
    alpha = jnp.exp2(m - m_new)
    p = jnp.exp2(s - m_new)
    l = alpha * l + jnp.sum(p, axis=0, keepdims=True)
    acc = alpha * acc + jnp.dot(vt, p.astype(BF16), preferred_element_type=F32)
    return m_new, l, acc


def _softmax_init(t):
    return jnp.full((1, t), NEG, F32), jnp.zeros((1, t), F32), jnp.zeros((HEAD_DIM, t), F32)


def _store_heads(o_ref, states, bg_ref, gate_col0):
    for r, (_, l, acc) in enumerate(states):
        o = (acc * (1.0 / l)).T
        if bg_ref is not None:
            o = o * _branch_gate(bg_ref, gate_col0 + r)
        o_ref[0, :, r * HEAD_DIM:(r + 1) * HEAD_DIM] = o.astype(o_ref.dtype)


def _attend_causal(qts, k_ref, vt_ref, qi, t, shared):
    nh = len(qts)

    def block(j0, n, states, diag):
        off = pl.multiple_of(j0 * t, t)
        ss = [jnp.dot(k_ref[0, pl.ds(off, n * t), (0 if shared else r) * AUG:(1 if shared else r + 1) * AUG],
                      qts[r], preferred_element_type=F32) for r in range(nh)]
        if diag:
            causal = _iota((n * t, t), 0) <= _iota((n * t, t), 1) + (n - 1) * t
        out = []
        for r in range(nh):
            s = jnp.where(causal, ss[r], NEG) if diag else ss[r]
            c = 0 if shared else r
            vt = jnp.concatenate([vt_ref[0, c, j0 + i] for i in range(n)], axis=1)
            out.append(_softmax_step(states[r], s, vt))
        return tuple(out)

    init = tuple(_softmax_init(t) for _ in range(nh))
    states = lax.cond((qi & 1) == 1, lambda: block(qi - 1, 2, init, True), lambda: block(qi, 1, init, True))
    return lax.fori_loop(0, qi >> 1, lambda j2, st: block(2 * j2, 2, st, False), states)


def _sel_attn_body(qt_ref, k_ref, vt_ref, bg_ref, o_ref, *, t, nh, gate_base):
    qts = [qt_ref[0, r, 0] for r in range(nh)]
    states = _attend_causal(qts, k_ref, vt_ref, pl.program_id(2), t, True)
    _store_heads(o_ref, states, bg_ref, gate_base + pl.program_id(1) * nh)


def _sel_attn(q_t, k_aug, v_t, bg, gate_base):
    B, H, n_t, _, t = q_t.shape
    S = n_t * t
    nh = NSA_HPG
    assert n_t >= 2
    return pl.pallas_call(
        functools.partial(_sel_attn_body, t=t, nh=nh, gate_base=gate_base),
        grid=(B, H // nh, n_t),
        in_specs=[pl.BlockSpec((1, nh, 1, AUG, t), lambda b, h, i: (b, h, i, 0, 0)),
                  pl.BlockSpec((1, S, AUG), lambda b, h, i: (b, 0, h)),
                  pl.BlockSpec((1, 1, n_t, HEAD_DIM, t), lambda b, h, i: (b, h, 0, 0, 0)),
                  pl.BlockSpec((1, t, LANES), lambda b, h, i: (b, i, 0))],
        out_specs=pl.BlockSpec((1, t, nh * HEAD_DIM), lambda b, h, i: (b, i, h)),
        out_shape=jax.ShapeDtypeStruct((B, S, WIDTH), BF16),
        compiler_params=_params("parallel", "parallel", "arbitrary"),
        name="sel_attn",
    )(q_t, k_aug, v_t, bg)


def _onehot_block(S, shift):
    return jnp.where((_iota((S, LANES), 0) >> shift) == _iota((S, LANES), 1), 1.0, 0.0).astype(BF16)


def _moba_prep_k_body(k_ref, v_ref, kn_ref, cos_ref, sin_ref, kaug_ref, vt_ref, kmean_ref, *, nb, t):
    kn = _rms_rope(k_ref[0], kn_ref[...], cos_ref[...], sin_ref[...])
    S = kn.shape[0]
    kaug_ref[0, :, :HEAD_DIM] = kn.astype(BF16)
    kaug_ref[0, :, HEAD_DIM:] = _onehot_block(S, _log2(MOBA_BLOCK))
    _store_tiles_t(vt_ref, (0, 0), v_ref[0], t)
    kmean_ref[0, 0] = jnp.mean(kn.reshape(nb, MOBA_BLOCK, HEAD_DIM), axis=1)


def _moba_prep_k(h, k_norm, cos2, sin2):
    B, S, _ = h.shape
    nb = S // MOBA_BLOCK
    t = ATT_TILE
    assert 1 < nb <= HEAD_DIM
    full = pl.BlockSpec((S, HEAD_DIM), lambda b, hh: (0, 0))
    return pl.pallas_call(
        functools.partial(_moba_prep_k_body, nb=nb, t=t),
        grid=(B, N_HEADS),
        in_specs=[pl.BlockSpec((1, S, HEAD_DIM), lambda b, hh: (b, 0, N_HEADS + hh)),
                  pl.BlockSpec((1, S, HEAD_DIM), lambda b, hh: (b, 0, 2 * N_HEADS + hh)),
                  pl.BlockSpec((1, HEAD_DIM), lambda b, hh: (0, 0)), full, full],
        out_specs=[pl.BlockSpec((1, S, AUG), lambda b, hh: (b, 0, hh)),
                   pl.BlockSpec((1, 1, S // t, HEAD_DIM, t), lambda b, hh: (b, hh, 0, 0, 0)),
                   pl.BlockSpec((1, 1, nb, HEAD_DIM), lambda b, hh: (b, hh, 0, 0))],
        out_shape=[jax.ShapeDtypeStruct((B, S, N_HEADS * AUG), BF16),
                   jax.ShapeDtypeStruct((B, N_HEADS, S // t, HEAD_DIM, t), BF16),
                   jax.ShapeDtypeStruct((B, N_HEADS, nb, HEAD_DIM), F32)],
        compiler_params=_params("parallel", "parallel"),
        name="moba_prep_k",
    )(h, h, k_norm.reshape(1, HEAD_DIM), cos2, sin2)


def _moba_attn_body(q_ref, qn_ref, cos_ref, sin_ref, kmean_ref, k_ref, vt_ref, o_ref, *, t, nh, nb, n_sel):
    cur = pl.program_id(2)
    cos2, sin2 = cos_ref[...], sin_ref[...]
    blk = _iota((nb, t), 0)
    blk_f = blk.astype(F32)
    qts = []
    for r in range(nh):
        qn = _rms_rope(q_ref[0, :, r * HEAD_DIM:(r + 1) * HEAD_DIM], qn_ref[...], cos2, sin2)
        qnt = qn.T
        gate = jnp.dot(kmean_ref[0, r], qnt, precision=HIGHEST, preferred_element_type=F32)
        g = jnp.where(blk < cur, gate, NEG)
        picked = jnp.zeros((nb, t), F32)
        for _ in range(n_sel):
            mx = jnp.max(g, axis=0, keepdims=True)
            first = jnp.min(jnp.where(g == mx, blk_f, float(nb)), axis=0, keepdims=True)
            hit = blk_f == first
            picked = jnp.where(hit, 1.0, picked)
            g = jnp.where(hit, LOWEST, g)
        attend = jnp.where(blk < cur, picked, jnp.where(blk == cur, 1.0, 0.0))
        bias = jnp.where(attend > 0.0, 0.0, NEG)
        qts.append(jnp.concatenate([qnt * (ATTN_SCALE * LOG2E), bias, jnp.zeros((HEAD_DIM - nb, t), F32)],
                                   axis=0).astype(BF16))
    states = _attend_causal(qts, k_ref, vt_ref, cur, t, False)
    _store_heads(o_ref, states, None, None)


def _moba_attn(h, q_norm, cos2, sin2, kmean, k_aug, v_t):
    B, S, _ = h.shape
    t = MOBA_BLOCK
    assert t == ATT_TILE
    nb = S // MOBA_BLOCK
    nh = HEADS_PER_STEP
    n_sel = min(MOBA_TOPK, nb - 1)
    tab = pl.BlockSpec((t, HEAD_DIM), lambda b, hh, i: (i, 0))
    return pl.pallas_call(
        functools.partial(_moba_attn_body, t=t, nh=nh, nb=nb, n_sel=n_sel),
        grid=(B, N_HEADS // nh, nb),
        in_specs=[pl.BlockSpec((1, t, nh * HEAD_DIM), lambda b, hh, i: (b, i, hh)),
                  pl.BlockSpec((1, HEAD_DIM), lambda b, hh, i: (0, 0)), tab, tab,
                  pl.BlockSpec((1, nh, nb, HEAD_DIM), lambda b, hh, i: (b, hh, 0, 0)),
                  pl.BlockSpec((1, S, nh * AUG), lambda b, hh, i: (b, 0, hh)),
                  pl.BlockSpec((1, nh, nb, HEAD_DIM, t), lambda b, hh, i: (b, hh, 0, 0, 0))],
        out_specs=pl.BlockSpec((1, t, nh * HEAD_DIM), lambda b, hh, i: (b, i, hh)),
        out_shape=jax.ShapeDtypeStruct((B, S, WIDTH), BF16),
        compiler_params=_params("parallel", "parallel", "arbitrary"),
        name="moba_attn",
    )(h, q_norm.reshape(1, HEAD_DIM), cos2, sin2, kmean, k_aug, v_t)


def _moba_layer(x, norm, w_in, q_norm, k_norm, w_out, cos2, sin2):
    B, S, D = x.shape
    x2 = x.reshape(B * S, D)
    h2 = _in_proj(x2, norm, w_in.astype(BF16), F32, 1024, 1024)
    h = h2.reshape(B, S, -1)
    k_aug, v_t, kmean = _moba_prep_k(h, k_norm, cos2, sin2)
    o = _moba_attn(h, q_norm, cos2, sin2, kmean, k_aug, v_t)
    out = _out_proj(x2, h2, 3, w_out.astype(BF16), [o.reshape(B * S, WIDTH)])
    return out.reshape(B, S, D)


def _split2_dot(a, u2):
    hi = a.astype(BF16)
    lo = (a - hi.astype(F32)).astype(BF16)
    return jnp.dot(jnp.concatenate([hi, lo], axis=1), u2, preferred_element_type=F32)


def _sb_attn_body(q_ref, k_ref, v_ref, o_ref, kt_ref, *, t, nh):
    qi = pl.program_id(2)

    @pl.when(qi == 0)
    def _():
        for r in range(nh):
            _store_tiles_t(kt_ref, (r,), k_ref[0, :, r * HEAD_DIM:(r + 1) * HEAD_DIM].astype(F32), t)

    after = jnp.where(_iota((t, t), 0) > _iota((t, t), 1), 1.0, 0.0).astype(BF16)
    after2 = jnp.concatenate([after, after], axis=0)
    qs = [(q_ref[0, :, r * HEAD_DIM:(r + 1) * HEAD_DIM].astype(F32) * ATTN_SCALE).astype(BF16)
          for r in range(nh)]

    def block(j0, n, states, diag):
        zs = [jnp.dot(qs[r], jnp.concatenate([kt_ref[r, j0 + i] for i in range(n)], axis=1),
                      preferred_element_type=F32) for r in range(nh)]
        if diag:
            past = _iota((t, n * t), 1) < _iota((t, n * t), 0) + (n - 1) * t
        staged = []
        for r in range(nh):
            z = zs[r]
            log_keep = -(jnp.maximum(z, 0.0) + jnp.log(1.0 + jnp.exp(-jnp.abs(z))))
            if diag:
                log_keep = jnp.where(past, log_keep, 0.0)
            tiles = []
            for i in range(n):
                lk = log_keep[:, i * t:(i + 1) * t]
                suffix = _split2_dot(lk, after2)
                tiles.append((suffix, suffix[:, 0:1] + lk[:, 0:1]))
            staged.append((z + log_keep, tiles))
        out = []
        for r in range(nh):
            carry, acc = states[r]
            log_beta, tiles = staged[r]
            vs = v_ref[0, pl.ds(pl.multiple_of(j0 * t, t), n * t), r * HEAD_DIM:(r + 1) * HEAD_DIM]
            behind = [None] * n
            for i in reversed(range(n)):
                behind[i] = tiles[i][0] + carry
                carry = carry + tiles[i][1]
            a = jnp.exp(log_beta + jnp.concatenate(behind, axis=1))
            if diag:
                a = jnp.where(past, a, 0.0)
            out.append((carry, acc + jnp.dot(a.astype(BF16), vs, preferred_element_type=F32)))
        return tuple(out)

    init = tuple((jnp.zeros((t, 1), F32), jnp.zeros((t, HEAD_DIM), F32)) for _ in range(nh))
    states = lax.cond((qi & 1) == 1, lambda: block(qi - 1, 2, init, True), lambda: block(qi, 1, init, True))
    n_pairs = qi >> 1
    states = lax.fori_loop(0, n_pairs, lambda i, st: block(2 * (n_pairs - 1 - i), 2, st, False), states)
    for r in range(nh):
        o_ref[0, :, r * HEAD_DIM:(r + 1) * HEAD_DIM] = states[r][1].astype(o_ref.dtype)


def _sb_attn(h):
    B, S, _ = h.shape
    t = ATT_TILE
    nh = HEADS_PER_STEP
    nhb = N_HEADS // nh
    w = nh * HEAD_DIM
    return pl.pallas_call(
        functools.partial(_sb_attn_body, t=t, nh=nh),
        grid=(B, nhb, S // t),
        in_specs=[pl.BlockSpec((1, t, w), lambda b, hh, i: (b, i, hh)),
                  pl.BlockSpec((1, S, w), lambda b, hh, i: (b, 0, nhb + hh)),
                  pl.BlockSpec((1, S, w), lambda b, hh, i: (b, 0, 2 * nhb + hh))],
        out_specs=pl.BlockSpec((1, t, w), lambda b, hh, i: (b, i, hh)),
        out_shape=jax.ShapeDtypeStruct((B, S, WIDTH), BF16),
        scratch_shapes=[pltpu.VMEM((nh, S // t, HEAD_DIM, t), BF16)],
        compiler_params=_params("parallel", "parallel", "arbitrary"),
        name="sb_attn",
    )(h, h, h)


def _sb_layer(x, norm, w_in, w_out):
    B, S, D = x.shape
    x2 = x.reshape(B * S, D)
    h2 = _in_proj(x2, norm, w_in.astype(BF16), BF16, 1024, 1024)
    o = _sb_attn(h2.reshape(B, S, -1))
    out = _out_proj(x2, h2, 3, w_out.astype(BF16), [o.reshape(B * S, WIDTH)])
    return out.reshape(B, S, D)


NSA_KV_BLK0 = 2 * N_HEADS


def _nsa_compress_body(kc_ref, vc_ref, wk_ref, wv_ref, pos_ref, kcn_ref, cosc_ref, sinc_ref,
                       kcc_ref, vcct_ref, *, n_rows):
    half = CMP_BLOCK // CMP_STRIDE
    assert half == 2

    def compress(src_ref, w_ref):
        first = jnp.zeros((n_rows, HEAD_DIM), F32)
        second = jnp.zeros((n_rows, HEAD_DIM), F32)
        for l in range(CMP_STRIDE):
            rows = src_ref[0, pl.ds(l, n_rows, stride=CMP_STRIDE), :]
            first = first + jnp.dot(rows + pos_ref[l:l + 1, :], w_ref[l],
                                    precision=HIGHEST, preferred_element_type=F32)
            second = second + jnp.dot(rows + pos_ref[CMP_STRIDE + l:CMP_STRIDE + l + 1, :],
                                      w_ref[CMP_STRIDE + l], precision=HIGHEST, preferred_element_type=F32)
        return first + pltpu.roll(second, n_rows - 1, 0)

    kc = compress(kc_ref, wk_ref)
    kcc_ref[0, 0] = _rms_rope(kc, kcn_ref[...], cosc_ref[...], sinc_ref[...])
    vcct_ref[0, 0] = compress(vc_ref, wv_ref).T.astype(vcct_ref.dtype)


def _nsa_compress(h, cmp_wk, cmp_wv, cmp_pos, kc_norm, cos_c, sin_c):
    B, S, _ = h.shape
    n_rows = S // CMP_STRIDE
    wspec = pl.BlockSpec((CMP_BLOCK, HEAD_DIM, HEAD_DIM), lambda b, g: (0, 0, 0))
    tab = pl.BlockSpec((n_rows, HEAD_DIM), lambda b, g: (0, 0))
    out = pl.BlockSpec((1, 1, n_rows, HEAD_DIM), lambda b, g: (b, g, 0, 0))
    return pl.pallas_call(
        functools.partial(_nsa_compress_body, n_rows=n_rows),
        grid=(B, NSA_KV_HEADS),
        in_specs=[pl.BlockSpec((1, S, HEAD_DIM), lambda b, g: (b, 0, NSA_KV_BLK0 + g)),
                  pl.BlockSpec((1, S, HEAD_DIM), lambda b, g: (b, 0, NSA_KV_BLK0 + NSA_KV_HEADS + g)),
                  wspec, wspec,
                  pl.BlockSpec((CMP_BLOCK, HEAD_DIM), lambda b, g: (0, 0)),
                  pl.BlockSpec((1, HEAD_DIM), lambda b, g: (0, 0)), tab, tab],
        out_specs=[out, pl.BlockSpec((1, 1, HEAD_DIM, n_rows), lambda b, g: (b, g, 0, 0))],
        out_shape=[jax.ShapeDtypeStruct((B, NSA_KV_HEADS, n_rows, HEAD_DIM), F32),
                   jax.ShapeDtypeStruct((B, NSA_KV_HEADS, HEAD_DIM, n_rows), BF16)],
        compiler_params=_params("parallel", "parallel"),
        name="nsa_compress",
    )(h, h, cmp_wk, cmp_wv, cmp_pos, kc_norm.reshape(1, HEAD_DIM), cos_c, sin_c)


def _nsa_prep_kv_body(ks_ref, vs_ref, kw_ref, vw_ref, ksn_ref, kwn_ref, cos_ref, sin_ref,
                      ksaug_ref, vst_ref, kwb_ref, vwt_ref, *, t):
    cos2, sin2 = cos_ref[...], sin_ref[...]
    S = cos2.shape[0]
    ksaug_ref[0, :, :HEAD_DIM] = _rms_rope(ks_ref[0], ksn_ref[...], cos2, sin2).astype(BF16)
    ksaug_ref[0, :, HEAD_DIM:] = _onehot_block(S, _log2(SLC_BLOCK))
    kwb_ref[0] = _rms_rope(kw_ref[0], kwn_ref[...], cos2, sin2).astype(BF16)
    _store_tiles_t(vst_ref, (0, 0), vs_ref[0], t)
    _store_tiles_t(vwt_ref, (0, 0), vw_ref[0], t)


def _nsa_prep_kv(h, ks_norm, kw_norm, cos2, sin2):
    B, S, _ = h.shape
    G = NSA_KV_HEADS
    t = ATT_TILE

    def src(i):
        return pl.BlockSpec((1, S, HEAD_DIM), lambda b, g: (b, 0, NSA_KV_BLK0 + i * G + g))

    vec = pl.BlockSpec((1, HEAD_DIM), lambda b, g: (0, 0))
    full = pl.BlockSpec((S, HEAD_DIM), lambda b, g: (0, 0))
    vt = pl.BlockSpec((1, 1, S // t, HEAD_DIM, t), lambda b, g: (b, g, 0, 0, 0))
    vt_shape = jax.ShapeDtypeStruct((B, G, S // t, HEAD_DIM, t), BF16)
    return pl.pallas_call(
        functools.partial(_nsa_prep_kv_body, t=t),
        grid=(B, G),
        in_specs=[src(2), src(3), src(4), src(5), vec, vec, full, full],
        out_specs=[pl.BlockSpec((1, S, AUG), lambda b, g: (b, 0, g)), vt,
                   pl.BlockSpec((1, S, HEAD_DIM), lambda b, g: (b, 0, g)), vt],
        out_shape=[jax.ShapeDtypeStruct((B, S, G * AUG), BF16), vt_shape,
                   jax.ShapeDtypeStruct((B, S, KV_WIDTH), BF16), vt_shape],
        compiler_params=_params("parallel", "parallel"),
        name="nsa_prep_kv",
    )(h, h, h, h, ks_norm.reshape(1, HEAD_DIM), kw_norm.reshape(1, HEAD_DIM), cos2, sin2)


def _nsa_cmp_sel_body(q_ref, qn_ref, cos_ref, sin_ref, kcc_ref, vcct_ref, bg_ref, ocmp_ref, qt_ref,
                      *, t, n_cmp, n_slc, n_sel):
    g_idx = pl.program_id(1)
    q0 = pl.program_id(2) * t
    cos2, sin2 = cos_ref[...], sin_ref[...]
    kcc = kcc_ref[0, 0]
    kcc_hi = kcc.astype(BF16)
    kcc_lo = (kcc - kcc_hi.astype(F32)).astype(BF16)
    kcc3 = jnp.concatenate([kcc_hi, kcc_hi, kcc_lo], axis=1)
    vcct = vcct_ref[0, 0]
    cmp_ok = _iota((n_cmp, t), 0) * CMP_STRIDE + (CMP_BLOCK - 1) <= q0 + _iota((n_cmp, t), 1)

    p_sum = jnp.zeros((n_cmp, t), F32)
    for r in range(NSA_HPG):
        qnt = _rms_rope(q_ref[0, :, r * HEAD_DIM:(r + 1) * HEAD_DIM], qn_ref[...], cos2, sin2).T
        qt_ref[0, r, 0, :HEAD_DIM, :] = (qnt * (ATTN_SCALE * LOG2E)).astype(BF16)
        q_hi = qnt.astype(BF16)
        q_lo = (qnt - q_hi.astype(F32)).astype(BF16)
        logits = jnp.dot(kcc3, jnp.concatenate([q_hi, q_lo, q_hi], axis=0),
                         preferred_element_type=F32) * ATTN_SCALE
        logits = jnp.where(cmp_ok, logits, NEG)
        mx = jnp.max(logits, axis=0, keepdims=True)
        e = jnp.where(cmp_ok, jnp.exp(logits - mx), 0.0)
        p = e * (1.0 / jnp.maximum(jnp.sum(e, axis=0, keepdims=True), TINY))
        p_sum = p_sum + p
        o = jnp.dot(vcct, p.astype(BF16), preferred_element_type=F32).T
        o = o * _branch_gate(bg_ref, g_idx * NSA_HPG + r)
        ocmp_ref[0, :, r * HEAD_DIM:(r + 1) * HEAD_DIM] = o.astype(ocmp_ref.dtype)

    c_start = _iota((n_slc, n_cmp), 1) * CMP_STRIDE
    s_start = _iota((n_slc, n_cmp), 0) * SLC_BLOCK
    overlap = jnp.where((c_start < s_start + SLC_BLOCK) & (c_start + CMP_BLOCK > s_start), 1.0, 0.0).astype(BF16)
    ps_hi = p_sum.astype(BF16)
    rest = p_sum - ps_hi.astype(F32)
    ps_mid = rest.astype(BF16)
    ps_lo = (rest - ps_mid.astype(F32)).astype(BF16)
    imp = jnp.dot(jnp.concatenate([overlap, overlap, overlap], axis=1),
                  jnp.concatenate([ps_hi, ps_mid, ps_lo], axis=0), preferred_element_type=F32)
    blk = _iota((n_slc, t), 0)
    cur = (q0 + _iota((n_slc, t), 1)) >> _log2(SLC_BLOCK)
    imp = jnp.where(blk == 0, FORCE, imp)
    imp = jnp.where(blk == cur, FORCE, imp)
    imp = jnp.where(blk == cur - 1, FORCE, imp)
    imp = jnp.where(blk <= cur, imp, NEG)
    sub = 8
    ranks = []
    for g0 in range(0, n_slc, sub):
        mine = imp[g0:g0 + sub, :]
        row = _iota((sub, t), 0) + g0
        rank = jnp.zeros((sub, t), F32)
        for k in range(n_slc):
            c = imp[k:k + 1, :]
            ge = jnp.where(c >= mine, 1.0, 0.0)
            gt = jnp.where(c > mine, 1.0, 0.0)
            rank = rank + (ge if k < g0 else gt if k >= g0 + sub else jnp.where(row > k, ge, gt))
        ranks.append(rank)
    bias = jnp.where(jnp.concatenate(ranks, axis=0) < float(n_sel), 0.0, NEG)
    bias = jnp.concatenate([bias, jnp.zeros((HEAD_DIM - n_slc, t), F32)], axis=0).astype(BF16)
    for r in range(NSA_HPG):
        qt_ref[0, r, 0, HEAD_DIM:, :] = bias


def _nsa_cmp_sel(h, bg, q_norm, cos2, sin2, kc_c, vc_ct):
    B, S, _ = h.shape
    t = ATT_TILE
    G, R = NSA_KV_HEADS, NSA_HPG
    n_cmp = S // CMP_STRIDE
    n_slc = S // SLC_BLOCK
    assert n_slc <= HEAD_DIM and n_slc % 8 == 0
    tab = pl.BlockSpec((t, HEAD_DIM), lambda b, g, i: (i, 0))
    return pl.pallas_call(
        functools.partial(_nsa_cmp_sel_body, t=t, n_cmp=n_cmp, n_slc=n_slc, n_sel=min(SLC_TOPK, n_slc)),
        grid=(B, G, S // t),
        in_specs=[pl.BlockSpec((1, t, R * HEAD_DIM), lambda b, g, i: (b, i, g)),
                  pl.BlockSpec((1, HEAD_DIM), lambda b, g, i: (0, 0)), tab, tab,
                  pl.BlockSpec((1, 1, n_cmp, HEAD_DIM), lambda b, g, i: (b, g, 0, 0)),
                  pl.BlockSpec((1, 1, HEAD_DIM, n_cmp), lambda b, g, i: (b, g, 0, 0)),
                  pl.BlockSpec((1, t, LANES), lambda b, g, i: (b, i, 0))],
        out_specs=[pl.BlockSpec((1, t, R * HEAD_DIM), lambda b, g, i: (b, i, g)),
                   pl.BlockSpec((1, R, 1, AUG, t), lambda b, g, i: (b, g, i, 0, 0))],
        out_shape=[jax.ShapeDtypeStruct((B, S, WIDTH), BF16),
                   jax.ShapeDtypeStruct((B, N_HEADS, S // t, AUG, t), BF16)],
        compiler_params=_params("parallel", "parallel", "arbitrary"),
        name="nsa_cmp_sel",
    )(h, q_norm.reshape(1, HEAD_DIM), cos2, sin2, kc_c, vc_ct, bg)


def _win_attn_body(qt_ref, k_ref, vt_ref, bg_ref, o_ref, *, t, nh, n_back, gate_base):
    qi = pl.program_id(2)
    n_blk = n_back + 1
    j0 = jnp.maximum(qi - n_back, 0)
    k = k_ref[0, pl.ds(pl.multiple_of(j0 * t, t), n_blk * t), :]
    vt = jnp.concatenate([vt_ref[0, 0, j0 + d] for d in range(n_blk)], axis=1)
    dist = _iota((n_blk * t, t), 1) - _iota((n_blk * t, t), 0) + (qi - j0) * t
    ss = [jnp.dot(k, qt_ref[0, r, 0], preferred_element_type=F32) for r in range(nh)]
    states = []
    for r in range(nh):
        s = jnp.where(dist >= 0, ss[r], NEG)
        s = jnp.where(dist < WINDOW, s, NEG)
        m = jnp.max(s, axis=0, keepdims=True)
        p = jnp.exp2(s - m)
        states.append((m, jnp.sum(p, axis=0, keepdims=True), jnp.dot(vt, p.astype(BF16), preferred_element_type=F32)))
    _store_heads(o_ref, states, bg_ref, gate_base + pl.program_id(1) * nh)


def _win_attn(q_t, kw, vw_t, bg, gate_base):
    B, H, n_t, _, t = q_t.shape
    S = n_t * t
    nh = NSA_HPG
    assert WINDOW % t == 0 and n_t > WINDOW // t
    return pl.pallas_call(
        functools.partial(_win_attn_body, t=t, nh=nh, n_back=WINDOW // t, gate_base=gate_base),
        grid=(B, H // nh, n_t),
        in_specs=[pl.BlockSpec((1, nh, 1, HEAD_DIM, t), lambda b, g, i: (b, g, i, 0, 0)),
                  pl.BlockSpec((1, S, HEAD_DIM), lambda b, g, i: (b, 0, g)),
                  pl.BlockSpec((1, 1, n_t, HEAD_DIM, t), lambda b, g, i: (b, g, 0, 0, 0)),
                  pl.BlockSpec((1, t, LANES), lambda b, g, i: (b, i, 0))],
        out_specs=pl.BlockSpec((1, t, nh * HEAD_DIM), lambda b, g, i: (b, i, g)),
        out_shape=jax.ShapeDtypeStruct((B, S, WIDTH), BF16),
        compiler_params=_params("parallel", "parallel", "arbitrary"),
        name="win_attn",
    )(q_t, kw, vw_t, bg)


def _nsa_layer(x, norm, w_in, q_norm, kc_norm, ks_norm, kw_norm, cmp_wk, cmp_wv, cmp_pos, w_out, cos2, sin2):
    B, S, D = x.shape
    x2 = x.reshape(B * S, D)
    kv_end = WIDTH + 6 * KV_WIDTH
    gate_end = kv_end + WIDTH
    w_main = jnp.concatenate([w_in[:, :WIDTH], w_in[:, kv_end:gate_end], w_in[:, WIDTH:kv_end]], axis=1)
    w_bg = jnp.pad(w_in[:, gate_end:], ((0, 0), (0, LANES - 3 * N_HEADS)))
    h2 = _in_proj(x2, norm, w_main.astype(BF16), F32, 1024, 1024)
    bg = _in_proj(x2, norm, w_bg.astype(BF16), F32, 512, LANES).reshape(B, S, LANES)
    h = h2.reshape(B, S, -1)
    cos_c, sin_c = _rope_tables(jnp.arange(S // CMP_STRIDE) * CMP_STRIDE + (CMP_BLOCK - 1))
    kc_c, vc_ct = _nsa_compress(h, cmp_wk, cmp_wv, cmp_pos, kc_norm, cos_c, sin_c)
    ks_aug, vs_t, kw_b, vw_t = _nsa_prep_kv(h, ks_norm, kw_norm, cos2, sin2)
    o_cmp, q_t = _nsa_cmp_sel(h, bg, q_norm, cos2, sin2, kc_c, vc_ct)
    o_slc = _sel_attn(q_t, ks_aug, vs_t, bg, N_HEADS)
    o_win = _win_attn(q_t, kw_b, vw_t, bg, 2 * N_HEADS)
    os = [o.reshape(B * S, WIDTH) for o in (o_cmp, o_slc, o_win)]
    out = _out_proj(x2, h2, 1, w_out.astype(BF16), os)
    return out.reshape(B, S, D)


def kernel(x, l0_norm, l0_w_in, l0_q_norm, l0_k_norm, l0_w_out, l1_norm, l1_w_in, l1_w_out, l2_norm, l2_w_in, l2_q_norm, l2_kc_norm, l2_ks_norm, l2_kw_norm, l2_cmp_wk, l2_cmp_wv, l2_cmp_pos, l2_w_out, l3_norm, l3_w_in, l3_q_norm, l3_k_norm, l3_w_out):
    cos2, sin2 = _rope_tables(jnp.arange(x.shape[1]))
    x = _moba_layer(x, l0_norm, l0_w_in, l0_q_norm, l0_k_norm, l0_w_out, cos2, sin2)
    x = _sb_layer(x, l1_norm, l1_w_in, l1_w_out)
    x = _nsa_layer(x, l2_norm, l2_w_in, l2_q_norm, l2_kc_norm, l2_ks_norm, l2_kw_norm,
                   l2_cmp_wk, l2_cmp_wv, l2_cmp_pos, l2_w_out, cos2, sin2)
    x = _moba_layer(x, l3_norm, l3_w_in, l3_q_norm, l3_k_norm, l3_w_out, cos2, sin2)
    return x
```

```python
import functools
import math

import jax
import jax.numpy as jnp
from jax import lax
from jax.experimental import pallas as pl
from jax.experimental.pallas import tpu as pltpu

F32 = jnp.float32
BF16 = jnp.bfloat16
HIGHEST = lax.Precision.HIGHEST

D_MODEL = 2048
HEAD_DIM = 128
N_HEADS = 16
WIDTH = N_HEADS * HEAD_DIM
ROPE_THETA = 10000.0
EPS = 1e-6
ATTN_SCALE = HEAD_DIM ** -0.5
LOG2E = math.log2(math.e)
NEG = -1e30
LOWEST = -3e38
FORCE = 1e30
TINY = 1e-30
MOBA_BLOCK = 256
MOBA_TOPK = 3
NSA_KV_HEADS = 4
NSA_HPG = N_HEADS // NSA_KV_HEADS
KV_WIDTH = NSA_KV_HEADS * HEAD_DIM
CMP_BLOCK = 32
CMP_STRIDE = 16
SLC_BLOCK = 64
SLC_TOPK = 16
WINDOW = 512

LANES = 128
AUG = 2 * HEAD_DIM
ATT_TILE = 256
HEADS_PER_STEP = 4
VMEM_LIMIT = 48 * 1024 * 1024

_NT = (((1,), (1,)), ((), ()))


def _params(*sem):
    return pltpu.CompilerParams(dimension_semantics=sem, vmem_limit_bytes=VMEM_LIMIT)


def _rope_tables(pos):
    half = HEAD_DIM // 2
    inv_freq = jnp.exp(-math.log(ROPE_THETA) * jnp.arange(half, dtype=F32) / half)
    ang = pos.astype(F32)[:, None] * inv_freq[None, :]
    cos, sin = jnp.cos(ang), jnp.sin(ang)
    return jnp.concatenate([cos, cos], axis=-1), jnp.concatenate([-sin, sin], axis=-1)


def _rms_rope(x, g, cos2, sin2):
    y = x * lax.rsqrt(jnp.mean(x * x, axis=-1, keepdims=True) + EPS) * g
    return y * cos2 + pltpu.roll(y, HEAD_DIM // 2, 1) * sin2


def _iota(shape, dim):
    return lax.broadcasted_iota(jnp.int32, shape, dim)


def _log2(n):
    s = n.bit_length() - 1
    assert 1 << s == n
    return s


def _store_tiles_t(dst_ref, lead, x, t):
    for j in range(x.shape[0] // t):
        dst_ref[lead + (j,)] = x[j * t:(j + 1) * t, :].T.astype(dst_ref.dtype)


def _in_proj_body(x_ref, g_ref, w_ref, o_ref, xn_ref):
    @pl.when(pl.program_id(1) == 0)
    def _():
        x = x_ref[...]
        r = lax.rsqrt(jnp.mean(x * x, axis=-1, keepdims=True) + EPS)
        xn_ref[...] = (x * r * g_ref[...]).astype(xn_ref.dtype)

    o_ref[...] = jnp.dot(xn_ref[...], w_ref[...], preferred_element_type=F32).astype(o_ref.dtype)


def _in_proj(x2, g, w, out_dtype, tm, tn):
    M, K = x2.shape
    N = w.shape[1]
    return pl.pallas_call(
        _in_proj_body,
        grid=(M // tm, N // tn),
        in_specs=[pl.BlockSpec((tm, K), lambda i, j: (i, 0)),
                  pl.BlockSpec((1, K), lambda i, j: (0, 0)),
                  pl.BlockSpec((K, tn), lambda i, j: (0, j))],
        out_specs=pl.BlockSpec((tm, tn), lambda i, j: (i, j)),
        out_shape=jax.ShapeDtypeStruct((M, N), out_dtype),
        scratch_shapes=[pltpu.VMEM((tm, K), BF16)],
        compiler_params=_params("parallel", "arbitrary"),
        name="in_proj",
    )(x2, g.reshape(1, K), w)


def _out_proj_body(*refs, n_o):
    x_ref, gate_ref, w_ref = refs[0], refs[1], refs[2]
    o_refs = refs[3:3 + n_o]
    out_ref = refs[3 + n_o]
    o = o_refs[0][...].astype(F32)
    for r in o_refs[1:]:
        o = o + r[...].astype(F32)
    gate = gate_ref[...].astype(F32)
    y = o * (gate / (1.0 + jnp.exp(-gate)))
    out_ref[...] = x_ref[...] + jnp.dot(y.astype(BF16), w_ref[...], preferred_element_type=F32)


def _out_proj(x2, h2, gate_blk, w, os, tm=256):
    M, D = x2.shape
    n_o = len(os)
    row = pl.BlockSpec((tm, D), lambda i: (i, 0))
    return pl.pallas_call(
        functools.partial(_out_proj_body, n_o=n_o),
        grid=(M // tm,),
        in_specs=[row, pl.BlockSpec((tm, D), lambda i: (i, gate_blk)),
                  pl.BlockSpec((D, D), lambda i: (0, 0))] + [row] * n_o,
        out_specs=row,
        out_shape=jax.ShapeDtypeStruct((M, D), F32),
        compiler_params=_params("parallel"),
        name="out_proj",
    )(x2, h2, w, *os)


def _branch_gate(bg_ref, col):
    bg = bg_ref[0]
    g = 1.0 / (1.0 + jnp.exp(-bg))
    return jnp.sum(jnp.where(_iota(bg.shape, 1) == col, g, 0.0), axis=1, keepdims=True)


def _softmax_block(ml_ref, acc_ref, r, s, vt, first):
    if first:
        m_new = jnp.max(s, axis=0, keepdims=True)
        p = jnp.exp2(s - m_new)
        l = jnp.sum(p, axis=0, keepdims=True)
        acc_ref[r] = jnp.dot(vt, p.astype(BF16), preferred_element_type=F32)
    else:
        m = ml_ref[2 * r:2 * r + 1, :]
        m_new = jnp.maximum(m, jnp.max(s, axis=0, keepdims=True))
        alpha = jnp.exp2(m - m_new)
        p = jnp.exp2(s - m_new)
        l = alpha * ml_ref[2 * r + 1:2 * r + 2, :] + jnp.sum(p, axis=0, keepdims=True)
        acc_ref[r] = alpha * acc_ref[r] + jnp.dot(vt, p.astype(BF16), preferred_element_type=F32)
    ml_ref[2 * r:2 * r + 1, :] = m_new
    ml_ref[2 * r + 1:2 * r + 2, :] = l


def _store_heads(o_ref, sums_accs, bg_ref, gate_col0):
    for r, (l, acc) in enumerate(sums_accs):
        o = (acc * (1.0 / l)).T
        if bg_ref is not None:
            o = o * _branch_gate(bg_ref, gate_col0 + r)
        o_ref[0, :, r * HEAD_DIM:(r + 1) * HEAD_DIM] = o.astype(o_ref.dtype)


def _attend_causal(qts, k_ref, vt_ref, ml_ref, acc_ref, qi, t, shared):
    nh = len(qts)

    def block(j0, n, diag):
        off = pl.multiple_of(j0 * t, t)
        ss = [jnp.dot(k_ref[0, pl.ds(off, n * t), (0 if shared else r) * AUG:(1 if shared else r + 1) * AUG],
                      qts[r], preferred_element_type=F32) for r in range(nh)]
        if diag:
            causal = _iota((n * t, t), 0) <= _iota((n * t, t), 1) + (n - 1) * t
        for r in range(nh):
            s = jnp.where(causal, ss[r], NEG) if diag else ss[r]
            c = 0 if shared else r
            vt = jnp.concatenate([vt_ref[0, c, j0 + i] for i in range(n)], axis=1)
            _softmax_block(ml_ref, acc_ref, r, s, vt, diag)

    @pl.when((qi & 1) == 1)
    def _():
        block(qi - 1, 2, True)

    @pl.when((qi & 1) == 0)
    def _():
        block(qi, 1, True)

    def body(j2, carry):
        block(2 * j2, 2, False)
        return carry

    lax.fori_loop(0, qi >> 1, body, 0)
    return [(ml_ref[2 * r + 1:2 * r + 2, :], acc_ref[r]) for r in range(nh)]


def _attn_scratch(nh, t):
    return [pltpu.VMEM((2 * nh, t), F32), pltpu.VMEM((nh, HEAD_DIM, t), F32)]


def _sel_attn_body(qt_ref, k_ref, vt_ref, bg_ref, o_ref, ml_ref, acc_ref, *, t, nh, gate_base):
    qts = [qt_ref[0, r, 0] for r in range(nh)]
    sums_accs = _attend_causal(qts, k_ref, vt_ref, ml_ref, acc_ref, pl.program_id(2), t, True)
    _store_heads(o_ref, sums_accs, bg_ref, gate_base + pl.program_id(1) * nh)


def _sel_attn(q_t, k_aug, v_t, bg, gate_base):
    B, H, n_t, _, t = q_t.shape
    S = n_t * t
    nh = NSA_HPG
    assert n_t >= 2
    return pl.pallas_call(
        functools.partial(_sel_attn_body, t=t, nh=nh, gate_base=gate_base),
        grid=(B, H // nh, n_t),
        in_specs=[pl.BlockSpec((1, nh, 1, AUG, t), lambda b, h, i: (b, h, i, 0, 0)),
                  pl.BlockSpec((1, S, AUG), lambda b, h, i: (b, 0, h)),
                  pl.BlockSpec((1, 1, n_t, HEAD_DIM, t), lambda b, h, i: (b, h, 0, 0, 0)),
                  pl.BlockSpec((1, t, LANES), lambda b, h, i: (b, i, 0))],
        out_specs=pl.BlockSpec((1, t, nh * HEAD_DIM), lambda b, h, i: (b, i, h)),
        out_shape=jax.ShapeDtypeStruct((B, S, WIDTH), BF16),
        scratch_shapes=_attn_scratch(nh, t),
        compiler_params=_params("parallel", "parallel", "arbitrary"),
        name="sel_attn",
    )(q_t, k_aug, v_t, bg)


def _onehot_block(S, shift):
    return jnp.where((_iota((S, LANES), 0) >> shift) == _iota((S, LANES), 1), 1.0, 0.0).astype(BF16)


def _moba_prep_k_body(k_ref, v_ref, kn_ref, cos_ref, sin_ref, kaug_ref, vt_ref, kmean_ref, *, nb, t):
    kn = _rms_rope(k_ref[0], kn_ref[...], cos_ref[...], sin_ref[...])
    S = kn.shape[0]
    kaug_ref[0, :, :HEAD_DIM] = kn.astype(BF16)
    kaug_ref[0, :, HEAD_DIM:] = _onehot_block(S, _log2(MOBA_BLOCK))
    _store_tiles_t(vt_ref, (0, 0), v_ref[0], t)
    kmean_ref[0, 0] = jnp.mean(kn.reshape(nb, MOBA_BLOCK, HEAD_DIM), axis=1)


def _moba_prep_k(h, k_norm, cos2, sin2):
    B, S, _ = h.shape
    nb = S // MOBA_BLOCK
    t = ATT_TILE
    assert 1 < nb <= HEAD_DIM
    full = pl.BlockSpec((S, HEAD_DIM), lambda b, hh: (0, 0))
    return pl.pallas_call(
        functools.partial(_moba_prep_k_body, nb=nb, t=t),
        grid=(B, N_HEADS),
        in_specs=[pl.BlockSpec((1, S, HEAD_DIM), lambda b, hh: (b, 0, N_HEADS + hh)),
                  pl.BlockSpec((1, S, HEAD_DIM), lambda b, hh: (b, 0, 2 * N_HEADS + hh)),
                  pl.BlockSpec((1, HEAD_DIM), lambda b, hh: (0, 0)), full, full],
        out_specs=[pl.BlockSpec((1, S, AUG), lambda b, hh: (b, 0, hh)),
                   pl.BlockSpec((1, 1, S // t, HEAD_DIM, t), lambda b, hh: (b, hh, 0, 0, 0)),
                   pl.BlockSpec((1, 1, nb, HEAD_DIM), lambda b, hh: (b, hh, 0, 0))],
        out_shape=[jax.ShapeDtypeStruct((B, S, N_HEADS * AUG), BF16),
                   jax.ShapeDtypeStruct((B, N_HEADS, S // t, HEAD_DIM, t), BF16),
                   jax.ShapeDtypeStruct((B, N_HEADS, nb, HEAD_DIM), F32)],
        compiler_params=_params("parallel", "parallel"),
        name="moba_prep_k",
    )(h, h, k_norm.reshape(1, HEAD_DIM), cos2, sin2)


def _moba_attn_body(q_ref, qn_ref, cos_ref, sin_ref, kmean_ref, k_ref, vt_ref, o_ref, ml_ref, acc_ref,
                    *, t, nh, nb, n_sel):
    cur = pl.program_id(2)
    cos2, sin2 = cos_ref[...], sin_ref[...]
    blk = _iota((nb, t), 0)
    blk_f = blk.astype(F32)
    qts = []
    for r in range(nh):
        qn = _rms_rope(q_ref[0, :, r * HEAD_DIM:(r + 1) * HEAD_DIM], qn_ref[...], cos2, sin2)
        qnt = qn.T
        gate = jnp.dot(kmean_ref[0, r], qnt, precision=HIGHEST, preferred_element_type=F32)
        g = jnp.where(blk < cur, gate, NEG)
        picked = jnp.zeros((nb, t), F32)
        for _ in range(n_sel):
            mx = jnp.max(g, axis=0, keepdims=True)
            first = jnp.min(jnp.where(g == mx, blk_f, float(nb)), axis=0, keepdims=True)
            hit = blk_f == first
            picked = jnp.where(hit, 1.0, picked)
            g = jnp.where(hit, LOWEST, g)
        attend = jnp.where(blk < cur, picked, jnp.where(blk == cur, 1.0, 0.0))
        bias = jnp.where(attend > 0.0, 0.0, NEG)
        qts.append(jnp.concatenate([qnt * (ATTN_SCALE * LOG2E), bias, jnp.zeros((HEAD_DIM - nb, t), F32)],
                                   axis=0).astype(BF16))
    _store_heads(o_ref, _attend_causal(qts, k_ref, vt_ref, ml_ref, acc_ref, cur, t, False), None, None)


def _moba_attn(h, q_norm, cos2, sin2, kmean, k_aug, v_t):
    B, S, _ = h.shape
    t = MOBA_BLOCK
    assert t == ATT_TILE
    nb = S // MOBA_BLOCK
    nh = HEADS_PER_STEP
    n_sel = min(MOBA_TOPK, nb - 1)
    tab = pl.BlockSpec((t, HEAD_DIM), lambda b, hh, i: (i, 0))
    return pl.pallas_call(
        functools.partial(_moba_attn_body, t=t, nh=nh, nb=nb, n_sel=n_sel),
        grid=(B, N_HEADS // nh, nb),
        in_specs=[pl.BlockSpec((1, t, nh * HEAD_DIM), lambda b, hh, i: (b, i, hh)),
                  pl.BlockSpec((1, HEAD_DIM), lambda b, hh, i: (0, 0)), tab, tab,
                  pl.BlockSpec((1, nh, nb, HEAD_DIM), lambda b, hh, i: (b, hh, 0, 0)),
                  pl.BlockSpec((1, S, nh * AUG), lambda b, hh, i: (b, 0, hh)),
                  pl.BlockSpec((1, nh, nb, HEAD_DIM, t), lambda b, hh, i: (b, hh, 0, 0, 0))],
        out_specs=pl.BlockSpec((1, t, nh * HEAD_DIM), lambda b, hh, i: (b, i, hh)),
        out_shape=jax.ShapeDtypeStruct((B, S, WIDTH), BF16),
        scratch_shapes=_attn_scratch(nh, t),
        compiler_params=_params("parallel", "parallel", "arbitrary"),
        name="moba_attn",
    )(h, q_norm.reshape(1, HEAD_DIM), cos2, sin2, kmean, k_aug, v_t)


def _moba_layer(x, norm, w_in, q_norm, k_norm, w_out, cos2, sin2):
    B, S, D = x.shape
    x2 = x.reshape(B * S, D)
    h2 = _in_proj(x2, norm, w_in.astype(BF16), F32, 1024, 1024)
    h = h2.reshape(B, S, -1)
    k_aug, v_t, kmean = _moba_prep_k(h, k_norm, cos2, sin2)
    o = _moba_attn(h, q_norm, cos2, sin2, kmean, k_aug, v_t)
    out = _out_proj(x2, h2, 3, w_out.astype(BF16), [o.reshape(B * S, WIDTH)])
    return out.reshape(B, S, D)


def _split2_dot(a, u):
    hi = a.astype(BF16)
    lo = (a - hi.astype(F32)).astype(BF16)
    return jnp.dot(hi, u, preferred_element_type=F32) + jnp.dot(lo, u, preferred_element_type=F32)


def _sb_attn_body(q_ref, k_ref, v_ref, o_ref, kt_ref, carry_ref, acc_ref, *, t, nh):
    qi = pl.program_id(2)

    @pl.when(qi == 0)
    def _():
        for r in range(nh):
            _store_tiles_t(kt_ref, (r,), k_ref[0, :, r * HEAD_DIM:(r + 1) * HEAD_DIM].astype(F32), t)

    after = jnp.where(_iota((t, t), 0) > _iota((t, t), 1), 1.0, 0.0).astype(BF16)
    qs = [(q_ref[0, :, r * HEAD_DIM:(r + 1) * HEAD_DIM].astype(F32) * ATTN_SCALE).astype(BF16)
          for r in range(nh)]

    def block(j0, n, diag):
        zs = [jnp.dot(qs[r], jnp.concatenate([kt_ref[r, j0 + i] for i in range(n)], axis=1),
                      preferred_element_type=F32) for r in range(nh)]
        if diag:
            past = _iota((t, n * t), 1) < _iota((t, n * t), 0) + (n - 1) * t
        staged = []
        for r in range(nh):
            z = zs[r]
            drop = jnp.maximum(z, 0.0) + jnp.log(1.0 + jnp.exp2(jnp.abs(z) * (-LOG2E)))
            if diag:
                drop = jnp.where(past, drop, 0.0)
            tiles = []
            for i in range(n):
                d = drop[:, i * t:(i + 1) * t]
                suffix = _split2_dot(d, after)
                tiles.append((suffix, suffix[:, 0:1] + d[:, 0:1]))
            staged.append((drop, tiles))
        for r in range(nh):
            drop, tiles = staged[r]
            vs = v_ref[0, pl.ds(pl.multiple_of(j0 * t, t), n * t), r * HEAD_DIM:(r + 1) * HEAD_DIM]
            carry = None if diag else carry_ref[r]
            behind = [None] * n
            for i in reversed(range(n)):
                behind[i] = tiles[i][0] if carry is None else tiles[i][0] + carry
                carry = tiles[i][1] if carry is None else carry + tiles[i][1]
            a = jnp.exp(zs[r] - (drop + jnp.concatenate(behind, axis=1)))
            if diag:
                a = jnp.where(past, a, 0.0)
            o = jnp.dot(a.astype(BF16), vs, preferred_element_type=F32)
            acc_ref[r] = o if diag else acc_ref[r] + o
            carry_ref[r] = carry

    @pl.when((qi & 1) == 1)
    def _():
        block(qi - 1, 2, True)

    @pl.when((qi & 1) == 0)
    def _():
        block(qi, 1, True)

    n_pairs = qi >> 1

    def body(i, carry):
        block(2 * (n_pairs - 1 - i), 2, False)
        return carry

    lax.fori_loop(0, n_pairs, body, 0)
    for r in range(nh):
        o_ref[0, :, r * HEAD_DIM:(r + 1) * HEAD_DIM] = acc_ref[r].astype(o_ref.dtype)


def _sb_attn(h):
    B, S, _ = h.shape
    t = ATT_TILE
    nh = HEADS_PER_STEP
    nhb = N_HEADS // nh
    w = nh * HEAD_DIM
    return pl.pallas_call(
        functools.partial(_sb_attn_body, t=t, nh=nh),
        grid=(B, nhb, S // t),
        in_specs=[pl.BlockSpec((1, t, w), lambda b, hh, i: (b, i, hh)),
                  pl.BlockSpec((1, S, w), lambda b, hh, i: (b, 0, nhb + hh)),
                  pl.BlockSpec((1, S, w), lambda b, hh, i: (b, 0, 2 * nhb + hh))],
        out_specs=pl.BlockSpec((1, t, w), lambda b, hh, i: (b, i, hh)),
        out_shape=jax.ShapeDtypeStruct((B, S, WIDTH), BF16),
        scratch_shapes=[pltpu.VMEM((nh, S // t, HEAD_DIM, t), BF16), pltpu.VMEM((nh, t, 1), F32),
                        pltpu.VMEM((nh, t, HEAD_DIM), F32)],
        compiler_params=_params("parallel", "parallel", "arbitrary"),
        name="sb_attn",
    )(h, h, h)


def _sb_layer(x, norm, w_in, w_out):
    B, S, D = x.shape
    x2 = x.reshape(B * S, D)
    h2 = _in_proj(x2, norm, w_in.astype(BF16), BF16, 1024, 1024)
    o = _sb_attn(h2.reshape(B, S, -1))
    out = _out_proj(x2, h2, 3, w_out.astype(BF16), [o.reshape(B * S, WIDTH)])
    return out.reshape(B, S, D)


NSA_KV_BLK0 = 2 * N_HEADS


def _nsa_compress_body(kc_ref, vc_ref, wk_ref, wv_ref, pos_ref, kcn_ref, cosc_ref, sinc_ref,
                       kcc_ref, vcct_ref, *, n_rows):
    half = CMP_BLOCK // CMP_STRIDE
    assert half == 2

    def compress(src_ref, w_ref):
        first = jnp.zeros((n_rows, HEAD_DIM), F32)
        second = jnp.zeros((n_rows, HEAD_DIM), F32)
        for l in range(CMP_STRIDE):
            rows = src_ref[0, pl.ds(l, n_rows, stride=CMP_STRIDE), :]
            first = first + jnp.dot(rows + pos_ref[l:l + 1, :], w_ref[l],
                                    precision=HIGHEST, preferred_element_type=F32)
            second = second + jnp.dot(rows + pos_ref[CMP_STRIDE + l:CMP_STRIDE + l + 1, :],
                                      w_ref[CMP_STRIDE + l], precision=HIGHEST, preferred_element_type=F32)
        return first + pltpu.roll(second, n_rows - 1, 0)

    kc = compress(kc_ref, wk_ref)
    kcc_ref[0, 0] = _rms_rope(kc, kcn_ref[...], cosc_ref[...], sinc_ref[...])
    vcct_ref[0, 0] = compress(vc_ref, wv_ref).T.astype(vcct_ref.dtype)


def _nsa_compress(h, cmp_wk, cmp_wv, cmp_pos, kc_norm, cos_c, sin_c):
    B, S, _ = h.shape
    n_rows = S // CMP_STRIDE
    wspec = pl.BlockSpec((CMP_BLOCK, HEAD_DIM, HEAD_DIM), lambda b, g: (0, 0, 0))
    tab = pl.BlockSpec((n_rows, HEAD_DIM), lambda b, g: (0, 0))
    out = pl.BlockSpec((1, 1, n_rows, HEAD_DIM), lambda b, g: (b, g, 0, 0))
    return pl.pallas_call(
        functools.partial(_nsa_compress_body, n_rows=n_rows),
        grid=(B, NSA_KV_HEADS),
        in_specs=[pl.BlockSpec((1, S, HEAD_DIM), lambda b, g: (b, 0, NSA_KV_BLK0 + g)),
                  pl.BlockSpec((1, S, HEAD_DIM), lambda b, g: (b, 0, NSA_KV_BLK0 + NSA_KV_HEADS + g)),
                  wspec, wspec,
                  pl.BlockSpec((CMP_BLOCK, HEAD_DIM), lambda b, g: (0, 0)),
                  pl.BlockSpec((1, HEAD_DIM), lambda b, g: (0, 0)), tab, tab],
        out_specs=[out, pl.BlockSpec((1, 1, HEAD_DIM, n_rows), lambda b, g: (b, g, 0, 0))],
        out_shape=[jax.ShapeDtypeStruct((B, NSA_KV_HEADS, n_rows, HEAD_DIM), F32),
                   jax.ShapeDtypeStruct((B, NSA_KV_HEADS, HEAD_DIM, n_rows), BF16)],
        compiler_params=_params("parallel", "parallel"),
        name="nsa_compress",
    )(h, h, cmp_wk, cmp_wv, cmp_pos, kc_norm.reshape(1, HEAD_DIM), cos_c, sin_c)


def _nsa_prep_kv_body(ks_ref, vs_ref, kw_ref, vw_ref, ksn_ref, kwn_ref, cos_ref, sin_ref,
                      ksaug_ref, vst_ref, kwb_ref, vwt_ref, *, t):
    cos2, sin2 = cos_ref[...], sin_ref[...]
    S = cos2.shape[0]
    ksaug_ref[0, :, :HEAD_DIM] = _rms_rope(ks_ref[0], ksn_ref[...], cos2, sin2).astype(BF16)
    ksaug_ref[0, :, HEAD_DIM:] = _onehot_block(S, _log2(SLC_BLOCK))
    kwb_ref[0] = _rms_rope(kw_ref[0], kwn_ref[...], cos2, sin2).astype(BF16)
    _store_tiles_t(vst_ref, (0, 0), vs_ref[0], t)
    _store_tiles_t(vwt_ref, (0, 0), vw_ref[0], t)


def _nsa_prep_kv(h, ks_norm, kw_norm, cos2, sin2):
    B, S, _ = h.shape
    G = NSA_KV_HEADS
    t = ATT_TILE

    def src(i):
        return pl.BlockSpec((1, S, HEAD_DIM), lambda b, g: (b, 0, NSA_KV_BLK0 + i * G + g))

    vec = pl.BlockSpec((1, HEAD_DIM), lambda b, g: (0, 0))
    full = pl.BlockSpec((S, HEAD_DIM), lambda b, g: (0, 0))
    vt = pl.BlockSpec((1, 1, S // t, HEAD_DIM, t), lambda b, g: (b, g, 0, 0, 0))
    vt_shape = jax.ShapeDtypeStruct((B, G, S // t, HEAD_DIM, t), BF16)
    return pl.pallas_call(
        functools.partial(_nsa_prep_kv_body, t=t),
        grid=(B, G),
        in_specs=[src(2), src(3), src(4), src(5), vec, vec, full, full],
        out_specs=[pl.BlockSpec((1, S, AUG), lambda b, g: (b, 0, g)), vt,
                   pl.BlockSpec((1, S, HEAD_DIM), lambda b, g: (b, 0, g)), vt],
        out_shape=[jax.ShapeDtypeStruct((B, S, G * AUG), BF16), vt_shape,
                   jax.ShapeDtypeStruct((B, S, KV_WIDTH), BF16), vt_shape],
        compiler_params=_params("parallel", "parallel"),
        name="nsa_prep_kv",
    )(h, h, h, h, ks_norm.reshape(1, HEAD_DIM), kw_norm.reshape(1, HEAD_DIM), cos2, sin2)


def _nsa_cmp_sel_body(q_ref, qn_ref, cos_ref, sin_ref, kcc_ref, vcct_ref, bg_ref, ocmp_ref, qt_ref,
                      *, t, n_cmp, n_slc, n_sel):
    g_idx = pl.program_id(1)
    q0 = pl.program_id(2) * t
    cos2, sin2 = cos_ref[...], sin_ref[...]
    kcc = kcc_ref[0, 0]
    kcc_hi = kcc.astype(BF16)
    kcc_lo = (kcc - kcc_hi.astype(F32)).astype(BF16)
    kcc3 = jnp.concatenate([kcc_hi, kcc_hi, kcc_lo], axis=1)
    vcct = vcct_ref[0, 0]
    cmp_ok = _iota((n_cmp, t), 0) * CMP_STRIDE + (CMP_BLOCK - 1) <= q0 + _iota((n_cmp, t), 1)

    p_sum = jnp.zeros((n_cmp, t), F32)
    for r in range(NSA_HPG):
        qnt = _rms_rope(q_ref[0, :, r * HEAD_DIM:(r + 1) * HEAD_DIM], qn_ref[...], cos2, sin2).T
        qt_ref[0, r, 0, :HEAD_DIM, :] = (qnt * (ATTN_SCALE * LOG2E)).astype(BF16)
        q_hi = qnt.astype(BF16)
        q_lo = (qnt - q_hi.astype(F32)).astype(BF16)
        logits = jnp.dot(kcc3, jnp.concatenate([q_hi, q_lo, q_hi], axis=0),
                         preferred_element_type=F32) * ATTN_SCALE
        logits = jnp.where(cmp_ok, logits, NEG)
        mx = jnp.max(logits, axis=0, keepdims=True)
        e = jnp.where(cmp_ok, jnp.exp(logits - mx), 0.0)
        p = e * (1.0 / jnp.maximum(jnp.sum(e, axis=0, keepdims=True), TINY))
        p_sum = p_sum + p
        o = jnp.dot(vcct, p.astype(BF16), preferred_element_type=F32).T
        o = o * _branch_gate(bg_ref, g_idx * NSA_HPG + r)
        ocmp_ref[0, :, r * HEAD_DIM:(r + 1) * HEAD_DIM] = o.astype(ocmp_ref.dtype)

    c_start = _iota((n_slc, n_cmp), 1) * CMP_STRIDE
    s_start = _iota((n_slc, n_cmp), 0) * SLC_BLOCK
    overlap = jnp.where((c_start < s_start + SLC_BLOCK) & (c_start + CMP_BLOCK > s_start), 1.0, 0.0).astype(BF16)
    ps_hi = p_sum.astype(BF16)
    rest = p_sum - ps_hi.astype(F32)
    ps_mid = rest.astype(BF16)
    ps_lo = (rest - ps_mid.astype(F32)).astype(BF16)
    imp = jnp.dot(jnp.concatenate([overlap, overlap, overlap], axis=1),
                  jnp.concatenate([ps_hi, ps_mid, ps_lo], axis=0), preferred_element_type=F32)
    blk = _iota((n_slc, t), 0)
    cur = (q0 + _iota((n_slc, t), 1)) >> _log2(SLC_BLOCK)
    imp = jnp.where(blk == 0, FORCE, imp)
    imp = jnp.where(blk == cur, FORCE, imp)
    imp = jnp.where(blk == cur - 1, FORCE, imp)
    imp = jnp.where(blk <= cur, imp, NEG)
    sub = 8
    ranks = []
    for g0 in range(0, n_slc, sub):
        mine = imp[g0:g0 + sub, :]
        row = _iota((sub, t), 0) + g0
        rank = jnp.zeros((sub, t), F32)
        for k in range(n_slc):
            c = imp[k:k + 1, :]
            ge = jnp.where(c >= mine, 1.0, 0.0)
            gt = jnp.where(c > mine, 1.0, 0.0)
            rank = rank + (ge if k < g0 else gt if k >= g0 + sub else jnp.where(row > k, ge, gt))
        ranks.append(rank)
    bias = jnp.where(jnp.concatenate(ranks, axis=0) < float(n_sel), 0.0, NEG)
    bias = jnp.concatenate([bias, jnp.zeros((HEAD_DIM - n_slc, t), F32)], axis=0).astype(BF16)
    for r in range(NSA_HPG):
        qt_ref[0, r, 0, HEAD_DIM:, :] = bias


def _nsa_cmp_sel(h, bg, q_norm, cos2, sin2, kc_c, vc_ct):
    B, S, _ = h.shape
    t = ATT_TILE
    G, R = NSA_KV_HEADS, NSA_HPG
    n_cmp = S // CMP_STRIDE
    n_slc = S // SLC_BLOCK
    assert n_slc <= HEAD_DIM and n_slc % 8 == 0
    tab = pl.BlockSpec((t, HEAD_DIM), lambda b, g, i: (i, 0))
    return pl.pallas_call(
        functools.partial(_nsa_cmp_sel_body, t=t, n_cmp=n_cmp, n_slc=n_slc, n_sel=min(SLC_TOPK, n_slc)),
        grid=(B, G, S // t),
        in_specs=[pl.BlockSpec((1, t, R * HEAD_DIM), lambda b, g, i: (b, i, g)),
                  pl.BlockSpec((1, HEAD_DIM), lambda b, g, i: (0, 0)), tab, tab,
                  pl.BlockSpec((1, 1, n_cmp, HEAD_DIM), lambda b, g, i: (b, g, 0, 0)),
                  pl.BlockSpec((1, 1, HEAD_DIM, n_cmp), lambda b, g, i: (b, g, 0, 0)),
                  pl.BlockSpec((1, t, LANES), lambda b, g, i: (b, i, 0))],
        out_specs=[pl.BlockSpec((1, t, R * HEAD_DIM), lambda b, g, i: (b, i, g)),
                   pl.BlockSpec((1, R, 1, AUG, t), lambda b, g, i: (b, g, i, 0, 0))],
        out_shape=[jax.ShapeDtypeStruct((B, S, WIDTH), BF16),
                   jax.ShapeDtypeStruct((B, N_HEADS, S // t, AUG, t), BF16)],
        compiler_params=_params("parallel", "parallel", "arbitrary"),
        name="nsa_cmp_sel",
    )(h, q_norm.reshape(1, HEAD_DIM), cos2, sin2, kc_c, vc_ct, bg)


def _win_attn_body(qt_ref, k_ref, vt_ref, bg_ref, o_ref, *, t, nh, n_back, gate_base):
    qi = pl.program_id(2)
    n_blk = n_back + 1
    j0 = jnp.maximum(qi - n_back, 0)
    k = k_ref[0, pl.ds(pl.multiple_of(j0 * t, t), n_blk * t), :]
    vt = jnp.concatenate([vt_ref[0, 0, j0 + d] for d in range(n_blk)], axis=1)
    dist = _iota((n_blk * t, t), 1) - _iota((n_blk * t, t), 0) + (qi - j0) * t
    ss = [jnp.dot(k, qt_ref[0, r, 0], preferred_element_type=F32) for r in range(nh)]
    states = []
    for r in range(nh):
        s = jnp.where(dist >= 0, ss[r], NEG)
        s = jnp.where(dist < WINDOW, s, NEG)
        m = jnp.max(s, axis=0, keepdims=True)
        p = jnp.exp2(s - m)
        states.append((jnp.sum(p, axis=0, keepdims=True), jnp.dot(vt, p.astype(BF16), preferred_element_type=F32)))
    _store_heads(o_ref, states, bg_ref, gate_base + pl.program_id(1) * nh)


def _win_attn(q_t, kw, vw_t, bg, gate_base):
    B, H, n_t, _, t = q_t.shape
    S = n_t * t
    nh = NSA_HPG
    assert WINDOW % t == 0 and n_t > WINDOW // t
    return pl.pallas_call(
        functools.partial(_win_attn_body, t=t, nh=nh, n_back=WINDOW // t, gate_base=gate_base),
        grid=(B, H // nh, n_t),
        in_specs=[pl.BlockSpec((1, nh, 1, HEAD_DIM, t), lambda b, g, i: (b, g, i, 0, 0)),
                  pl.BlockSpec((1, S, HEAD_DIM), lambda b, g, i: (b, 0, g)),
                  pl.BlockSpec((1, 1, n_t, HEAD_DIM, t), lambda b, g, i: (b, g, 0, 0, 0)),
                  pl.BlockSpec((1, t, LANES), lambda b, g, i: (b, i, 0))],
        out_specs=pl.BlockSpec((1, t, nh * HEAD_DIM), lambda b, g, i: (b, i, g)),
        out_shape=jax.ShapeDtypeStruct((B, S, WIDTH), BF16),
        compiler_params=_params("parallel", "parallel", "arbitrary"),
        name="win_attn",
    )(q_t, kw, vw_t, bg)


def _nsa_layer(x, norm, w_in, q_norm, kc_norm, ks_norm, kw_norm, cmp_wk, cmp_wv, cmp_pos, w_out, cos2, sin2):
    B, S, D = x.shape
    x2 = x.reshape(B * S, D)
    kv_end = WIDTH + 6 * KV_WIDTH
    gate_end = kv_end + WIDTH
    w_main = jnp.concatenate([w_in[:, :WIDTH], w_in[:, kv_end:gate_end], w_in[:, WIDTH:kv_end]], axis=1)
    w_bg = jnp.pad(w_in[:, gate_end:], ((0, 0), (0, LANES - 3 * N_HEADS)))
    h2 = _in_proj(x2, norm, w_main.astype(BF16), F32, 1024, 1024)
    bg = _in_proj(x2, norm, w_bg.astype(BF16), F32, 512, LANES).reshape(B, S, LANES)
    h = h2.reshape(B, S, -1)
    cos_c, sin_c = _rope_tables(jnp.arange(S // CMP_STRIDE) * CMP_STRIDE + (CMP_BLOCK - 1))
    kc_c, vc_ct = _nsa_compress(h, cmp_wk, cmp_wv, cmp_pos, kc_norm, cos_c, sin_c)
    ks_aug, vs_t, kw_b, vw_t = _nsa_prep_kv(h, ks_norm, kw_norm, cos2, sin2)
    o_cmp, q_t = _nsa_cmp_sel(h, bg, q_norm, cos2, sin2, kc_c, vc_ct)
    o_slc = _sel_attn(q_t, ks_aug, vs_t, bg, N_HEADS)
    o_win = _win_attn(q_t, kw_b, vw_t, bg, 2 * N_HEADS)
    os = [o.reshape(B * S, WIDTH) for o in (o_cmp, o_slc, o_win)]
    out = _out_proj(x2, h2, 1, w_out.astype(BF16), os)
    return out.reshape(B, S, D)


def kernel(x, l0_norm, l0_w_in, l0_q_norm, l0_k_norm, l0_w_out, l1_norm, l1_w_in, l1_w_out, l2_norm, l2_w_in, l2_q_norm, l2_kc_norm, l2_ks_norm, l2_kw_norm, l2_cmp_wk, l2_cmp_wv, l2_cmp_pos, l2_w_out, l3_norm, l3_w_in, l3_q_norm, l3_k_norm, l3_w_out):
    cos2, sin2 = _rope_tables(jnp.arange(x.shape[1]))
    x = _moba_layer(x, l0_norm, l0_w_in, l0_q_norm, l0_k_norm, l0_w_out, cos2, sin2)
    x = _sb_layer(x, l1_norm, l1_w_in, l1_w_out)
    x = _nsa_layer(x, l2_norm, l2_w_in, l2_q_norm, l2_kc_norm, l2_ks_norm, l2_kw_norm,
                   l2_cmp_wk, l2_cmp_wv, l2_cmp_pos, l2_w_out, cos2, sin2)
    x = _moba_layer(x, l3_norm, l3_w_in, l3_q_norm, l3_k_norm, l3_w_out, cos2, sin2)
    return x
```

```python
import functools
import math

import jax
import jax.numpy as jnp
from jax import lax
from jax.experimental import pallas as pl
from jax.experimental.pallas import tpu as pltpu

F32 = jnp.float32
BF16 = jnp.bfloat16
HIGHEST = lax.Precision.HIGHEST

D_MODEL = 2048
HEAD_DIM = 128
N_HEADS = 16
WIDTH = N_HEADS * HEAD_DIM
ROPE_THETA = 10000.0
EPS = 1e-6
ATTN_SCALE = HEAD_DIM ** -0.5
LOG2E = math.log2(math.e)
NEG = -1e30
LOWEST = -3e38
FORCE = 1e30
TINY = 1e-30
MOBA_BLOCK = 256
MOBA_TOPK = 3
NSA_KV_HEADS = 4
NSA_HPG = N_HEADS // NSA_KV_HEADS
KV_WIDTH = NSA_KV_HEADS * HEAD_DIM
CMP_BLOCK = 32
CMP_STRIDE = 16
SLC_BLOCK = 64
SLC_TOPK = 16
WINDOW = 512

LANES = 128
AUG = 2 * HEAD_DIM
ATT_TILE = 256
HEADS_PER_STEP = 4
VMEM_LIMIT = 48 * 1024 * 1024

_NT = (((1,), (1,)), ((), ()))


def _params(*sem):
    return pltpu.CompilerParams(dimension_semantics=sem, vmem_limit_bytes=VMEM_LIMIT)


def _rope_tables(pos):
    half = HEAD_DIM // 2
    inv_freq = jnp.exp(-math.log(ROPE_THETA) * jnp.arange(half, dtype=F32) / half)
    ang = pos.astype(F32)[:, None] * inv_freq[None, :]
    cos, sin = jnp.cos(ang), jnp.sin(ang)
    return jnp.concatenate([cos, cos], axis=-1), jnp.concatenate([-sin, sin], axis=-1)


def _rms_rope(x, g, cos2, sin2):
    y = x * lax.rsqrt(jnp.mean(x * x, axis=-1, keepdims=True) + EPS) * g
    return y * cos2 + pltpu.roll(y, HEAD_DIM // 2, 1) * sin2


def _iota(shape, dim):
    return lax.broadcasted_iota(jnp.int32, shape, dim)


def _log2(n):
    s = n.bit_length() - 1
    assert 1 << s == n
    return s


def _store_tiles_t(dst_ref, lead, x, t):
    for j in range(x.shape[0] // t):
        dst_ref[lead + (j,)] = x[j * t:(j + 1) * t, :].T.astype(dst_ref.dtype)


def _in_proj_body(x_ref, g_ref, w_ref, o_ref, xn_ref):
    @pl.when(pl.program_id(1) == 0)
    def _():
        x = x_ref[...]
        r = lax.rsqrt(jnp.mean(x * x, axis=-1, keepdims=True) + EPS)
        xn_ref[...] = (x * r * g_ref[...]).astype(xn_ref.dtype)

    o_ref[...] = jnp.dot(xn_ref[...], w_ref[...], preferred_element_type=F32).astype(o_ref.dtype)


def _in_proj(x2, g, w, out_dtype, tm, tn):
    M, K = x2.shape
    N = w.shape[1]
    return pl.pallas_call(
        _in_proj_body,
        grid=(M // tm, N // tn),
        in_specs=[pl.BlockSpec((tm, K), lambda i, j: (i, 0)),
                  pl.BlockSpec((1, K), lambda i, j: (0, 0)),
                  pl.BlockSpec((K, tn), lambda i, j: (0, j))],
        out_specs=pl.BlockSpec((tm, tn), lambda i, j: (i, j)),
        out_shape=jax.ShapeDtypeStruct((M, N), out_dtype),
        scratch_shapes=[pltpu.VMEM((tm, K), BF16)],
        compiler_params=_params("parallel", "arbitrary"),
        name="in_proj",
    )(x2, g.reshape(1, K), w)


def _out_proj_body(*refs, n_o):
    x_ref, gate_ref, w_ref = refs[0], refs[1], refs[2]
    o_refs = refs[3:3 + n_o]
    out_ref = refs[3 + n_o]
    o = o_refs[0][...].astype(F32)
    for r in o_refs[1:]:
        o = o + r[...].astype(F32)
    gate = gate_ref[...].astype(F32)
    y = o * (gate / (1.0 + jnp.exp(-gate)))
    out_ref[...] = x_ref[...] + jnp.dot(y.astype(BF16), w_ref[...], preferred_element_type=F32)


def _out_proj(x2, h2, gate_blk, w, os, tm=256):
    M, D = x2.shape
    n_o = len(os)
    row = pl.BlockSpec((tm, D), lambda i: (i, 0))
    return pl.pallas_call(
        functools.partial(_out_proj_body, n_o=n_o),
        grid=(M // tm,),
        in_specs=[row, pl.BlockSpec((tm, D), lambda i: (i, gate_blk)),
                  pl.BlockSpec((D, D), lambda i: (0, 0))] + [row] * n_o,
        out_specs=row,
        out_shape=jax.ShapeDtypeStruct((M, D), F32),
        compiler_params=_params("parallel"),
        name="out_proj",
    )(x2, h2, w, *os)


def _branch_gate(bg_ref, col):
    bg = bg_ref[0]
    g = 1.0 / (1.0 + jnp.exp(-bg))
    return jnp.sum(jnp.where(_iota(bg.shape, 1) == col, g, 0.0), axis=1, keepdims=True)


def _softmax_block(ml_ref, acc_ref, r, s, vt, first):
    if first:
        m_new = jnp.max(s, axis=0, keepdims=True)
        p = jnp.exp2(s - m_new)
        l = jnp.sum(p, axis=0, keepdims=True)
        acc_ref[r] = jnp.dot(vt, p.astype(BF16), preferred_element_type=F32)
    else:
        m = ml_ref[2 * r:2 * r + 1, :]
        m_new = jnp.maximum(m, jnp.max(s, axis=0, keepdims=True))
        alpha = jnp.exp2(m - m_new)
        p = jnp.exp2(s - m_new)
        l = alpha * ml_ref[2 * r + 1:2 * r + 2, :] + jnp.sum(p, axis=0, keepdims=True)
        acc_ref[r] = alpha * acc_ref[r] + jnp.dot(vt, p.astype(BF16), preferred_element_type=F32)
    ml_ref[2 * r:2 * r + 1, :] = m_new
    ml_ref[2 * r + 1:2 * r + 2, :] = l


def _store_heads(o_ref, sums_accs, bg_ref, gate_col0):
    for r, (l, acc) in enumerate(sums_accs):
        o = (acc * (1.0 / l)).T
        if bg_ref is not None:
            o = o * _branch_gate(bg_ref, gate_col0 + r)
        o_ref[0, :, r * HEAD_DIM:(r + 1) * HEAD_DIM] = o.astype(o_ref.dtype)


def _attend_causal(qts, k_ref, vt_ref, ml_ref, acc_ref, qi, t, hpg):
    nh = len(qts)
    w = 2 * t

    def block(j0, diag):
        off = pl.multiple_of(j0 * t, w)
        ss = [jnp.dot(k_ref[0, pl.ds(off, w), (r // hpg) * AUG:(r // hpg + 1) * AUG], qts[r],
                      preferred_element_type=F32) for r in range(nh)]
        if diag:
            causal = _iota((w, w), 0) <= _iota((w, w), 1)
        for r in range(nh):
            s = jnp.where(causal, ss[r], NEG) if diag else ss[r]
            vt = jnp.concatenate([vt_ref[0, r // hpg, j0], vt_ref[0, r // hpg, j0 + 1]], axis=1)
            _softmax_block(ml_ref, acc_ref, r, s, vt, diag)

    block(2 * qi, True)

    def body(j2, carry):
        block(2 * j2, False)
        return carry

    lax.fori_loop(0, qi, body, 0)
    return [(ml_ref[2 * r + 1:2 * r + 2, :], acc_ref[r]) for r in range(nh)]


def _attn_scratch(nh, t):
    return [pltpu.VMEM((2 * nh, 2 * t), F32), pltpu.VMEM((nh, HEAD_DIM, 2 * t), F32)]


def _sel_attn_body(qt_ref, k_ref, vt_ref, bg_ref, o_ref, ml_ref, acc_ref, *, t, nh, gate_base):
    qts = [jnp.concatenate([qt_ref[0, r, 0], qt_ref[0, r, 1]], axis=1) for r in range(nh)]
    sums_accs = _attend_causal(qts, k_ref, vt_ref, ml_ref, acc_ref, pl.program_id(2), t, NSA_HPG)
    _store_heads(o_ref, sums_accs, bg_ref, gate_base + pl.program_id(1) * nh)


def _sel_attn(q_t, k_aug, v_t, bg, gate_base):
    B, H, n_t, _, t = q_t.shape
    S = n_t * t
    nh = NSA_HPG
    assert n_t % 2 == 0
    return pl.pallas_call(
        functools.partial(_sel_attn_body, t=t, nh=nh, gate_base=gate_base),
        grid=(B, H // nh, n_t // 2),
        in_specs=[pl.BlockSpec((1, nh, 2, AUG, t), lambda b, h, i: (b, h, i, 0, 0)),
                  pl.BlockSpec((1, S, AUG), lambda b, h, i: (b, 0, h)),
                  pl.BlockSpec((1, 1, n_t, HEAD_DIM, t), lambda b, h, i: (b, h, 0, 0, 0)),
                  pl.BlockSpec((1, 2 * t, LANES), lambda b, h, i: (b, i, 0))],
        out_specs=pl.BlockSpec((1, 2 * t, nh * HEAD_DIM), lambda b, h, i: (b, i, h)),
        out_shape=jax.ShapeDtypeStruct((B, S, WIDTH), BF16),
        scratch_shapes=_attn_scratch(nh, t),
        compiler_params=_params("parallel", "parallel", "arbitrary"),
        name="sel_attn",
    )(q_t, k_aug, v_t, bg)


def _onehot_block(S, shift):
    return jnp.where((_iota((S, LANES), 0) >> shift) == _iota((S, LANES), 1), 1.0, 0.0).astype(BF16)


def _moba_prep_k_body(k_ref, v_ref, kn_ref, cos_ref, sin_ref, kaug_ref, vt_ref, kmean_ref, *, nb, t):
    kn = _rms_rope(k_ref[0], kn_ref[...], cos_ref[...], sin_ref[...])
    S = kn.shape[0]
    kaug_ref[0, :, :HEAD_DIM] = kn.astype(BF16)
    kaug_ref[0, :, HEAD_DIM:] = _onehot_block(S, _log2(MOBA_BLOCK))
    _store_tiles_t(vt_ref, (0, 0), v_ref[0], t)
    kmean_ref[0, 0] = jnp.mean(kn.reshape(nb, MOBA_BLOCK, HEAD_DIM), axis=1)


def _moba_prep_k(h, k_norm, cos2, sin2):
    B, S, _ = h.shape
    nb = S // MOBA_BLOCK
    t = ATT_TILE
    assert 1 < nb <= HEAD_DIM
    full = pl.BlockSpec((S, HEAD_DIM), lambda b, hh: (0, 0))
    return pl.pallas_call(
        functools.partial(_moba_prep_k_body, nb=nb, t=t),
        grid=(B, N_HEADS),
        in_specs=[pl.BlockSpec((1, S, HEAD_DIM), lambda b, hh: (b, 0, N_HEADS + hh)),
                  pl.BlockSpec((1, S, HEAD_DIM), lambda b, hh: (b, 0, 2 * N_HEADS + hh)),
                  pl.BlockSpec((1, HEAD_DIM), lambda b, hh: (0, 0)), full, full],
        out_specs=[pl.BlockSpec((1, S, AUG), lambda b, hh: (b, 0, hh)),
                   pl.BlockSpec((1, 1, S // t, HEAD_DIM, t), lambda b, hh: (b, hh, 0, 0, 0)),
                   pl.BlockSpec((1, 1, nb, HEAD_DIM), lambda b, hh: (b, hh, 0, 0))],
        out_shape=[jax.ShapeDtypeStruct((B, S, N_HEADS * AUG), BF16),
                   jax.ShapeDtypeStruct((B, N_HEADS, S // t, HEAD_DIM, t), BF16),
                   jax.ShapeDtypeStruct((B, N_HEADS, nb, HEAD_DIM), F32)],
        compiler_params=_params("parallel", "parallel"),
        name="moba_prep_k",
    )(h, h, k_norm.reshape(1, HEAD_DIM), cos2, sin2)


def _moba_attn_body(q_ref, qn_ref, cos_ref, sin_ref, kmean_ref, k_ref, vt_ref, o_ref, ml_ref, acc_ref,
                    *, t, nh, nb, n_sel):
    qi = pl.program_id(2)
    cos2, sin2 = cos_ref[...], sin_ref[...]
    w = 2 * t
    blk = _iota((nb, w), 0)
    blk_f = blk.astype(F32)
    cur = 2 * qi + (_iota((nb, w), 1) >> _log2(t))
    qts = []
    for r in range(nh):
        qn = _rms_rope(q_ref[0, :, r * HEAD_DIM:(r + 1) * HEAD_DIM], qn_ref[...], cos2, sin2)
        qnt = qn.T
        gate = jnp.dot(kmean_ref[0, r], qnt, precision=HIGHEST, preferred_element_type=F32)
        g = jnp.where(blk < cur, gate, NEG)
        picked = jnp.zeros((nb, w), F32)
        for _ in range(n_sel):
            mx = jnp.max(g, axis=0, keepdims=True)
            first = jnp.min(jnp.where(g == mx, blk_f, float(nb)), axis=0, keepdims=True)
            hit = blk_f == first
            picked = jnp.where(hit, 1.0, picked)
            g = jnp.where(hit, LOWEST, g)
        attend = jnp.where(blk < cur, picked, jnp.where(blk == cur, 1.0, 0.0))
        bias = jnp.where(attend > 0.0, 0.0, NEG)
        qts.append(jnp.concatenate([qnt * (ATTN_SCALE * LOG2E), bias, jnp.zeros((HEAD_DIM - nb, w), F32)],
                                   axis=0).astype(BF16))
    _store_heads(o_ref, _attend_causal(qts, k_ref, vt_ref, ml_ref, acc_ref, qi, t, 1), None, None)


def _moba_attn(h, q_norm, cos2, sin2, kmean, k_aug, v_t):
    B, S, _ = h.shape
    t = MOBA_BLOCK
    assert t == ATT_TILE
    nb = S // MOBA_BLOCK
    assert nb % 2 == 0
    nh = HEADS_PER_STEP
    n_sel = min(MOBA_TOPK, nb - 1)
    tab = pl.BlockSpec((2 * t, HEAD_DIM), lambda b, hh, i: (i, 0))
    return pl.pallas_call(
        functools.partial(_moba_attn_body, t=t, nh=nh, nb=nb, n_sel=n_sel),
        grid=(B, N_HEADS // nh, nb // 2),
        in_specs=[pl.BlockSpec((1, 2 * t, nh * HEAD_DIM), lambda b, hh, i: (b, i, hh)),
                  pl.BlockSpec((1, HEAD_DIM), lambda b, hh, i: (0, 0)), tab, tab,
                  pl.BlockSpec((1, nh, nb, HEAD_DIM), lambda b, hh, i: (b, hh, 0, 0)),
                  pl.BlockSpec((1, S, nh * AUG), lambda b, hh, i: (b, 0, hh)),
                  pl.BlockSpec((1, nh, nb, HEAD_DIM, t), lambda b, hh, i: (b, hh, 0, 0, 0))],
        out_specs=pl.BlockSpec((1, 2 * t, nh * HEAD_DIM), lambda b, hh, i: (b, i, hh)),
        out_shape=jax.ShapeDtypeStruct((B, S, WIDTH), BF16),
        scratch_shapes=_attn_scratch(nh, t),
        compiler_params=_params("parallel", "parallel", "arbitrary"),
        name="moba_attn",
    )(h, q_norm.reshape(1, HEAD_DIM), cos2, sin2, kmean, k_aug, v_t)


def _moba_layer(x, norm, w_in, q_norm, k_norm, w_out, cos2, sin2):
    B, S, D = x.shape
    x2 = x.reshape(B * S, D)
    h2 = _in_proj(x2, norm, w_in.astype(BF16), F32, 1024, 1024)
    h = h2.reshape(B, S, -1)
    k_aug, v_t, kmean = _moba_prep_k(h, k_norm, cos2, sin2)
    o = _moba_attn(h, q_norm, cos2, sin2, kmean, k_aug, v_t)
    out = _out_proj(x2, h2, 3, w_out.astype(BF16), [o.reshape(B * S, WIDTH)])
    return out.reshape(B, S, D)


def _split2_dot(a, u):
    hi = a.astype(BF16)
    lo = (a - hi.astype(F32)).astype(BF16)
    return jnp.dot(hi, u, preferred_element_type=F32) + jnp.dot(lo, u, preferred_element_type=F32)


def _sb_attn_body(q_ref, k_ref, v_ref, o_ref, kt_ref, carry_ref, acc_ref, *, t, nh):
    qi = pl.program_id(2)

    @pl.when(qi == 0)
    def _():
        for r in range(nh):
            _store_tiles_t(kt_ref, (r,), k_ref[0, :, r * HEAD_DIM:(r + 1) * HEAD_DIM].astype(F32), t)

    after = jnp.where(_iota((t, t), 0) > _iota((t, t), 1), 1.0, 0.0).astype(BF16)
    qs = [(q_ref[0, :, r * HEAD_DIM:(r + 1) * HEAD_DIM].astype(F32) * ATTN_SCALE).astype(BF16)
          for r in range(nh)]

    def block(j0, n, diag):
        zs = [jnp.dot(qs[r], jnp.concatenate([kt_ref[r, j0 + i] for i in range(n)], axis=1),
                      preferred_element_type=F32) for r in range(nh)]
        if diag:
            past = _iota((t, n * t), 1) < _iota((t, n * t), 0) + (n - 1) * t
        staged = []
        for r in range(nh):
            z = zs[r]
            drop = jnp.maximum(z, 0.0) + jnp.log(1.0 + jnp.exp2(jnp.abs(z) * (-LOG2E)))
            if diag:
                drop = jnp.where(past, drop, 0.0)
            tiles = []
            for i in range(n):
                d = drop[:, i * t:(i + 1) * t]
                suffix = _split2_dot(d, after)
                tiles.append((suffix, suffix[:, 0:1] + d[:, 0:1]))
            staged.append((drop, tiles))
        for r in range(nh):
            drop, tiles = staged[r]
            vs = v_ref[0, pl.ds(pl.multiple_of(j0 * t, t), n * t), r * HEAD_DIM:(r + 1) * HEAD_DIM]
            carry = None if diag else carry_ref[r]
            behind = [None] * n
            for i in reversed(range(n)):
                behind[i] = tiles[i][0] if carry is None else tiles[i][0] + carry
                carry = tiles[i][1] if carry is None else carry + tiles[i][1]
            a = jnp.exp(zs[r] - (drop + jnp.concatenate(behind, axis=1)))
            if diag:
                a = jnp.where(past, a, 0.0)
            o = jnp.dot(a.astype(BF16), vs, preferred_element_type=F32)
            acc_ref[r] = o if diag else acc_ref[r] + o
            carry_ref[r] = carry

    @pl.when((qi & 1) == 1)
    def _():
        block(qi - 1, 2, True)

    @pl.when((qi & 1) == 0)
    def _():
        block(qi, 1, True)

    n_pairs = qi >> 1

    def body(i, carry):
        block(2 * (n_pairs - 1 - i), 2, False)
        return carry

    lax.fori_loop(0, n_pairs, body, 0)
    for r in range(nh):
        o_ref[0, :, r * HEAD_DIM:(r + 1) * HEAD_DIM] = acc_ref[r].astype(o_ref.dtype)


def _sb_attn(h):
    B, S, _ = h.shape
    t = ATT_TILE
    nh = HEADS_PER_STEP
    nhb = N_HEADS // nh
    w = nh * HEAD_DIM
    return pl.pallas_call(
        functools.partial(_sb_attn_body, t=t, nh=nh),
        grid=(B, nhb, S // t),
        in_specs=[pl.BlockSpec((1, t, w), lambda b, hh, i: (b, i, hh)),
                  pl.BlockSpec((1, S, w), lambda b, hh, i: (b, 0, nhb + hh)),
                  pl.BlockSpec((1, S, w), lambda b, hh, i: (b, 0, 2 * nhb + hh))],
        out_specs=pl.BlockSpec((1, t, w), lambda b, hh, i: (b, i, hh)),
        out_shape=jax.ShapeDtypeStruct((B, S, WIDTH), BF16),
        scratch_shapes=[pltpu.VMEM((nh, S // t, HEAD_DIM, t), BF16), pltpu.VMEM((nh, t, 1), F32),
                        pltpu.VMEM((nh, t, HEAD_DIM), F32)],
        compiler_params=_params("parallel", "parallel", "arbitrary"),
        name="sb_attn",
    )(h, h, h)


def _sb_layer(x, norm, w_in, w_out):
    B, S, D = x.shape
    x2 = x.reshape(B * S, D)
    h2 = _in_proj(x2, norm, w_in.astype(BF16), BF16, 1024, 1024)
    o = _sb_attn(h2.reshape(B, S, -1))
    out = _out_proj(x2, h2, 3, w_out.astype(BF16), [o.reshape(B * S, WIDTH)])
    return out.reshape(B, S, D)


NSA_KV_BLK0 = 2 * N_HEADS


def _nsa_compress_body(kc_ref, vc_ref, wk_ref, wv_ref, pos_ref, kcn_ref, cosc_ref, sinc_ref,
                       kcc_ref, vcct_ref, *, n_rows):
    half = CMP_BLOCK // CMP_STRIDE
    assert half == 2

    def compress(src_ref, w_ref):
        first = jnp.zeros((n_rows, HEAD_DIM), F32)
        second = jnp.zeros((n_rows, HEAD_DIM), F32)
        for l in range(CMP_STRIDE):
            rows = src_ref[0, pl.ds(l, n_rows, stride=CMP_STRIDE), :]
            first = first + jnp.dot(rows + pos_ref[l:l + 1, :], w_ref[l],
                                    precision=HIGHEST, preferred_element_type=F32)
            second = second + jnp.dot(rows + pos_ref[CMP_STRIDE + l:CMP_STRIDE + l + 1, :],
                                      w_ref[CMP_STRIDE + l], precision=HIGHEST, preferred_element_type=F32)
        return first + pltpu.roll(second, n_rows - 1, 0)

    kc = compress(kc_ref, wk_ref)
    kcc_ref[0, 0] = _rms_rope(kc, kcn_ref[...], cosc_ref[...], sinc_ref[...])
    vcct_ref[0, 0] = compress(vc_ref, wv_ref).T.astype(vcct_ref.dtype)


def _nsa_compress(h, cmp_wk, cmp_wv, cmp_pos, kc_norm, cos_c, sin_c):
    B, S, _ = h.shape
    n_rows = S // CMP_STRIDE
    wspec = pl.BlockSpec((CMP_BLOCK, HEAD_DIM, HEAD_DIM), lambda b, g: (0, 0, 0))
    tab = pl.BlockSpec((n_rows, HEAD_DIM), lambda b, g: (0, 0))
    out = pl.BlockSpec((1, 1, n_rows, HEAD_DIM), lambda b, g: (b, g, 0, 0))
    return pl.pallas_call(
        functools.partial(_nsa_compress_body, n_rows=n_rows),
        grid=(B, NSA_KV_HEADS),
        in_specs=[pl.BlockSpec((1, S, HEAD_DIM), lambda b, g: (b, 0, NSA_KV_BLK0 + g)),
                  pl.BlockSpec((1, S, HEAD_DIM), lambda b, g: (b, 0, NSA_KV_BLK0 + NSA_KV_HEADS + g)),
                  wspec, wspec,
                  pl.BlockSpec((CMP_BLOCK, HEAD_DIM), lambda b, g: (0, 0)),
                  pl.BlockSpec((1, HEAD_DIM), lambda b, g: (0, 0)), tab, tab],
        out_specs=[out, pl.BlockSpec((1, 1, HEAD_DIM, n_rows), lambda b, g: (b, g, 0, 0))],
        out_shape=[jax.ShapeDtypeStruct((B, NSA_KV_HEADS, n_rows, HEAD_DIM), F32),
                   jax.ShapeDtypeStruct((B, NSA_KV_HEADS, HEAD_DIM, n_rows), BF16)],
        compiler_params=_params("parallel", "parallel"),
        name="nsa_compress",
    )(h, h, cmp_wk, cmp_wv, cmp_pos, kc_norm.reshape(1, HEAD_DIM), cos_c, sin_c)


def _nsa_prep_kv_body(ks_ref, vs_ref, kw_ref, vw_ref, ksn_ref, kwn_ref, cos_ref, sin_ref,
                      ksaug_ref, vst_ref, kwb_ref, vwt_ref, *, t):
    cos2, sin2 = cos_ref[...], sin_ref[...]
    S = cos2.shape[0]
    ksaug_ref[0, :, :HEAD_DIM] = _rms_rope(ks_ref[0], ksn_ref[...], cos2, sin2).astype(BF16)
    ksaug_ref[0, :, HEAD_DIM:] = _onehot_block(S, _log2(SLC_BLOCK))
    kwb_ref[0] = _rms_rope(kw_ref[0], kwn_ref[...], cos2, sin2).astype(BF16)
    _store_tiles_t(vst_ref, (0, 0), vs_ref[0], t)
    _store_tiles_t(vwt_ref, (0, 0), vw_ref[0], t)


def _nsa_prep_kv(h, ks_norm, kw_norm, cos2, sin2):
    B, S, _ = h.shape
    G = NSA_KV_HEADS
    t = ATT_TILE

    def src(i):
        return pl.BlockSpec((1, S, HEAD_DIM), lambda b, g: (b, 0, NSA_KV_BLK0 + i * G + g))

    vec = pl.BlockSpec((1, HEAD_DIM), lambda b, g: (0, 0))
    full = pl.BlockSpec((S, HEAD_DIM), lambda b, g: (0, 0))
    vt = pl.BlockSpec((1, 1, S // t, HEAD_DIM, t), lambda b, g: (b, g, 0, 0, 0))
    vt_shape = jax.ShapeDtypeStruct((B, G, S // t, HEAD_DIM, t), BF16)
    return pl.pallas_call(
        functools.partial(_nsa_prep_kv_body, t=t),
        grid=(B, G),
        in_specs=[src(2), src(3), src(4), src(5), vec, vec, full, full],
        out_specs=[pl.BlockSpec((1, S, AUG), lambda b, g: (b, 0, g)), vt,
                   pl.BlockSpec((1, S, HEAD_DIM), lambda b, g: (b, 0, g)), vt],
        out_shape=[jax.ShapeDtypeStruct((B, S, G * AUG), BF16), vt_shape,
                   jax.ShapeDtypeStruct((B, S, KV_WIDTH), BF16), vt_shape],
        compiler_params=_params("parallel", "parallel"),
        name="nsa_prep_kv",
    )(h, h, h, h, ks_norm.reshape(1, HEAD_DIM), kw_norm.reshape(1, HEAD_DIM), cos2, sin2)


def _nsa_cmp_sel_body(q_ref, qn_ref, cos_ref, sin_ref, kcc_ref, vcct_ref, bg_ref, ocmp_ref, qt_ref,
                      *, t, n_cmp, n_slc, n_sel):
    g_idx = pl.program_id(1)
    q0 = pl.program_id(2) * t
    cos2, sin2 = cos_ref[...], sin_ref[...]
    kcc = kcc_ref[0, 0]
    kcc_hi = kcc.astype(BF16)
    kcc_lo = (kcc - kcc_hi.astype(F32)).astype(BF16)
    kcc3 = jnp.concatenate([kcc_hi, kcc_hi, kcc_lo], axis=1)
    vcct = vcct_ref[0, 0]
    cmp_ok = _iota((n_cmp, t), 0) * CMP_STRIDE + (CMP_BLOCK - 1) <= q0 + _iota((n_cmp, t), 1)

    p_sum = jnp.zeros((n_cmp, t), F32)
    for r in range(NSA_HPG):
        qnt = _rms_rope(q_ref[0, :, r * HEAD_DIM:(r + 1) * HEAD_DIM], qn_ref[...], cos2, sin2).T
        qt_ref[0, r, 0, :HEAD_DIM, :] = (qnt * (ATTN_SCALE * LOG2E)).astype(BF16)
        q_hi = qnt.astype(BF16)
        q_lo = (qnt - q_hi.astype(F32)).astype(BF16)
        logits = jnp.dot(kcc3, jnp.concatenate([q_hi, q_lo, q_hi], axis=0),
                         preferred_element_type=F32) * ATTN_SCALE
        logits = jnp.where(cmp_ok, logits, NEG)
        mx = jnp.max(logits, axis=0, keepdims=True)
        e = jnp.where(cmp_ok, jnp.exp(logits - mx), 0.0)
        p = e * (1.0 / jnp.maximum(jnp.sum(e, axis=0, keepdims=True), TINY))
        p_sum = p_sum + p
        o = jnp.dot(vcct, p.astype(BF16), preferred_element_type=F32).T
        o = o * _branch_gate(bg_ref, g_idx * NSA_HPG + r)
        ocmp_ref[0, :, r * HEAD_DIM:(r + 1) * HEAD_DIM] = o.astype(ocmp_ref.dtype)

    c_start = _iota((n_slc, n_cmp), 1) * CMP_STRIDE
    s_start = _iota((n_slc, n_cmp), 0) * SLC_BLOCK
    overlap = jnp.where((c_start < s_start + SLC_BLOCK) & (c_start + CMP_BLOCK > s_start), 1.0, 0.0).astype(BF16)
    ps_hi = p_sum.astype(BF16)
    rest = p_sum - ps_hi.astype(F32)
    ps_mid = rest.astype(BF16)
    ps_lo = (rest - ps_mid.astype(F32)).astype(BF16)
    imp = jnp.dot(jnp.concatenate([overlap, overlap, overlap], axis=1),
                  jnp.concatenate([ps_hi, ps_mid, ps_lo], axis=0), preferred_element_type=F32)
    blk = _iota((n_slc, t), 0)
    cur = (q0 + _iota((n_slc, t), 1)) >> _log2(SLC_BLOCK)
    imp = jnp.where(blk == 0, FORCE, imp)
    imp = jnp.where(blk == cur, FORCE, imp)
    imp = jnp.where(blk == cur - 1, FORCE, imp)
    imp = jnp.where(blk <= cur, imp, NEG)
    sub = 8
    ranks = []
    for g0 in range(0, n_slc, sub):
        mine = imp[g0:g0 + sub, :]
        row = _iota((sub, t), 0) + g0
        rank = jnp.zeros((sub, t), F32)
        for k in range(n_slc):
            c = imp[k:k + 1, :]
            ge = jnp.where(c >= mine, 1.0, 0.0)
            gt = jnp.where(c > mine, 1.0, 0.0)
            rank = rank + (ge if k < g0 else gt if k >= g0 + sub else jnp.where(row > k, ge, gt))
        ranks.append(rank)
    bias = jnp.where(jnp.concatenate(ranks, axis=0) < float(n_sel), 0.0, NEG)
    bias = jnp.concatenate([bias, jnp.zeros((HEAD_DIM - n_slc, t), F32)], axis=0).astype(BF16)
    for r in range(NSA_HPG):
        qt_ref[0, r, 0, HEAD_DIM:, :] = bias


def _nsa_cmp_sel(h, bg, q_norm, cos2, sin2, kc_c, vc_ct):
    B, S, _ = h.shape
    t = ATT_TILE
    G, R = NSA_KV_HEADS, NSA_HPG
    n_cmp = S // CMP_STRIDE
    n_slc = S // SLC_BLOCK
    assert n_slc <= HEAD_DIM and n_slc % 8 == 0
    tab = pl.BlockSpec((t, HEAD_DIM), lambda b, g, i: (i, 0))
    return pl.pallas_call(
        functools.partial(_nsa_cmp_sel_body, t=t, n_cmp=n_cmp, n_slc=n_slc, n_sel=min(SLC_TOPK, n_slc)),
        grid=(B, G, S // t),
        in_specs=[pl.BlockSpec((1, t, R * HEAD_DIM), lambda b, g, i: (b, i, g)),
                  pl.BlockSpec((1, HEAD_DIM), lambda b, g, i: (0, 0)), tab, tab,
                  pl.BlockSpec((1, 1, n_cmp, HEAD_DIM), lambda b, g, i: (b, g, 0, 0)),
                  pl.BlockSpec((1, 1, HEAD_DIM, n_cmp), lambda b, g, i: (b, g, 0, 0)),
                  pl.BlockSpec((1, t, LANES), lambda b, g, i: (b, i, 0))],
        out_specs=[pl.BlockSpec((1, t, R * HEAD_DIM), lambda b, g, i: (b, i, g)),
                   pl.BlockSpec((1, R, 1, AUG, t), lambda b, g, i: (b, g, i, 0, 0))],
        out_shape=[jax.ShapeDtypeStruct((B, S, WIDTH), BF16),
                   jax.ShapeDtypeStruct((B, N_HEADS, S // t, AUG, t), BF16)],
        compiler_params=_params("parallel", "parallel", "arbitrary"),
        name="nsa_cmp_sel",
    )(h, q_norm.reshape(1, HEAD_DIM), cos2, sin2, kc_c, vc_ct, bg)


def _win_attn_body(qt_ref, k_ref, vt_ref, bg_ref, o_ref, *, t, nh, n_back, gate_base):
    qi = pl.program_id(2)
    n_blk = n_back + 1
    j0 = jnp.maximum(qi - n_back, 0)
    k = k_ref[0, pl.ds(pl.multiple_of(j0 * t, t), n_blk * t), :]
    vt = jnp.concatenate([vt_ref[0, 0, j0 + d] for d in range(n_blk)], axis=1)
    dist = _iota((n_blk * t, t), 1) - _iota((n_blk * t, t), 0) + (qi - j0) * t
    ss = [jnp.dot(k, qt_ref[0, r, 0], preferred_element_type=F32) for r in range(nh)]
    states = []
    for r in range(nh):
        s = jnp.where(dist >= 0, ss[r], NEG)
        s = jnp.where(dist < WINDOW, s, NEG)
        m = jnp.max(s, axis=0, keepdims=True)
        p = jnp.exp2(s - m)
        states.append((jnp.sum(p, axis=0, keepdims=True), jnp.dot(vt, p.astype(BF16), preferred_element_type=F32)))
    _store_heads(o_ref, states, bg_ref, gate_base + pl.program_id(1) * nh)


def _win_attn(q_t, kw, vw_t, bg, gate_base):
    B, H, n_t, _, t = q_t.shape
    S = n_t * t
    nh = NSA_HPG
    assert WINDOW % t == 0 and n_t > WINDOW // t
    return pl.pallas_call(
        functools.partial(_win_attn_body, t=t, nh=nh, n_back=WINDOW // t, gate_base=gate_base),
        grid=(B, H // nh, n_t),
        in_specs=[pl.BlockSpec((1, nh, 1, HEAD_DIM, t), lambda b, g, i: (b, g, i, 0, 0)),
                  pl.BlockSpec((1, S, HEAD_DIM), lambda b, g, i: (b, 0, g)),
                  pl.BlockSpec((1, 1, n_t, HEAD_DIM, t), lambda b, g, i: (b, g, 0, 0, 0)),
                  pl.BlockSpec((1, t, LANES), lambda b, g, i: (b, i, 0))],
        out_specs=pl.BlockSpec((1, t, nh * HEAD_DIM), lambda b, g, i: (b, i, g)),
        out_shape=jax.ShapeDtypeStruct((B, S, WIDTH), BF16),
        compiler_params=_params("parallel", "parallel", "arbitrary"),
        name="win_attn",
    )(q_t, kw, vw_t, bg)


def _nsa_layer(x, norm, w_in, q_norm, kc_norm, ks_norm, kw_norm, cmp_wk, cmp_wv, cmp_pos, w_out, cos2, sin2):
    B, S, D = x.shape
    x2 = x.reshape(B * S, D)
    kv_end = WIDTH + 6 * KV_WIDTH
    gate_end = kv_end + WIDTH
    w_main = jnp.concatenate([w_in[:, :WIDTH], w_in[:, kv_end:gate_end], w_in[:, WIDTH:kv_end]], axis=1)
    w_bg = jnp.pad(w_in[:, gate_end:], ((0, 0), (0, LANES - 3 * N_HEADS)))
    h2 = _in_proj(x2, norm, w_main.astype(BF16), F32, 1024, 1024)
    bg = _in_proj(x2, norm, w_bg.astype(BF16), F32, 512, LANES).reshape(B, S, LANES)
    h = h2.reshape(B, S, -1)
    cos_c, sin_c = _rope_tables(jnp.arange(S // CMP_STRIDE) * CMP_STRIDE + (CMP_BLOCK - 1))
    kc_c, vc_ct = _nsa_compress(h, cmp_wk, cmp_wv, cmp_pos, kc_norm, cos_c, sin_c)
    ks_aug, vs_t, kw_b, vw_t = _nsa_prep_kv(h, ks_norm, kw_norm, cos2, sin2)
    o_cmp, q_t = _nsa_cmp_sel(h, bg, q_norm, cos2, sin2, kc_c, vc_ct)
    o_slc = _sel_attn(q_t, ks_aug, vs_t, bg, N_HEADS)
    o_win = _win_attn(q_t, kw_b, vw_t, bg, 2 * N_HEADS)
    os = [o.reshape(B * S, WIDTH) for o in (o_cmp, o_slc, o_win)]
    out = _out_proj(x2, h2, 1, w_out.astype(BF16), os)
    return out.reshape(B, S, D)


def kernel(x, l0_norm, l0_w_in, l0_q_norm, l0_k_norm, l0_w_out, l1_norm, l1_w_in, l1_w_out, l2_norm, l2_w_in, l2_q_norm, l2_kc_norm, l2_ks_norm, l2_kw_norm, l2_cmp_wk, l2_cmp_wv, l2_cmp_pos, l2_w_out, l3_norm, l3_w_in, l3_q_norm, l3_k_norm, l3_w_out):
    cos2, sin2 = _rope_tables(jnp.arange(x.shape[1]))
    x = _moba_layer(x, l0_norm, l0_w_in, l0_q_norm, l0_k_norm, l0_w_out, cos2, sin2)
    x = _sb_layer(x, l1_norm, l1_w_in, l1_w_out)
    x = _nsa_layer(x, l2_norm, l2_w_in, l2_q_norm, l2_kc_norm, l2_ks_norm, l2_kw_norm,
                   l2_cmp_wk, l2_cmp_wv, l2_cmp_pos, l2_w_out, cos2, sin2)
    x = _moba_layer(x, l3_norm, l3_w_in, l3_q_norm, l3_k_norm, l3_w_out, cos2, sin2)
    return x
```

```python
import functools
import math

import jax
import jax.numpy as jnp
from jax import lax
from jax.experimental import pallas as pl
from jax.experimental.pallas import tpu as pltpu

F32 = jnp.float32
BF16 = jnp.bfloat16
HIGHEST = lax.Precision.HIGHEST

D_MODEL = 2048
HEAD_DIM = 128
N_HEADS = 16
WIDTH = N_HEADS * HEAD_DIM
ROPE_THETA = 10000.0
EPS = 1e-6
ATTN_SCALE = HEAD_DIM ** -0.5
LOG2E = math.log2(math.e)
NEG = -1e30
LOWEST = -3e38
FORCE = 1e30
TINY = 1e-30
MOBA_BLOCK = 256
MOBA_TOPK = 3
NSA_KV_HEADS = 4
NSA_HPG = N_HEADS // NSA_KV_HEADS
KV_WIDTH = NSA_KV_HEADS * HEAD_DIM
CMP_BLOCK = 32
CMP_STRIDE = 16
SLC_BLOCK = 64
SLC_TOPK = 16
WINDOW = 512

LANES = 128
AUG = 2 * HEAD_DIM
ATT_TILE = 256
HEADS_PER_STEP = 4
SB_CUTOFF = 110.0
VMEM_LIMIT = 48 * 1024 * 1024

_NT = (((1,), (1,)), ((), ()))


def _params(*sem):
    return pltpu.CompilerParams(dimension_semantics=sem, vmem_limit_bytes=VMEM_LIMIT)


def _rope_tables(pos):
    half = HEAD_DIM // 2
    inv_freq = jnp.exp(-math.log(ROPE_THETA) * jnp.arange(half, dtype=F32) / half)
    ang = pos.astype(F32)[:, None] * inv_freq[None, :]
    cos, sin = jnp.cos(ang), jnp.sin(ang)
    return jnp.concatenate([cos, cos], axis=-1), jnp.concatenate([-sin, sin], axis=-1)


def _rms_rope(x, g, cos2, sin2):
    y = x * lax.rsqrt(jnp.mean(x * x, axis=-1, keepdims=True) + EPS) * g
    return y * cos2 + pltpu.roll(y, HEAD_DIM // 2, 1) * sin2


def _iota(shape, dim):
    return lax.broadcasted_iota(jnp.int32, shape, dim)


def _log2(n):
    s = n.bit_length() - 1
    assert 1 << s == n
    return s


def _store_tiles_t(dst_ref, lead, x, t):
    for j in range(x.shape[0] // t):
        dst_ref[lead + (j,)] = x[j * t:(j + 1) * t, :].T.astype(dst_ref.dtype)


def _in_proj_body(x_ref, g_ref, w_ref, o_ref, xn_ref):
    @pl.when(pl.program_id(1) == 0)
    def _():
        x = x_ref[...]
        r = lax.rsqrt(jnp.mean(x * x, axis=-1, keepdims=True) + EPS)
        xn_ref[...] = (x * r * g_ref[...]).astype(xn_ref.dtype)

    o_ref[...] = jnp.dot(xn_ref[...], w_ref[...], preferred_element_type=F32).astype(o_ref.dtype)


def _in_proj(x2, g, w, out_dtype, tm, tn):
    M, K = x2.shape
    N = w.shape[1]
    return pl.pallas_call(
        _in_proj_body,
        grid=(M // tm, N // tn),
        in_specs=[pl.BlockSpec((tm, K), lambda i, j: (i, 0)),
                  pl.BlockSpec((1, K), lambda i, j: (0, 0)),
                  pl.BlockSpec((K, tn), lambda i, j: (0, j))],
        out_specs=pl.BlockSpec((tm, tn), lambda i, j: (i, j)),
        out_shape=jax.ShapeDtypeStruct((M, N), out_dtype),
        scratch_shapes=[pltpu.VMEM((tm, K), BF16)],
        compiler_params=_params("parallel", "arbitrary"),
        name="in_proj",
    )(x2, g.reshape(1, K), w)


def _out_proj_body(*refs, n_o):
    x_ref, gate_ref, w_ref = refs[0], refs[1], refs[2]
    o_refs = refs[3:3 + n_o]
    out_ref = refs[3 + n_o]
    o = o_refs[0][...].astype(F32)
    for r in o_refs[1:]:
        o = o + r[...].astype(F32)
    gate = gate_ref[...].astype(F32)
    y = o * (gate / (1.0 + jnp.exp(-gate)))
    out_ref[...] = x_ref[...] + jnp.dot(y.astype(BF16), w_ref[...], preferred_element_type=F32)


def _out_proj(x2, h2, gate_blk, w, os, tm=256):
    M, D = x2.shape
    n_o = len(os)
    row = pl.BlockSpec((tm, D), lambda i: (i, 0))
    return pl.pallas_call(
        functools.partial(_out_proj_body, n_o=n_o),
        grid=(M // tm,),
        in_specs=[row, pl.BlockSpec((tm, D), lambda i: (i, gate_blk)),
                  pl.BlockSpec((D, D), lambda i: (0, 0))] + [row] * n_o,
        out_specs=row,
        out_shape=jax.ShapeDtypeStruct((M, D), F32),
        compiler_params=_params("parallel"),
        name="out_proj",
    )(x2, h2, w, *os)


def _branch_gate(bg_ref, col):
    bg = bg_ref[0]
    g = 1.0 / (1.0 + jnp.exp(-bg))
    return jnp.sum(jnp.where(_iota(bg.shape, 1) == col, g, 0.0), axis=1, keepdims=True)


def _softmax_block(ml_ref, acc_ref, r, s, vt, first):
    if first:
        m_new = jnp.max(s, axis=0, keepdims=True)
        p = jnp.exp2(s - m_new)
        l = jnp.sum(p, axis=0, keepdims=True)
        acc_ref[r] = jnp.dot(vt, p.astype(BF16), preferred_element_type=F32)
    else:
        m = ml_ref[2 * r:2 * r + 1, :]
        m_new = jnp.maximum(m, jnp.max(s, axis=0, keepdims=True))
        alpha = jnp.exp2(m - m_new)
        p = jnp.exp2(s - m_new)
        l = alpha * ml_ref[2 * r + 1:2 * r + 2, :] + jnp.sum(p, axis=0, keepdims=True)
        acc_ref[r] = alpha * acc_ref[r] + jnp.dot(vt, p.astype(BF16), preferred_element_type=F32)
    ml_ref[2 * r:2 * r + 1, :] = m_new
    ml_ref[2 * r + 1:2 * r + 2, :] = l


def _store_heads(o_ref, sums_accs, bg_ref, gate_col0):
    for r, (l, acc) in enumerate(sums_accs):
        o = (acc * (1.0 / l)).T
        if bg_ref is not None:
            o = o * _branch_gate(bg_ref, gate_col0 + r)
        o_ref[0, :, r * HEAD_DIM:(r + 1) * HEAD_DIM] = o.astype(o_ref.dtype)


def _attend_causal(qts, k_ref, vt_ref, ml_ref, acc_ref, qi, t, hpg):
    nh = len(qts)
    w = 2 * t

    def block(j0, diag):
        off = pl.multiple_of(j0 * t, w)
        ss = [jnp.dot(k_ref[0, pl.ds(off, w), (r // hpg) * AUG:(r // hpg + 1) * AUG], qts[r],
                      preferred_element_type=F32) for r in range(nh)]
        if diag:
            causal = _iota((w, w), 0) <= _iota((w, w), 1)
        for r in range(nh):
            s = jnp.where(causal, ss[r], NEG) if diag else ss[r]
            vt = jnp.concatenate([vt_ref[0, r // hpg, j0], vt_ref[0, r // hpg, j0 + 1]], axis=1)
            _softmax_block(ml_ref, acc_ref, r, s, vt, diag)

    block(2 * qi, True)

    def body(j2, carry):
        block(2 * j2, False)
        return carry

    lax.fori_loop(0, qi, body, 0)
    return [(ml_ref[2 * r + 1:2 * r + 2, :], acc_ref[r]) for r in range(nh)]


def _attn_scratch(nh, t):
    return [pltpu.VMEM((2 * nh, 2 * t), F32), pltpu.VMEM((nh, HEAD_DIM, 2 * t), F32)]


def _sel_attn_body(qt_ref, k_ref, vt_ref, bg_ref, o_ref, ml_ref, acc_ref, *, t, nh, gate_base):
    qts = [jnp.concatenate([qt_ref[0, r, 0], qt_ref[0, r, 1]], axis=1) for r in range(nh)]
    sums_accs = _attend_causal(qts, k_ref, vt_ref, ml_ref, acc_ref, pl.program_id(2), t, NSA_HPG)
    _store_heads(o_ref, sums_accs, bg_ref, gate_base + pl.program_id(1) * nh)


def _sel_attn(q_t, k_aug, v_t, bg, gate_base):
    B, H, n_t, _, t = q_t.shape
    S = n_t * t
    nh = NSA_HPG
    assert n_t % 2 == 0
    return pl.pallas_call(
        functools.partial(_sel_attn_body, t=t, nh=nh, gate_base=gate_base),
        grid=(B, H // nh, n_t // 2),
        in_specs=[pl.BlockSpec((1, nh, 2, AUG, t), lambda b, h, i: (b, h, i, 0, 0)),
                  pl.BlockSpec((1, S, AUG), lambda b, h, i: (b, 0, h)),
                  pl.BlockSpec((1, 1, n_t, HEAD_DIM, t), lambda b, h, i: (b, h, 0, 0, 0)),
                  pl.BlockSpec((1, 2 * t, LANES), lambda b, h, i: (b, i, 0))],
        out_specs=pl.BlockSpec((1, 2 * t, nh * HEAD_DIM), lambda b, h, i: (b, i, h)),
        out_shape=jax.ShapeDtypeStruct((B, S, WIDTH), BF16),
        scratch_shapes=_attn_scratch(nh, t),
        compiler_params=_params("parallel", "parallel", "arbitrary"),
        name="sel_attn",
    )(q_t, k_aug, v_t, bg)


def _onehot_block(S, shift):
    return jnp.where((_iota((S, LANES), 0) >> shift) == _iota((S, LANES), 1), 1.0, 0.0).astype(BF16)


def _moba_prep_k_body(k_ref, v_ref, kn_ref, cos_ref, sin_ref, kaug_ref, vt_ref, kmean_ref, *, nb, t):
    kn = _rms_rope(k_ref[0], kn_ref[...], cos_ref[...], sin_ref[...])
    S = kn.shape[0]
    kaug_ref[0, :, :HEAD_DIM] = kn.astype(BF16)
    kaug_ref[0, :, HEAD_DIM:] = _onehot_block(S, _log2(MOBA_BLOCK))
    _store_tiles_t(vt_ref, (0, 0), v_ref[0], t)
    kmean_ref[0, 0] = jnp.mean(kn.reshape(nb, MOBA_BLOCK, HEAD_DIM), axis=1)


def _moba_prep_k(h, k_norm, cos2, sin2):
    B, S, _ = h.shape
    nb = S // MOBA_BLOCK
    t = ATT_TILE
    assert 1 < nb <= HEAD_DIM
    full = pl.BlockSpec((S, HEAD_DIM), lambda b, hh: (0, 0))
    return pl.pallas_call(
        functools.partial(_moba_prep_k_body, nb=nb, t=t),
        grid=(B, N_HEADS),
        in_specs=[pl.BlockSpec((1, S, HEAD_DIM), lambda b, hh: (b, 0, N_HEADS + hh)),
                  pl.BlockSpec((1, S, HEAD_DIM), lambda b, hh: (b, 0, 2 * N_HEADS + hh)),
                  pl.BlockSpec((1, HEAD_DIM), lambda b, hh: (0, 0)), full, full],
        out_specs=[pl.BlockSpec((1, S, AUG), lambda b, hh: (b, 0, hh)),
                   pl.BlockSpec((1, 1, S // t, HEAD_DIM, t), lambda b, hh: (b, hh, 0, 0, 0)),
                   pl.BlockSpec((1, 1, nb, HEAD_DIM), lambda b, hh: (b, hh, 0, 0))],
        out_shape=[jax.ShapeDtypeStruct((B, S, N_HEADS * AUG), BF16),
                   jax.ShapeDtypeStruct((B, N_HEADS, S // t, HEAD_DIM, t), BF16),
                   jax.ShapeDtypeStruct((B, N_HEADS, nb, HEAD_DIM), F32)],
        compiler_params=_params("parallel", "parallel"),
        name="moba_prep_k",
    )(h, h, k_norm.reshape(1, HEAD_DIM), cos2, sin2)


def _moba_attn_body(q_ref, qn_ref, cos_ref, sin_ref, kmean_ref, k_ref, vt_ref, o_ref, ml_ref, acc_ref,
                    *, t, nh, nb, n_sel):
    qi = pl.program_id(2)
    cos2, sin2 = cos_ref[...], sin_ref[...]
    w = 2 * t
    blk = _iota((nb, w), 0)
    blk_f = blk.astype(F32)
    cur = 2 * qi + (_iota((nb, w), 1) >> _log2(t))
    qts = []
    for r in range(nh):
        qn = _rms_rope(q_ref[0, :, r * HEAD_DIM:(r + 1) * HEAD_DIM], qn_ref[...], cos2, sin2)
        qnt = qn.T
        gate = jnp.dot(kmean_ref[0, r], qnt, precision=HIGHEST, preferred_element_type=F32)
        g = jnp.where(blk < cur, gate, NEG)
        picked = jnp.zeros((nb, w), F32)
        for _ in range(n_sel):
            mx = jnp.max(g, axis=0, keepdims=True)
            first = jnp.min(jnp.where(g == mx, blk_f, float(nb)), axis=0, keepdims=True)
            hit = blk_f == first
            picked = jnp.where(hit, 1.0, picked)
            g = jnp.where(hit, LOWEST, g)
        attend = jnp.where(blk < cur, picked, jnp.where(blk == cur, 1.0, 0.0))
        bias = jnp.where(attend > 0.0, 0.0, NEG)
        qts.append(jnp.concatenate([qnt * (ATTN_SCALE * LOG2E), bias, jnp.zeros((HEAD_DIM - nb, w), F32)],
                                   axis=0).astype(BF16))
    _store_heads(o_ref, _attend_causal(qts, k_ref, vt_ref, ml_ref, acc_ref, qi, t, 1), None, None)


def _moba_attn(h, q_norm, cos2, sin2, kmean, k_aug, v_t):
    B, S, _ = h.shape
    t = MOBA_BLOCK
    assert t == ATT_TILE
    nb = S // MOBA_BLOCK
    assert nb % 2 == 0
    nh = HEADS_PER_STEP
    n_sel = min(MOBA_TOPK, nb - 1)
    tab = pl.BlockSpec((2 * t, HEAD_DIM), lambda b, hh, i: (i, 0))
    return pl.pallas_call(
        functools.partial(_moba_attn_body, t=t, nh=nh, nb=nb, n_sel=n_sel),
        grid=(B, N_HEADS // nh, nb // 2),
        in_specs=[pl.BlockSpec((1, 2 * t, nh * HEAD_DIM), lambda b, hh, i: (b, i, hh)),
                  pl.BlockSpec((1, HEAD_DIM), lambda b, hh, i: (0, 0)), tab, tab,
                  pl.BlockSpec((1, nh, nb, HEAD_DIM), lambda b, hh, i: (b, hh, 0, 0)),
                  pl.BlockSpec((1, S, nh * AUG), lambda b, hh, i: (b, 0, hh)),
                  pl.BlockSpec((1, nh, nb, HEAD_DIM, t), lambda b, hh, i: (b, hh, 0, 0, 0))],
        out_specs=pl.BlockSpec((1, 2 * t, nh * HEAD_DIM), lambda b, hh, i: (b, i, hh)),
        out_shape=jax.ShapeDtypeStruct((B, S, WIDTH), BF16),
        scratch_shapes=_attn_scratch(nh, t),
        compiler_params=_params("parallel", "parallel", "arbitrary"),
        name="moba_attn",
    )(h, q_norm.reshape(1, HEAD_DIM), cos2, sin2, kmean, k_aug, v_t)


def _moba_layer(x, norm, w_in, q_norm, k_norm, w_out, cos2, sin2):
    B, S, D = x.shape
    x2 = x.reshape(B * S, D)
    h2 = _in_proj(x2, norm, w_in.astype(BF16), F32, 1024, 1024)
    h = h2.reshape(B, S, -1)
    k_aug, v_t, kmean = _moba_prep_k(h, k_norm, cos2, sin2)
    o = _moba_attn(h, q_norm, cos2, sin2, kmean, k_aug, v_t)
    out = _out_proj(x2, h2, 3, w_out.astype(BF16), [o.reshape(B * S, WIDTH)])
    return out.reshape(B, S, D)


def _split2_dot(a, u):
    hi = a.astype(BF16)
    lo = (a - hi.astype(F32)).astype(BF16)
    return jnp.dot(hi, u, preferred_element_type=F32) + jnp.dot(lo, u, preferred_element_type=F32)


def _sb_attn_body(q_ref, k_ref, v_ref, o_ref, kt_ref, carry_ref, acc_ref, *, t, nh):
    qi = pl.program_id(2)
    w = 2 * t

    @pl.when(qi == 0)
    def _():
        for r in range(nh):
            _store_tiles_t(kt_ref, (r,), k_ref[0, :, r * HEAD_DIM:(r + 1) * HEAD_DIM].astype(F32), t)

    after = jnp.where(_iota((t, t), 0) > _iota((t, t), 1), 1.0, 0.0).astype(BF16)
    qs = [(q_ref[0, :, r * HEAD_DIM:(r + 1) * HEAD_DIM].astype(F32) * ATTN_SCALE).astype(BF16)
          for r in range(nh)]

    def block(j0, diag):
        zs = [jnp.dot(qs[r], jnp.concatenate([kt_ref[r, j0], kt_ref[r, j0 + 1]], axis=1),
                      preferred_element_type=F32) for r in range(nh)]
        if diag:
            past = _iota((w, w), 1) < _iota((w, w), 0)
        staged = []
        for r in range(nh):
            z = zs[r]
            drop = jnp.maximum(z, 0.0) + jnp.log(1.0 + jnp.exp2(jnp.abs(z) * (-LOG2E)))
            if diag:
                drop = jnp.where(past, drop, 0.0)
            tiles = []
            for i in range(2):
                d = drop[:, i * t:(i + 1) * t]
                suffix = _split2_dot(d, after)
                tiles.append((suffix, suffix[:, 0:1] + d[:, 0:1]))
            staged.append((drop, tiles))
        for r in range(nh):
            drop, ((suf_old, sum_old), (suf_new, sum_new)) = staged[r]
            vs = v_ref[0, pl.ds(pl.multiple_of(j0 * t, w), w), r * HEAD_DIM:(r + 1) * HEAD_DIM]
            if diag:
                behind = jnp.concatenate([suf_old + sum_new, suf_new], axis=1)
                carry = sum_new + sum_old
            else:
                carry = carry_ref[r]
                carry_old = carry + sum_new
                behind = jnp.concatenate([suf_old + carry_old, suf_new + carry], axis=1)
                carry = carry_old + sum_old
            a = jnp.exp(zs[r] - (drop + behind))
            if diag:
                a = jnp.where(past, a, 0.0)
            o = jnp.dot(a.astype(BF16), vs, preferred_element_type=F32)
            acc_ref[r] = o if diag else acc_ref[r] + o
            carry_ref[r] = carry
            least = jnp.min(carry) if r == 0 else jnp.minimum(least, jnp.min(carry))
        return least

    def body(c):
        return c[0] + 1, block(2 * (qi - 1 - c[0]), False)

    lax.while_loop(lambda c: (c[0] < qi) & (c[1] < SB_CUTOFF), body, (jnp.int32(0), block(2 * qi, True)))
    for r in range(nh):
        o_ref[0, :, r * HEAD_DIM:(r + 1) * HEAD_DIM] = acc_ref[r].astype(o_ref.dtype)


def _sb_attn(h):
    B, S, _ = h.shape
    t = ATT_TILE
    assert S % (2 * t) == 0
    nh = HEADS_PER_STEP
    nhb = N_HEADS // nh
    w = nh * HEAD_DIM
    return pl.pallas_call(
        functools.partial(_sb_attn_body, t=t, nh=nh),
        grid=(B, nhb, S // (2 * t)),
        in_specs=[pl.BlockSpec((1, 2 * t, w), lambda b, hh, i: (b, i, hh)),
                  pl.BlockSpec((1, S, w), lambda b, hh, i: (b, 0, nhb + hh)),
                  pl.BlockSpec((1, S, w), lambda b, hh, i: (b, 0, 2 * nhb + hh))],
        out_specs=pl.BlockSpec((1, 2 * t, w), lambda b, hh, i: (b, i, hh)),
        out_shape=jax.ShapeDtypeStruct((B, S, WIDTH), BF16),
        scratch_shapes=[pltpu.VMEM((nh, S // t, HEAD_DIM, t), BF16), pltpu.VMEM((nh, 2 * t, 1), F32),
                        pltpu.VMEM((nh, 2 * t, HEAD_DIM), F32)],
        compiler_params=_params("parallel", "parallel", "arbitrary"),
        name="sb_attn",
    )(h, h, h)


def _sb_layer(x, norm, w_in, w_out):
    B, S, D = x.shape
    x2 = x.reshape(B * S, D)
    h2 = _in_proj(x2, norm, w_in.astype(BF16), BF16, 1024, 1024)
    o = _sb_attn(h2.reshape(B, S, -1))
    out = _out_proj(x2, h2, 3, w_out.astype(BF16), [o.reshape(B * S, WIDTH)])
    return out.reshape(B, S, D)


NSA_KV_BLK0 = 2 * N_HEADS


def _nsa_compress_body(kc_ref, vc_ref, wk_ref, wv_ref, pos_ref, kcn_ref, cosc_ref, sinc_ref,
                       kcc_ref, vcct_ref, *, n_rows):
    half = CMP_BLOCK // CMP_STRIDE
    assert half == 2

    def compress(src_ref, w_ref):
        first = jnp.zeros((n_rows, HEAD_DIM), F32)
        second = jnp.zeros((n_rows, HEAD_DIM), F32)
        for l in range(CMP_STRIDE):
            rows = src_ref[0, pl.ds(l, n_rows, stride=CMP_STRIDE), :]
            first = first + jnp.dot(rows + pos_ref[l:l + 1, :], w_ref[l],
                                    precision=HIGHEST, preferred_element_type=F32)
            second = second + jnp.dot(rows + pos_ref[CMP_STRIDE + l:CMP_STRIDE + l + 1, :],
                                      w_ref[CMP_STRIDE + l], precision=HIGHEST, preferred_element_type=F32)
        return first + pltpu.roll(second, n_rows - 1, 0)

    kc = compress(kc_ref, wk_ref)
    kcc_ref[0, 0] = _rms_rope(kc, kcn_ref[...], cosc_ref[...], sinc_ref[...])
    vcct_ref[0, 0] = compress(vc_ref, wv_ref).T.astype(vcct_ref.dtype)


def _nsa_compress(h, cmp_wk, cmp_wv, cmp_pos, kc_norm, cos_c, sin_c):
    B, S, _ = h.shape
    n_rows = S // CMP_STRIDE
    wspec = pl.BlockSpec((CMP_BLOCK, HEAD_DIM, HEAD_DIM), lambda b, g: (0, 0, 0))
    tab = pl.BlockSpec((n_rows, HEAD_DIM), lambda b, g: (0, 0))
    out = pl.BlockSpec((1, 1, n_rows, HEAD_DIM), lambda b, g: (b, g, 0, 0))
    return pl.pallas_call(
        functools.partial(_nsa_compress_body, n_rows=n_rows),
        grid=(B, NSA_KV_HEADS),
        in_specs=[pl.BlockSpec((1, S, HEAD_DIM), lambda b, g: (b, 0, NSA_KV_BLK0 + g)),
                  pl.BlockSpec((1, S, HEAD_DIM), lambda b, g: (b, 0, NSA_KV_BLK0 + NSA_KV_HEADS + g)),
                  wspec, wspec,
                  pl.BlockSpec((CMP_BLOCK, HEAD_DIM), lambda b, g: (0, 0)),
                  pl.BlockSpec((1, HEAD_DIM), lambda b, g: (0, 0)), tab, tab],
        out_specs=[out, pl.BlockSpec((1, 1, HEAD_DIM, n_rows), lambda b, g: (b, g, 0, 0))],
        out_shape=[jax.ShapeDtypeStruct((B, NSA_KV_HEADS, n_rows, HEAD_DIM), F32),
                   jax.ShapeDtypeStruct((B, NSA_KV_HEADS, HEAD_DIM, n_rows), BF16)],
        compiler_params=_params("parallel", "parallel"),
        name="nsa_compress",
    )(h, h, cmp_wk, cmp_wv, cmp_pos, kc_norm.reshape(1, HEAD_DIM), cos_c, sin_c)


def _nsa_prep_kv_body(ks_ref, vs_ref, kw_ref, vw_ref, ksn_ref, kwn_ref, cos_ref, sin_ref,
                      ksaug_ref, vst_ref, kwb_ref, vwt_ref, *, t):
    cos2, sin2 = cos_ref[...], sin_ref[...]
    S = cos2.shape[0]
    ksaug_ref[0, :, :HEAD_DIM] = _rms_rope(ks_ref[0], ksn_ref[...], cos2, sin2).astype(BF16)
    ksaug_ref[0, :, HEAD_DIM:] = _onehot_block(S, _log2(SLC_BLOCK))
    kwb_ref[0] = _rms_rope(kw_ref[0], kwn_ref[...], cos2, sin2).astype(BF16)
    _store_tiles_t(vst_ref, (0, 0), vs_ref[0], t)
    _store_tiles_t(vwt_ref, (0, 0), vw_ref[0], t)


def _nsa_prep_kv(h, ks_norm, kw_norm, cos2, sin2):
    B, S, _ = h.shape
    G = NSA_KV_HEADS
    t = ATT_TILE

    def src(i):
        return pl.BlockSpec((1, S, HEAD_DIM), lambda b, g: (b, 0, NSA_KV_BLK0 + i * G + g))

    vec = pl.BlockSpec((1, HEAD_DIM), lambda b, g: (0, 0))
    full = pl.BlockSpec((S, HEAD_DIM), lambda b, g: (0, 0))
    vt = pl.BlockSpec((1, 1, S // t, HEAD_DIM, t), lambda b, g: (b, g, 0, 0, 0))
    vt_shape = jax.ShapeDtypeStruct((B, G, S // t, HEAD_DIM, t), BF16)
    return pl.pallas_call(
        functools.partial(_nsa_prep_kv_body, t=t),
        grid=(B, G),
        in_specs=[src(2), src(3), src(4), src(5), vec, vec, full, full],
        out_specs=[pl.BlockSpec((1, S, AUG), lambda b, g: (b, 0, g)), vt,
                   pl.BlockSpec((1, S, HEAD_DIM), lambda b, g: (b, 0, g)), vt],
        out_shape=[jax.ShapeDtypeStruct((B, S, G * AUG), BF16), vt_shape,
                   jax.ShapeDtypeStruct((B, S, KV_WIDTH), BF16), vt_shape],
        compiler_params=_params("parallel", "parallel"),
        name="nsa_prep_kv",
    )(h, h, h, h, ks_norm.reshape(1, HEAD_DIM), kw_norm.reshape(1, HEAD_DIM), cos2, sin2)


def _nsa_cmp_sel_body(q_ref, qn_ref, cos_ref, sin_ref, kcc_ref, vcct_ref, bg_ref, ocmp_ref, qt_ref,
                      *, t, n_cmp, n_slc, n_sel):
    g_idx = pl.program_id(1)
    q0 = pl.program_id(2) * t
    cos2, sin2 = cos_ref[...], sin_ref[...]
    kcc = kcc_ref[0, 0]
    kcc_hi = kcc.astype(BF16)
    kcc_lo = (kcc - kcc_hi.astype(F32)).astype(BF16)
    kcc3 = jnp.concatenate([kcc_hi, kcc_hi, kcc_lo], axis=1)
    vcct = vcct_ref[0, 0]
    cmp_ok = _iota((n_cmp, t), 0) * CMP_STRIDE + (CMP_BLOCK - 1) <= q0 + _iota((n_cmp, t), 1)

    p_sum = jnp.zeros((n_cmp, t), F32)
    for r in range(NSA_HPG):
        qnt = _rms_rope(q_ref[0, :, r * HEAD_DIM:(r + 1) * HEAD_DIM], qn_ref[...], cos2, sin2).T
        qt_ref[0, r, 0, :HEAD_DIM, :] = (qnt * (ATTN_SCALE * LOG2E)).astype(BF16)
        q_hi = qnt.astype(BF16)
        q_lo = (qnt - q_hi.astype(F32)).astype(BF16)
        logits = jnp.dot(kcc3, jnp.concatenate([q_hi, q_lo, q_hi], axis=0),
                         preferred_element_type=F32) * ATTN_SCALE
        logits = jnp.where(cmp_ok, logits, NEG)
        mx = jnp.max(logits, axis=0, keepdims=True)
        e = jnp.where(cmp_ok, jnp.exp(logits - mx), 0.0)
        p = e * (1.0 / jnp.maximum(jnp.sum(e, axis=0, keepdims=True), TINY))
        p_sum = p_sum + p
        o = jnp.dot(vcct, p.astype(BF16), preferred_element_type=F32).T
        o = o * _branch_gate(bg_ref, g_idx * NSA_HPG + r)
        ocmp_ref[0, :, r * HEAD_DIM:(r + 1) * HEAD_DIM] = o.astype(ocmp_ref.dtype)

    c_start = _iota((n_slc, n_cmp), 1) * CMP_STRIDE
    s_start = _iota((n_slc, n_cmp), 0) * SLC_BLOCK
    overlap = jnp.where((c_start < s_start + SLC_BLOCK) & (c_start + CMP_BLOCK > s_start), 1.0, 0.0).astype(BF16)
    ps_hi = p_sum.astype(BF16)
    rest = p_sum - ps_hi.astype(F32)
    ps_mid = rest.astype(BF16)
    ps_lo = (rest - ps_mid.astype(F32)).astype(BF16)
    imp = jnp.dot(jnp.concatenate([overlap, overlap, overlap], axis=1),
                  jnp.concatenate([ps_hi, ps_mid, ps_lo], axis=0), preferred_element_type=F32)
    blk = _iota((n_slc, t), 0)
    cur = (q0 + _iota((n_slc, t), 1)) >> _log2(SLC_BLOCK)
    imp = jnp.where(blk == 0, FORCE, imp)
    imp = jnp.where(blk == cur, FORCE, imp)
    imp = jnp.where(blk == cur - 1, FORCE, imp)
    imp = jnp.where(blk <= cur, imp, NEG)
    sub = 8
    ranks = []
    for g0 in range(0, n_slc, sub):
        mine = imp[g0:g0 + sub, :]
        row = _iota((sub, t), 0) + g0
        rank = jnp.zeros((sub, t), F32)
        for k in range(n_slc):
            c = imp[k:k + 1, :]
            ge = jnp.where(c >= mine, 1.0, 0.0)
            gt = jnp.where(c > mine, 1.0, 0.0)
            rank = rank + (ge if k < g0 else gt if k >= g0 + sub else jnp.where(row > k, ge, gt))
        ranks.append(rank)
    bias = jnp.where(jnp.concatenate(ranks, axis=0) < float(n_sel), 0.0, NEG)
    bias = jnp.concatenate([bias, jnp.zeros((HEAD_DIM - n_slc, t), F32)], axis=0).astype(BF16)
    for r in range(NSA_HPG):
        qt_ref[0, r, 0, HEAD_DIM:, :] = bias


def _nsa_cmp_sel(h, bg, q_norm, cos2, sin2, kc_c, vc_ct):
    B, S, _ = h.shape
    t = ATT_TILE
    G, R = NSA_KV_HEADS, NSA_HPG
    n_cmp = S // CMP_STRIDE
    n_slc = S // SLC_BLOCK
    assert n_slc <= HEAD_DIM and n_slc % 8 == 0
    tab = pl.BlockSpec((t, HEAD_DIM), lambda b, g, i: (i, 0))
    return pl.pallas_call(
        functools.partial(_nsa_cmp_sel_body, t=t, n_cmp=n_cmp, n_slc=n_slc, n_sel=min(SLC_TOPK, n_slc)),
        grid=(B, G, S // t),
        in_specs=[pl.BlockSpec((1, t, R * HEAD_DIM), lambda b, g, i: (b, i, g)),
                  pl.BlockSpec((1, HEAD_DIM), lambda b, g, i: (0, 0)), tab, tab,
                  pl.BlockSpec((1, 1, n_cmp, HEAD_DIM), lambda b, g, i: (b, g, 0, 0)),
                  pl.BlockSpec((1, 1, HEAD_DIM, n_cmp), lambda b, g, i: (b, g, 0, 0)),
                  pl.BlockSpec((1, t, LANES), lambda b, g, i: (b, i, 0))],
        out_specs=[pl.BlockSpec((1, t, R * HEAD_DIM), lambda b, g, i: (b, i, g)),
                   pl.BlockSpec((1, R, 1, AUG, t), lambda b, g, i: (b, g, i, 0, 0))],
        out_shape=[jax.ShapeDtypeStruct((B, S, WIDTH), BF16),
                   jax.ShapeDtypeStruct((B, N_HEADS, S // t, AUG, t), BF16)],
        compiler_params=_params("parallel", "parallel", "arbitrary"),
        name="nsa_cmp_sel",
    )(h, q_norm.reshape(1, HEAD_DIM), cos2, sin2, kc_c, vc_ct, bg)


def _win_attn_body(qt_ref, k_ref, vt_ref, bg_ref, o_ref, *, t, nh, n_back, gate_base):
    qi = pl.program_id(2)
    n_blk = n_back + 1
    j0 = jnp.maximum(qi - n_back, 0)
    k = k_ref[0, pl.ds(pl.multiple_of(j0 * t, t), n_blk * t), :]
    vt = jnp.concatenate([vt_ref[0, 0, j0 + d] for d in range(n_blk)], axis=1)
    dist = _iota((n_blk * t, t), 1) - _iota((n_blk * t, t), 0) + (qi - j0) * t
    ss = [jnp.dot(k, qt_ref[0, r, 0], preferred_element_type=F32) for r in range(nh)]
    states = []
    for r in range(nh):
        s = jnp.where(dist >= 0, ss[r], NEG)
        s = jnp.where(dist < WINDOW, s, NEG)
        m = jnp.max(s, axis=0, keepdims=True)
        p = jnp.exp2(s - m)
        states.append((jnp.sum(p, axis=0, keepdims=True), jnp.dot(vt, p.astype(BF16), preferred_element_type=F32)))
    _store_heads(o_ref, states, bg_ref, gate_base + pl.program_id(1) * nh)


def _win_attn(q_t, kw, vw_t, bg, gate_base):
    B, H, n_t, _, t = q_t.shape
    S = n_t * t
    nh = NSA_HPG
    assert WINDOW % t == 0 and n_t > WINDOW // t
    return pl.pallas_call(
        functools.partial(_win_attn_body, t=t, nh=nh, n_back=WINDOW // t, gate_base=gate_base),
        grid=(B, H // nh, n_t),
        in_specs=[pl.BlockSpec((1, nh, 1, HEAD_DIM, t), lambda b, g, i: (b, g, i, 0, 0)),
                  pl.BlockSpec((1, S, HEAD_DIM), lambda b, g, i: (b, 0, g)),
                  pl.BlockSpec((1, 1, n_t, HEAD_DIM, t), lambda b, g, i: (b, g, 0, 0, 0)),
                  pl.BlockSpec((1, t, LANES), lambda b, g, i: (b, i, 0))],
        out_specs=pl.BlockSpec((1, t, nh * HEAD_DIM), lambda b, g, i: (b, i, g)),
        out_shape=jax.ShapeDtypeStruct((B, S, WIDTH), BF16),
        compiler_params=_params("parallel", "parallel", "arbitrary"),
        name="win_attn",
    )(q_t, kw, vw_t, bg)


def _nsa_layer(x, norm, w_in, q_norm, kc_norm, ks_norm, kw_norm, cmp_wk, cmp_wv, cmp_pos, w_out, cos2, sin2):
    B, S, D = x.shape
    x2 = x.reshape(B * S, D)
    kv_end = WIDTH + 6 * KV_WIDTH
    gate_end = kv_end + WIDTH
    w_main = jnp.concatenate([w_in[:, :WIDTH], w_in[:, kv_end:gate_end], w_in[:, WIDTH:kv_end]], axis=1)
    w_bg = jnp.pad(w_in[:, gate_end:], ((0, 0), (0, LANES - 3 * N_HEADS)))
    h2 = _in_proj(x2, norm, w_main.astype(BF16), F32, 1024, 1024)
    bg = _in_proj(x2, norm, w_bg.astype(BF16), F32, 512, LANES).reshape(B, S, LANES)
    h = h2.reshape(B, S, -1)
    cos_c, sin_c = _rope_tables(jnp.arange(S // CMP_STRIDE) * CMP_STRIDE + (CMP_BLOCK - 1))
    kc_c, vc_ct = _nsa_compress(h, cmp_wk, cmp_wv, cmp_pos, kc_norm, cos_c, sin_c)
    ks_aug, vs_t, kw_b, vw_t = _nsa_prep_kv(h, ks_norm, kw_norm, cos2, sin2)
    o_cmp, q_t = _nsa_cmp_sel(h, bg, q_norm, cos2, sin2, kc_c, vc_ct)
    o_slc = _sel_attn(q_t, ks_aug, vs_t, bg, N_HEADS)
    o_win = _win_attn(q_t, kw_b, vw_t, bg, 2 * N_HEADS)
    os = [o.reshape(B * S, WIDTH) for o in (o_cmp, o_slc, o_win)]
    out = _out_proj(x2, h2, 1, w_out.astype(BF16), os)
    return out.reshape(B, S, D)


def kernel(x, l0_norm, l0_w_in, l0_q_norm, l0_k_norm, l0_w_out, l1_norm, l1_w_in, l1_w_out, l2_norm, l2_w_in, l2_q_norm, l2_kc_norm, l2_ks_norm, l2_kw_norm, l2_cmp_wk, l2_cmp_wv, l2_cmp_pos, l2_w_out, l3_norm, l3_w_in, l3_q_norm, l3_k_norm, l3_w_out):
    cos2, sin2 = _rope_tables(jnp.arange(x.shape[1]))
    x = _moba_layer(x, l0_norm, l0_w_in, l0_q_norm, l0_k_norm, l0_w_out, cos2, sin2)
    x = _sb_layer(x, l1_norm, l1_w_in, l1_w_out)
    x = _nsa_layer(x, l2_norm, l2_w_in, l2_q_norm, l2_kc_norm, l2_ks_norm, l2_kw_norm,
                   l2_cmp_wk, l2_cmp_wv, l2_cmp_pos, l2_w_out, cos2, sin2)
    x = _moba_layer(x, l3_norm, l3_w_in, l3_q_norm, l3_k_norm, l3_w_out, cos2, sin2)
    return x
```

```python
import functools
import math

import jax
import jax.numpy as jnp
from jax import lax
from jax.experimental import pallas as pl
from jax.experimental.pallas import tpu as pltpu

F32 = jnp.float32
BF16 = jnp.bfloat16
HIGHEST = lax.Precision.HIGHEST

D_MODEL = 2048
HEAD_DIM = 128
N_HEADS = 16
WIDTH = N_HEADS * HEAD_DIM
ROPE_THETA = 10000.0
EPS = 1e-6
ATTN_SCALE = HEAD_DIM ** -0.5
LOG2E = math.log2(math.e)
NEG = -1e30
LOWEST = -3e38
FORCE = 1e30
TINY = 1e-30
MOBA_BLOCK = 256
MOBA_TOPK = 3
NSA_KV_HEADS = 4
NSA_HPG = N_HEADS // NSA_KV_HEADS
KV_WIDTH = NSA_KV_HEADS * HEAD_DIM
CMP_BLOCK = 32
CMP_STRIDE = 16
SLC_BLOCK = 64
SLC_TOPK = 16
WINDOW = 512

LANES = 128
AUG = 2 * HEAD_DIM
ATT_TILE = 256
HEADS_PER_STEP = 4
SB_CUTOFF = 110.0
VMEM_LIMIT = 48 * 1024 * 1024

_NT = (((1,), (1,)), ((), ()))


def _params(*sem):
    return pltpu.CompilerParams(dimension_semantics=sem, vmem_limit_bytes=VMEM_LIMIT)


def _rope_tables(pos):
    half = HEAD_DIM // 2
    inv_freq = jnp.exp(-math.log(ROPE_THETA) * jnp.arange(half, dtype=F32) / half)
    ang = pos.astype(F32)[:, None] * inv_freq[None, :]
    cos, sin = jnp.cos(ang), jnp.sin(ang)
    return jnp.concatenate([cos, cos], axis=-1), jnp.concatenate([-sin, sin], axis=-1)


def _rms_rope(x, g, cos2, sin2):
    y = x * lax.rsqrt(jnp.mean(x * x, axis=-1, keepdims=True) + EPS) * g
    return y * cos2 + pltpu.roll(y, HEAD_DIM // 2, 1) * sin2


def _iota(shape, dim):
    return lax.broadcasted_iota(jnp.int32, shape, dim)


def _log2(n):
    s = n.bit_length() - 1
    assert 1 << s == n
    return s


def _store_tiles_t(dst_ref, lead, x, t):
    for j in range(x.shape[0] // t):
        dst_ref[lead + (j,)] = x[j * t:(j + 1) * t, :].T.astype(dst_ref.dtype)


def _in_proj_body(x_ref, g_ref, w_ref, o_ref, xn_ref):
    @pl.when(pl.program_id(1) == 0)
    def _():
        x = x_ref[...]
        r = lax.rsqrt(jnp.mean(x * x, axis=-1, keepdims=True) + EPS)
        xn_ref[...] = (x * r * g_ref[...]).astype(xn_ref.dtype)

    o_ref[...] = jnp.dot(xn_ref[...], w_ref[...], preferred_element_type=F32).astype(o_ref.dtype)


def _in_proj(x2, g, w, out_dtype, tm, tn):
    M, K = x2.shape
    N = w.shape[1]
    return pl.pallas_call(
        _in_proj_body,
        grid=(M // tm, N // tn),
        in_specs=[pl.BlockSpec((tm, K), lambda i, j: (i, 0)),
                  pl.BlockSpec((1, K), lambda i, j: (0, 0)),
                  pl.BlockSpec((K, tn), lambda i, j: (0, j))],
        out_specs=pl.BlockSpec((tm, tn), lambda i, j: (i, j)),
        out_shape=jax.ShapeDtypeStruct((M, N), out_dtype),
        scratch_shapes=[pltpu.VMEM((tm, K), BF16)],
        compiler_params=_params("parallel", "arbitrary"),
        name="in_proj",
    )(x2, g.reshape(1, K), w)


def _out_proj_body(*refs, n_o):
    x_ref, gate_ref, w_ref = refs[0], refs[1], refs[2]
    o_refs = refs[3:3 + n_o]
    out_ref = refs[3 + n_o]
    o = o_refs[0][...].astype(F32)
    for r in o_refs[1:]:
        o = o + r[...].astype(F32)
    gate = gate_ref[...].astype(F32)
    y = o * (gate / (1.0 + jnp.exp(-gate)))
    out_ref[...] = x_ref[...] + jnp.dot(y.astype(BF16), w_ref[...], preferred_element_type=F32)


def _out_proj(x2, h2, gate_blk, w, os, tm=256):
    M, D = x2.shape
    n_o = len(os)
    row = pl.BlockSpec((tm, D), lambda i: (i, 0))
    return pl.pallas_call(
        functools.partial(_out_proj_body, n_o=n_o),
        grid=(M // tm,),
        in_specs=[row, pl.BlockSpec((tm, D), lambda i: (i, gate_blk)),
                  pl.BlockSpec((D, D), lambda i: (0, 0))] + [row] * n_o,
        out_specs=row,
        out_shape=jax.ShapeDtypeStruct((M, D), F32),
        compiler_params=_params("parallel"),
        name="out_proj",
    )(x2, h2, w, *os)


def _branch_gate(bg_ref, col):
    bg = bg_ref[0]
    g = 1.0 / (1.0 + jnp.exp(-bg))
    return jnp.sum(jnp.where(_iota(bg.shape, 1) == col, g, 0.0), axis=1, keepdims=True)


def _softmax_block(ml_ref, acc_ref, r, s, vt, first):
    if first:
        m_new = jnp.max(s, axis=0, keepdims=True)
        p = jnp.exp2(s - m_new)
        l = jnp.sum(p, axis=0, keepdims=True)
        acc_ref[r] = jnp.dot(vt, p.astype(BF16), preferred_element_type=F32)
    else:
        m = ml_ref[2 * r:2 * r + 1, :]
        m_new = jnp.maximum(m, jnp.max(s, axis=0, keepdims=True))
        alpha = jnp.exp2(m - m_new)
        p = jnp.exp2(s - m_new)
        l = alpha * ml_ref[2 * r + 1:2 * r + 2, :] + jnp.sum(p, axis=0, keepdims=True)
        acc_ref[r] = alpha * acc_ref[r] + jnp.dot(vt, p.astype(BF16), preferred_element_type=F32)
    ml_ref[2 * r:2 * r + 1, :] = m_new
    ml_ref[2 * r + 1:2 * r + 2, :] = l


def _store_heads(o_ref, sums_accs, bg_ref, gate_col0):
    for r, (l, acc) in enumerate(sums_accs):
        o = (acc * (1.0 / l)).T
        if bg_ref is not None:
            o = o * _branch_gate(bg_ref, gate_col0 + r)
        o_ref[0, :, r * HEAD_DIM:(r + 1) * HEAD_DIM] = o.astype(o_ref.dtype)


def _attend_causal(qts, k_ref, vt_ref, ml_ref, acc_ref, qi, t, hpg):
    nh = len(qts)
    w = 2 * t

    def block(j0, diag):
        off = pl.multiple_of(j0 * t, w)
        ss = [jnp.dot(k_ref[0, pl.ds(off, w), (r // hpg) * AUG:(r // hpg + 1) * AUG], qts[r],
                      preferred_element_type=F32) for r in range(nh)]
        if diag:
            causal = _iota((w, w), 0) <= _iota((w, w), 1)
        for r in range(nh):
            s = jnp.where(causal, ss[r], NEG) if diag else ss[r]
            vt = jnp.concatenate([vt_ref[0, r // hpg, j0], vt_ref[0, r // hpg, j0 + 1]], axis=1)
            _softmax_block(ml_ref, acc_ref, r, s, vt, diag)

    block(2 * qi, True)

    def body(j2, carry):
        block(2 * j2, False)
        return carry

    lax.fori_loop(0, qi, body, 0)
    return [(ml_ref[2 * r + 1:2 * r + 2, :], acc_ref[r]) for r in range(nh)]


def _attn_scratch(nh, t):
    return [pltpu.VMEM((2 * nh, 2 * t), F32), pltpu.VMEM((nh, HEAD_DIM, 2 * t), F32)]


def _sel_attn_body(qt_ref, k_ref, vt_ref, bg_ref, o_ref, ml_ref, acc_ref, *, t, nh, gate_base):
    qts = [jnp.concatenate([qt_ref[0, r, 0], qt_ref[0, r, 1]], axis=1) for r in range(nh)]
    sums_accs = _attend_causal(qts, k_ref, vt_ref, ml_ref, acc_ref, pl.program_id(2), t, NSA_HPG)
    _store_heads(o_ref, sums_accs, bg_ref, gate_base + pl.program_id(1) * nh)


def _sel_attn(q_t, k_aug, v_t, bg, gate_base):
    B, H, n_t, _, t = q_t.shape
    S = n_t * t
    nh = NSA_HPG
    assert n_t % 2 == 0
    return pl.pallas_call(
        functools.partial(_sel_attn_body, t=t, nh=nh, gate_base=gate_base),
        grid=(B, H // nh, n_t // 2),
        in_specs=[pl.BlockSpec((1, nh, 2, AUG, t), lambda b, h, i: (b, h, i, 0, 0)),
                  pl.BlockSpec((1, S, AUG), lambda b, h, i: (b, 0, h)),
                  pl.BlockSpec((1, 1, n_t, HEAD_DIM, t), lambda b, h, i: (b, h, 0, 0, 0)),
                  pl.BlockSpec((1, 2 * t, LANES), lambda b, h, i: (b, i, 0))],
        out_specs=pl.BlockSpec((1, 2 * t, nh * HEAD_DIM), lambda b, h, i: (b, i, h)),
        out_shape=jax.ShapeDtypeStruct((B, S, WIDTH), BF16),
        scratch_shapes=_attn_scratch(nh, t),
        compiler_params=_params("parallel", "parallel", "arbitrary"),
        name="sel_attn",
    )(q_t, k_aug, v_t, bg)


def _onehot_block(S, shift):
    return jnp.where((_iota((S, LANES), 0) >> shift) == _iota((S, LANES), 1), 1.0, 0.0).astype(BF16)


def _moba_prep_k_body(k_ref, v_ref, kn_ref, cos_ref, sin_ref, kaug_ref, vt_ref, kmean_ref, *, nb, t):
    kn = _rms_rope(k_ref[0], kn_ref[...], cos_ref[...], sin_ref[...])
    S = kn.shape[0]
    kaug_ref[0, :, :HEAD_DIM] = kn.astype(BF16)
    kaug_ref[0, :, HEAD_DIM:] = _onehot_block(S, _log2(MOBA_BLOCK))
    _store_tiles_t(vt_ref, (0, 0), v_ref[0], t)
    kmean_ref[0, 0] = jnp.mean(kn.reshape(nb, MOBA_BLOCK, HEAD_DIM), axis=1)


def _moba_prep_k(h, k_norm, cos2, sin2):
    B, S, _ = h.shape
    nb = S // MOBA_BLOCK
    t = ATT_TILE
    assert 1 < nb <= HEAD_DIM
    full = pl.BlockSpec((S, HEAD_DIM), lambda b, hh: (0, 0))
    return pl.pallas_call(
        functools.partial(_moba_prep_k_body, nb=nb, t=t),
        grid=(B, N_HEADS),
        in_specs=[pl.BlockSpec((1, S, HEAD_DIM), lambda b, hh: (b, 0, N_HEADS + hh)),
                  pl.BlockSpec((1, S, HEAD_DIM), lambda b, hh: (b, 0, 2 * N_HEADS + hh)),
                  pl.BlockSpec((1, HEAD_DIM), lambda b, hh: (0, 0)), full, full],
        out_specs=[pl.BlockSpec((1, S, AUG), lambda b, hh: (b, 0, hh)),
                   pl.BlockSpec((1, 1, S // t, HEAD_DIM, t), lambda b, hh: (b, hh, 0, 0, 0)),
                   pl.BlockSpec((1, 1, nb, HEAD_DIM), lambda b, hh: (b, hh, 0, 0))],
        out_shape=[jax.ShapeDtypeStruct((B, S, N_HEADS * AUG), BF16),
                   jax.ShapeDtypeStruct((B, N_HEADS, S // t, HEAD_DIM, t), BF16),
                   jax.ShapeDtypeStruct((B, N_HEADS, nb, HEAD_DIM), F32)],
        compiler_params=_params("parallel", "parallel"),
        name="moba_prep_k",
    )(h, h, k_norm.reshape(1, HEAD_DIM), cos2, sin2)


def _moba_attn_body(q_ref, qn_ref, cos_ref, sin_ref, kmean_ref, k_ref, vt_ref, o_ref, ml_ref, acc_ref,
                    *, t, nh, nb, n_sel):
    qi = pl.program_id(2)
    cos2, sin2 = cos_ref[...], sin_ref[...]
    w = 2 * t
    blk = _iota((nb, w), 0)
    blk_f = blk.astype(F32)
    cur = 2 * qi + (_iota((nb, w), 1) >> _log2(t))
    qts = []
    for r in range(nh):
        qn = _rms_rope(q_ref[0, :, r * HEAD_DIM:(r + 1) * HEAD_DIM], qn_ref[...], cos2, sin2)
        qnt = qn.T
        gate = jnp.dot(kmean_ref[0, r], qnt, precision=HIGHEST, preferred_element_type=F32)
        g = jnp.where(blk < cur, gate, NEG)
        picked = jnp.zeros((nb, w), F32)
        for _ in range(n_sel):
            mx = jnp.max(g, axis=0, keepdims=True)
            first = jnp.min(jnp.where(g == mx, blk_f, float(nb)), axis=0, keepdims=True)
            hit = blk_f == first
            picked = jnp.where(hit, 1.0, picked)
            g = jnp.where(hit, LOWEST, g)
        attend = jnp.where(blk < cur, picked, jnp.where(blk == cur, 1.0, 0.0))
        bias = jnp.where(attend > 0.0, 0.0, NEG)
        qts.append(jnp.concatenate([qnt * (ATTN_SCALE * LOG2E), bias, jnp.zeros((HEAD_DIM - nb, w), F32)],
                                   axis=0).astype(BF16))
    _store_heads(o_ref, _attend_causal(qts, k_ref, vt_ref, ml_ref, acc_ref, qi, t, 1), None, None)


def _moba_attn(h, q_norm, cos2, sin2, kmean, k_aug, v_t):
    B, S, _ = h.shape
    t = MOBA_BLOCK
    assert t == ATT_TILE
    nb = S // MOBA_BLOCK
    assert nb % 2 == 0
    nh = HEADS_PER_STEP
    n_sel = min(MOBA_TOPK, nb - 1)
    tab = pl.BlockSpec((2 * t, HEAD_DIM), lambda b, hh, i: (i, 0))
    return pl.pallas_call(
        functools.partial(_moba_attn_body, t=t, nh=nh, nb=nb, n_sel=n_sel),
        grid=(B, N_HEADS // nh, nb // 2),
        in_specs=[pl.BlockSpec((1, 2 * t, nh * HEAD_DIM), lambda b, hh, i: (b, i, hh)),
                  pl.BlockSpec((1, HEAD_DIM), lambda b, hh, i: (0, 0)), tab, tab,
                  pl.BlockSpec((1, nh, nb, HEAD_DIM), lambda b, hh, i: (b, hh, 0, 0)),
                  pl.BlockSpec((1, S, nh * AUG), lambda b, hh, i: (b, 0, hh)),
                  pl.BlockSpec((1, nh, nb, HEAD_DIM, t), lambda b, hh, i: (b, hh, 0, 0, 0))],
        out_specs=pl.BlockSpec((1, 2 * t, nh * HEAD_DIM), lambda b, hh, i: (b, i, hh)),
        out_shape=jax.ShapeDtypeStruct((B, S, WIDTH), BF16),
        scratch_shapes=_attn_scratch(nh, t),
        compiler_params=_params("parallel", "parallel", "arbitrary"),
        name="moba_attn",
    )(h, q_norm.reshape(1, HEAD_DIM), cos2, sin2, kmean, k_aug, v_t)


def _moba_layer(x, norm, w_in, q_norm, k_norm, w_out, cos2, sin2):
    B, S, D = x.shape
    x2 = x.reshape(B * S, D)
    h2 = _in_proj(x2, norm, w_in.astype(BF16), F32, 1024, 1024)
    h = h2.reshape(B, S, -1)
    k_aug, v_t, kmean = _moba_prep_k(h, k_norm, cos2, sin2)
    o = _moba_attn(h, q_norm, cos2, sin2, kmean, k_aug, v_t)
    out = _out_proj(x2, h2, 3, w_out.astype(BF16), [o.reshape(B * S, WIDTH)])
    return out.reshape(B, S, D)


def _split2_dot(a, u):
    hi = a.astype(BF16)
    lo = (a - hi.astype(F32)).astype(BF16)
    return jnp.dot(hi, u, preferred_element_type=F32) + jnp.dot(lo, u, preferred_element_type=F32)


def _sb_attn_body(q_ref, k_ref, v_ref, o_ref, kt_ref, carry_ref, acc_ref, *, t, nh):
    qi = pl.program_id(2)

    @pl.when(qi == 0)
    def _():
        for r in range(nh):
            _store_tiles_t(kt_ref, (r,), k_ref[0, :, r * HEAD_DIM:(r + 1) * HEAD_DIM].astype(F32), t)

    after = jnp.where(_iota((t, t), 0) > _iota((t, t), 1), 1.0, 0.0).astype(BF16)
    past = _iota((t, t), 1) < _iota((t, t), 0)
    qs = [(q_ref[0, :, r * HEAD_DIM:(r + 1) * HEAD_DIM].astype(F32) * ATTN_SCALE).astype(BF16)
          for r in range(nh)]

    def tile(j, diag):
        zs = [jnp.dot(qs[r], kt_ref[r, j], preferred_element_type=F32) for r in range(nh)]
        staged = []
        for r in range(nh):
            z = zs[r]
            drop = jnp.maximum(z, 0.0) + jnp.log(1.0 + jnp.exp2(jnp.abs(z) * (-LOG2E)))
            if diag:
                drop = jnp.where(past, drop, 0.0)
            suffix = _split2_dot(drop, after)
            staged.append((drop, suffix, suffix[:, 0:1] + drop[:, 0:1]))
        for r in range(nh):
            drop, suffix, total = staged[r]
            vs = v_ref[0, pl.ds(pl.multiple_of(j * t, t), t), r * HEAD_DIM:(r + 1) * HEAD_DIM]
            if diag:
                behind, carry = suffix, total
            else:
                carry = carry_ref[r]
                behind, carry = suffix + carry, carry + total
            a = jnp.exp(zs[r] - (drop + behind))
            if diag:
                a = jnp.where(past, a, 0.0)
            o = jnp.dot(a.astype(BF16), vs, preferred_element_type=F32)
            acc_ref[r] = o if diag else acc_ref[r] + o
            carry_ref[r] = carry
            least = jnp.min(carry) if r == 0 else jnp.minimum(least, jnp.min(carry))
        return least

    def body(c):
        return c[0] + 1, tile(qi - 1 - c[0], False)

    lax.while_loop(lambda c: (c[0] < qi) & (c[1] < SB_CUTOFF), body, (jnp.int32(0), tile(qi, True)))
    for r in range(nh):
        o_ref[0, :, r * HEAD_DIM:(r + 1) * HEAD_DIM] = acc_ref[r].astype(o_ref.dtype)


def _sb_attn(h):
    B, S, _ = h.shape
    t = ATT_TILE
    nh = HEADS_PER_STEP
    nhb = N_HEADS // nh
    w = nh * HEAD_DIM
    return pl.pallas_call(
        functools.partial(_sb_attn_body, t=t, nh=nh),
        grid=(B, nhb, S // t),
        in_specs=[pl.BlockSpec((1, t, w), lambda b, hh, i: (b, i, hh)),
                  pl.BlockSpec((1, S, w), lambda b, hh, i: (b, 0, nhb + hh)),
                  pl.BlockSpec((1, S, w), lambda b, hh, i: (b, 0, 2 * nhb + hh))],
        out_specs=pl.BlockSpec((1, t, w), lambda b, hh, i: (b, i, hh)),
        out_shape=jax.ShapeDtypeStruct((B, S, WIDTH), BF16),
        scratch_shapes=[pltpu.VMEM((nh, S // t, HEAD_DIM, t), BF16), pltpu.VMEM((nh, t, 1), F32),
                        pltpu.VMEM((nh, t, HEAD_DIM), F32)],
        compiler_params=_params("parallel", "parallel", "arbitrary"),
        name="sb_attn",
    )(h, h, h)


def _sb_layer(x, norm, w_in, w_out):
    B, S, D = x.shape
    x2 = x.reshape(B * S, D)
    h2 = _in_proj(x2, norm, w_in.astype(BF16), BF16, 1024, 1024)
    o = _sb_attn(h2.reshape(B, S, -1))
    out = _out_proj(x2, h2, 3, w_out.astype(BF16), [o.reshape(B * S, WIDTH)])
    return out.reshape(B, S, D)


NSA_KV_BLK0 = 2 * N_HEADS


def _nsa_compress_body(kc_ref, vc_ref, wk_ref, wv_ref, pos_ref, kcn_ref, cosc_ref, sinc_ref,
                       kcc_ref, vcct_ref, *, n_rows):
    half = CMP_BLOCK // CMP_STRIDE
    assert half == 2

    def compress(src_ref, w_ref):
        first = jnp.zeros((n_rows, HEAD_DIM), F32)
        second = jnp.zeros((n_rows, HEAD_DIM), F32)
        for l in range(CMP_STRIDE):
            rows = src_ref[0, pl.ds(l, n_rows, stride=CMP_STRIDE), :]
            first = first + jnp.dot(rows + pos_ref[l:l + 1, :], w_ref[l],
                                    precision=HIGHEST, preferred_element_type=F32)
            second = second + jnp.dot(rows + pos_ref[CMP_STRIDE + l:CMP_STRIDE + l + 1, :],
                                      w_ref[CMP_STRIDE + l], precision=HIGHEST, preferred_element_type=F32)
        return first + pltpu.roll(second, n_rows - 1, 0)

    kc = compress(kc_ref, wk_ref)
    kcc_ref[0, 0] = _rms_rope(kc, kcn_ref[...], cosc_ref[...], sinc_ref[...])
    vcct_ref[0, 0] = compress(vc_ref, wv_ref).T.astype(vcct_ref.dtype)


def _nsa_compress(h, cmp_wk, cmp_wv, cmp_pos, kc_norm, cos_c, sin_c):
    B, S, _ = h.shape
    n_rows = S // CMP_STRIDE
    wspec = pl.BlockSpec((CMP_BLOCK, HEAD_DIM, HEAD_DIM), lambda b, g: (0, 0, 0))
    tab = pl.BlockSpec((n_rows, HEAD_DIM), lambda b, g: (0, 0))
    out = pl.BlockSpec((1, 1, n_rows, HEAD_DIM), lambda b, g: (b, g, 0, 0))
    return pl.pallas_call(
        functools.partial(_nsa_compress_body, n_rows=n_rows),
        grid=(B, NSA_KV_HEADS),
        in_specs=[pl.BlockSpec((1, S, HEAD_DIM), lambda b, g: (b, 0, NSA_KV_BLK0 + g)),
                  pl.BlockSpec((1, S, HEAD_DIM), lambda b, g: (b, 0, NSA_KV_BLK0 + NSA_KV_HEADS + g)),
                  wspec, wspec,
                  pl.BlockSpec((CMP_BLOCK, HEAD_DIM), lambda b, g: (0, 0)),
                  pl.BlockSpec((1, HEAD_DIM), lambda b, g: (0, 0)), tab, tab],
        out_specs=[out, pl.BlockSpec((1, 1, HEAD_DIM, n_rows), lambda b, g: (b, g, 0, 0))],
        out_shape=[jax.ShapeDtypeStruct((B, NSA_KV_HEADS, n_rows, HEAD_DIM), F32),
                   jax.ShapeDtypeStruct((B, NSA_KV_HEADS, HEAD_DIM, n_rows), BF16)],
        compiler_params=_params("parallel", "parallel"),
        name="nsa_compress",
    )(h, h, cmp_wk, cmp_wv, cmp_pos, kc_norm.reshape(1, HEAD_DIM), cos_c, sin_c)


def _nsa_prep_kv_body(ks_ref, vs_ref, kw_ref, vw_ref, ksn_ref, kwn_ref, cos_ref, sin_ref,
                      ksaug_ref, vst_ref, kwb_ref, vwt_ref, *, t):
    cos2, sin2 = cos_ref[...], sin_ref[...]
    S = cos2.shape[0]
    ksaug_ref[0, :, :HEAD_DIM] = _rms_rope(ks_ref[0], ksn_ref[...], cos2, sin2).astype(BF16)
    ksaug_ref[0, :, HEAD_DIM:] = _onehot_block(S, _log2(SLC_BLOCK))
    kwb_ref[0] = _rms_rope(kw_ref[0], kwn_ref[...], cos2, sin2).astype(BF16)
    _store_tiles_t(vst_ref, (0, 0), vs_ref[0], t)
    _store_tiles_t(vwt_ref, (0, 0), vw_ref[0], t)


def _nsa_prep_kv(h, ks_norm, kw_norm, cos2, sin2):
    B, S, _ = h.shape
    G = NSA_KV_HEADS
    t = ATT_TILE

    def src(i):
        return pl.BlockSpec((1, S, HEAD_DIM), lambda b, g: (b, 0, NSA_KV_BLK0 + i * G + g))

    vec = pl.BlockSpec((1, HEAD_DIM), lambda b, g: (0, 0))
    full = pl.BlockSpec((S, HEAD_DIM), lambda b, g: (0, 0))
    vt = pl.BlockSpec((1, 1, S // t, HEAD_DIM, t), lambda b, g: (b, g, 0, 0, 0))
    vt_shape = jax.ShapeDtypeStruct((B, G, S // t, HEAD_DIM, t), BF16)
    return pl.pallas_call(
        functools.partial(_nsa_prep_kv_body, t=t),
        grid=(B, G),
        in_specs=[src(2), src(3), src(4), src(5), vec, vec, full, full],
        out_specs=[pl.BlockSpec((1, S, AUG), lambda b, g: (b, 0, g)), vt,
                   pl.BlockSpec((1, S, HEAD_DIM), lambda b, g: (b, 0, g)), vt],
        out_shape=[jax.ShapeDtypeStruct((B, S, G * AUG), BF16), vt_shape,
                   jax.ShapeDtypeStruct((B, S, KV_WIDTH), BF16), vt_shape],
        compiler_params=_params("parallel", "parallel"),
        name="nsa_prep_kv",
    )(h, h, h, h, ks_norm.reshape(1, HEAD_DIM), kw_norm.reshape(1, HEAD_DIM), cos2, sin2)


def _nsa_cmp_sel_body(q_ref, qn_ref, cos_ref, sin_ref, kcc_ref, vcct_ref, bg_ref, ocmp_ref, qt_ref,
                      *, t, n_cmp, n_slc, n_sel):
    g_idx = pl.program_id(1)
    q0 = pl.program_id(2) * t
    cos2, sin2 = cos_ref[...], sin_ref[...]
    kcc = kcc_ref[0, 0]
    kcc_hi = kcc.astype(BF16)
    kcc_lo = (kcc - kcc_hi.astype(F32)).astype(BF16)
    kcc3 = jnp.concatenate([kcc_hi, kcc_hi, kcc_lo], axis=1)
    vcct = vcct_ref[0, 0]
    cmp_ok = _iota((n_cmp, t), 0) * CMP_STRIDE + (CMP_BLOCK - 1) <= q0 + _iota((n_cmp, t), 1)

    p_sum = jnp.zeros((n_cmp, t), F32)
    for r in range(NSA_HPG):
        qnt = _rms_rope(q_ref[0, :, r * HEAD_DIM:(r + 1) * HEAD_DIM], qn_ref[...], cos2, sin2).T
        qt_ref[0, r, 0, :HEAD_DIM, :] = (qnt * (ATTN_SCALE * LOG2E)).astype(BF16)
        q_hi = qnt.astype(BF16)
        q_lo = (qnt - q_hi.astype(F32)).astype(BF16)
        logits = jnp.dot(kcc3, jnp.concatenate([q_hi, q_lo, q_hi], axis=0),
                         preferred_element_type=F32) * ATTN_SCALE
        logits = jnp.where(cmp_ok, logits, NEG)
        mx = jnp.max(logits, axis=0, keepdims=True)
        e = jnp.where(cmp_ok, jnp.exp(logits - mx), 0.0)
        p = e * (1.0 / jnp.maximum(jnp.sum(e, axis=0, keepdims=True), TINY))
        p_sum = p_sum + p
        o = jnp.dot(vcct, p.astype(BF16), preferred_element_type=F32).T
        o = o * _branch_gate(bg_ref, g_idx * NSA_HPG + r)
        ocmp_ref[0, :, r * HEAD_DIM:(r + 1) * HEAD_DIM] = o.astype(ocmp_ref.dtype)

    c_start = _iota((n_slc, n_cmp), 1) * CMP_STRIDE
    s_start = _iota((n_slc, n_cmp), 0) * SLC_BLOCK
    overlap = jnp.where((c_start < s_start + SLC_BLOCK) & (c_start + CMP_BLOCK > s_start), 1.0, 0.0).astype(BF16)
    ps_hi = p_sum.astype(BF16)
    rest = p_sum - ps_hi.astype(F32)
    ps_mid = rest.astype(BF16)
    ps_lo = (rest - ps_mid.astype(F32)).astype(BF16)
    imp = jnp.dot(jnp.concatenate([overlap, overlap, overlap], axis=1),
                  jnp.concatenate([ps_hi, ps_mid, ps_lo], axis=0), preferred_element_type=F32)
    blk = _iota((n_slc, t), 0)
    cur = (q0 + _iota((n_slc, t), 1)) >> _log2(SLC_BLOCK)
    imp = jnp.where(blk == 0, FORCE, imp)
    imp = jnp.where(blk == cur, FORCE, imp)
    imp = jnp.where(blk == cur - 1, FORCE, imp)
    imp = jnp.where(blk <= cur, imp, NEG)
    sub = 8
    ranks = []
    for g0 in range(0, n_slc, sub):
        mine = imp[g0:g0 + sub, :]
        row = _iota((sub, t), 0) + g0
        rank = jnp.zeros((sub, t), F32)
        for k in range(n_slc):
            c = imp[k:k + 1, :]
            ge = jnp.where(c >= mine, 1.0, 0.0)
            gt = jnp.where(c > mine, 1.0, 0.0)
            rank = rank + (ge if k < g0 else gt if k >= g0 + sub else jnp.where(row > k, ge, gt))
        ranks.append(rank)
    bias = jnp.where(jnp.concatenate(ranks, axis=0) < float(n_sel), 0.0, NEG)
    bias = jnp.concatenate([bias, jnp.zeros((HEAD_DIM - n_slc, t), F32)], axis=0).astype(BF16)
    for r in range(NSA_HPG):
        qt_ref[0, r, 0, HEAD_DIM:, :] = bias


def _nsa_cmp_sel(h, bg, q_norm, cos2, sin2, kc_c, vc_ct):
    B, S, _ = h.shape
    t = ATT_TILE
    G, R = NSA_KV_HEADS, NSA_HPG
    n_cmp = S // CMP_STRIDE
    n_slc = S // SLC_BLOCK
    assert n_slc <= HEAD_DIM and n_slc % 8 == 0
    tab = pl.BlockSpec((t, HEAD_DIM), lambda b, g, i: (i, 0))
    return pl.pallas_call(
        functools.partial(_nsa_cmp_sel_body, t=t, n_cmp=n_cmp, n_slc=n_slc, n_sel=min(SLC_TOPK, n_slc)),
        grid=(B, G, S // t),
        in_specs=[pl.BlockSpec((1, t, R * HEAD_DIM), lambda b, g, i: (b, i, g)),
                  pl.BlockSpec((1, HEAD_DIM), lambda b, g, i: (0, 0)), tab, tab,
                  pl.BlockSpec((1, 1, n_cmp, HEAD_DIM), lambda b, g, i: (b, g, 0, 0)),
                  pl.BlockSpec((1, 1, HEAD_DIM, n_cmp), lambda b, g, i: (b, g, 0, 0)),
                  pl.BlockSpec((1, t, LANES), lambda b, g, i: (b, i, 0))],
        out_specs=[pl.BlockSpec((1, t, R * HEAD_DIM), lambda b, g, i: (b, i, g)),
                   pl.BlockSpec((1, R, 1, AUG, t), lambda b, g, i: (b, g, i, 0, 0))],
        out_shape=[jax.ShapeDtypeStruct((B, S, WIDTH), BF16),
                   jax.ShapeDtypeStruct((B, N_HEADS, S // t, AUG, t), BF16)],
        compiler_params=_params("parallel", "parallel", "arbitrary"),
        name="nsa_cmp_sel",
    )(h, q_norm.reshape(1, HEAD_DIM), cos2, sin2, kc_c, vc_ct, bg)


def _win_attn_body(qt_ref, k_ref, vt_ref, bg_ref, o_ref, *, t, nh, n_back, gate_base):
    qi = pl.program_id(2)
    n_blk = n_back + 1
    j0 = jnp.maximum(qi - n_back, 0)
    k = k_ref[0, pl.ds(pl.multiple_of(j0 * t, t), n_blk * t), :]
    vt = jnp.concatenate([vt_ref[0, 0, j0 + d] for d in range(n_blk)], axis=1)
    dist = _iota((n_blk * t, t), 1) - _iota((n_blk * t, t), 0) + (qi - j0) * t
    ss = [jnp.dot(k, qt_ref[0, r, 0], preferred_element_type=F32) for r in range(nh)]
    states = []
    for r in range(nh):
        s = jnp.where(dist >= 0, ss[r], NEG)
        s = jnp.where(dist < WINDOW, s, NEG)
        m = jnp.max(s, axis=0, keepdims=True)
        p = jnp.exp2(s - m)
        states.append((jnp.sum(p, axis=0, keepdims=True), jnp.dot(vt, p.astype(BF16), preferred_element_type=F32)))
    _store_heads(o_ref, states, bg_ref, gate_base + pl.program_id(1) * nh)


def _win_attn(q_t, kw, vw_t, bg, gate_base):
    B, H, n_t, _, t = q_t.shape
    S = n_t * t
    nh = NSA_HPG
    assert WINDOW % t == 0 and n_t > WINDOW // t
    return pl.pallas_call(
        functools.partial(_win_attn_body, t=t, nh=nh, n_back=WINDOW // t, gate_base=gate_base),
        grid=(B, H // nh, n_t),
        in_specs=[pl.BlockSpec((1, nh, 1, HEAD_DIM, t), lambda b, g, i: (b, g, i, 0, 0)),
                  pl.BlockSpec((1, S, HEAD_DIM), lambda b, g, i: (b, 0, g)),
                  pl.BlockSpec((1, 1, n_t, HEAD_DIM, t), lambda b, g, i: (b, g, 0, 0, 0)),
                  pl.BlockSpec((1, t, LANES), lambda b, g, i: (b, i, 0))],
        out_specs=pl.BlockSpec((1, t, nh * HEAD_DIM), lambda b, g, i: (b, i, g)),
        out_shape=jax.ShapeDtypeStruct((B, S, WIDTH), BF16),
        compiler_params=_params("parallel", "parallel", "arbitrary"),
        name="win_attn",
    )(q_t, kw, vw_t, bg)


def _nsa_layer(x, norm, w_in, q_norm, kc_norm, ks_norm, kw_norm, cmp_wk, cmp_wv, cmp_pos, w_out, cos2, sin2):
    B, S, D = x.shape
    x2 = x.reshape(B * S, D)
    kv_end = WIDTH + 6 * KV_WIDTH
    gate_end = kv_end + WIDTH
    w_main = jnp.concatenate([w_in[:, :WIDTH], w_in[:, kv_end:gate_end], w_in[:, WIDTH:kv_end]], axis=1)
    w_bg = jnp.pad(w_in[:, gate_end:], ((0, 0), (0, LANES - 3 * N_HEADS)))
    h2 = _in_proj(x2, norm, w_main.astype(BF16), F32, 1024, 1024)
    bg = _in_proj(x2, norm, w_bg.astype(BF16), F32, 512, LANES).reshape(B, S, LANES)
    h = h2.reshape(B, S, -1)
    cos_c, sin_c = _rope_tables(jnp.arange(S // CMP_STRIDE) * CMP_STRIDE + (CMP_BLOCK - 1))
    kc_c, vc_ct = _nsa_compress(h, cmp_wk, cmp_wv, cmp_pos, kc_norm, cos_c, sin_c)
    ks_aug, vs_t, kw_b, vw_t = _nsa_prep_kv(h, ks_norm, kw_norm, cos2, sin2)
    o_cmp, q_t = _nsa_cmp_sel(h, bg, q_norm, cos2, sin2, kc_c, vc_ct)
    o_slc = _sel_attn(q_t, ks_aug, vs_t, bg, N_HEADS)
    o_win = _win_attn(q_t, kw_b, vw_t, bg, 2 * N_HEADS)
    os = [o.reshape(B * S, WIDTH) for o in (o_cmp, o_slc, o_win)]
    out = _out_proj(x2, h2, 1, w_out.astype(BF16), os)
    return out.reshape(B, S, D)


def kernel(x, l0_norm, l0_w_in, l0_q_norm, l0_k_norm, l0_w_out, l1_norm, l1_w_in, l1_w_out, l2_norm, l2_w_in, l2_q_norm, l2_kc_norm, l2_ks_norm, l2_kw_norm, l2_cmp_wk, l2_cmp_wv, l2_cmp_pos, l2_w_out, l3_norm, l3_w_in, l3_q_norm, l3_k_norm, l3_w_out):
    cos2, sin2 = _rope_tables(jnp.arange(x.shape[1]))
    x = _moba_layer(x, l0_norm, l0_w_in, l0_q_norm, l0_k_norm, l0_w_out, cos2, sin2)
    x = _sb_layer(x, l1_norm, l1_w_in, l1_w_out)
    x = _nsa_layer(x, l2_norm, l2_w_in, l2_q_norm, l2_kc_norm, l2_ks_norm, l2_kw_norm,
                   l2_cmp_wk, l2_cmp_wv, l2_cmp_pos, l2_w_out, cos2, sin2)
    x = _moba_layer(x, l3_norm, l3_w_in, l3_q_norm, l3_k_norm, l3_w_out, cos2, sin2)
    return x
```

```python
import functools
import math

import jax
import jax.numpy as jnp
from jax import lax
from jax.experimental import pallas as pl
from jax.experimental.pallas import tpu as pltpu

F32 = jnp.float32
BF16 = jnp.bfloat16
HIGHEST = lax.Precision.HIGHEST

HEAD_DIM = 128
N_HEADS = 16
WIDTH = N_HEADS * HEAD_DIM
ROPE_THETA = 10000.0
EPS = 1e-6
ATTN_SCALE = HEAD_DIM ** -0.5
LOG2E = math.log2(math.e)
NEG = -1e30
LOWEST = -3e38
FORCE = 1e30
TINY = 1e-30
MOBA_BLOCK = 256
MOBA_TOPK = 3
NSA_KV_HEADS = 4
NSA_HPG = N_HEADS // NSA_KV_HEADS
KV_WIDTH = NSA_KV_HEADS * HEAD_DIM
CMP_BLOCK = 32
CMP_STRIDE = 16
SLC_BLOCK = 64
SLC_TOPK = 16
WINDOW = 512

LANES = 128
AUG = 2 * HEAD_DIM
ATT_TILE = 256
HEADS_PER_STEP = 4
SB_CUTOFF = 110.0
NSA_KV_PER_STEP = 2
VMEM_LIMIT = 48 * 1024 * 1024


def _params(*sem):
    return pltpu.CompilerParams(dimension_semantics=sem, vmem_limit_bytes=VMEM_LIMIT)


def _rope_tables(pos):
    half = HEAD_DIM // 2
    inv_freq = jnp.exp(-math.log(ROPE_THETA) * jnp.arange(half, dtype=F32) / half)
    ang = pos.astype(F32)[:, None] * inv_freq[None, :]
    cos, sin = jnp.cos(ang), jnp.sin(ang)
    return jnp.concatenate([cos, cos], axis=-1), jnp.concatenate([-sin, sin], axis=-1)


def _rms_rope(x, g, cos2, sin2):
    y = x * lax.rsqrt(jnp.mean(x * x, axis=-1, keepdims=True) + EPS) * g
    return y * cos2 + pltpu.roll(y, HEAD_DIM // 2, 1) * sin2


def _iota(shape, dim):
    return lax.broadcasted_iota(jnp.int32, shape, dim)


def _log2(n):
    s = n.bit_length() - 1
    assert 1 << s == n
    return s


def _store_tiles_t(dst_ref, lead, x, t):
    for j in range(x.shape[0] // t):
        dst_ref[lead + (j,)] = x[j * t:(j + 1) * t, :].T.astype(dst_ref.dtype)


def _in_proj_body(x_ref, g_ref, w_ref, o_ref, xn_ref):
    @pl.when(pl.program_id(1) == 0)
    def _():
        x = x_ref[...]
        r = lax.rsqrt(jnp.mean(x * x, axis=-1, keepdims=True) + EPS)
        xn_ref[...] = (x * r * g_ref[...]).astype(xn_ref.dtype)

    o_ref[...] = jnp.dot(xn_ref[...], w_ref[...], preferred_element_type=F32).astype(o_ref.dtype)


def _in_proj(x2, g, w, out_dtype, tm, tn):
    M, K = x2.shape
    N = w.shape[1]
    return pl.pallas_call(
        _in_proj_body,
        grid=(M // tm, N // tn),
        in_specs=[pl.BlockSpec((tm, K), lambda i, j: (i, 0)),
                  pl.BlockSpec((1, K), lambda i, j: (0, 0)),
                  pl.BlockSpec((K, tn), lambda i, j: (0, j))],
        out_specs=pl.BlockSpec((tm, tn), lambda i, j: (i, j)),
        out_shape=jax.ShapeDtypeStruct((M, N), out_dtype),
        scratch_shapes=[pltpu.VMEM((tm, K), BF16)],
        compiler_params=_params("parallel", "arbitrary"),
        name="in_proj",
    )(x2, g.reshape(1, K), w)


def _out_proj_body(*refs, n_o):
    x_ref, gate_lo_ref, gate_hi_ref, w_ref = refs[:4]
    o_refs = refs[4:4 + n_o]
    out_ref = refs[4 + n_o]
    o = o_refs[0][...].astype(F32)
    for r in o_refs[1:]:
        o = o + r[...].astype(F32)
    gate = jnp.concatenate([gate_lo_ref[...], gate_hi_ref[...]], axis=1).astype(F32)
    y = o * (gate / (1.0 + jnp.exp(-gate)))
    out_ref[...] = x_ref[...] + jnp.dot(y.astype(BF16), w_ref[...], preferred_element_type=F32)


def _out_proj(x2, h2, gate_col, w, os, tm=256):
    M, D = x2.shape
    n_o = len(os)
    half = D // 2
    assert gate_col % half == 0
    row = pl.BlockSpec((tm, D), lambda i: (i, 0))
    return pl.pallas_call(
        functools.partial(_out_proj_body, n_o=n_o),
        grid=(M // tm,),
        in_specs=[row, pl.BlockSpec((tm, half), lambda i: (i, gate_col // half)),
                  pl.BlockSpec((tm, half), lambda i: (i, gate_col // half + 1)),
                  pl.BlockSpec((D, D), lambda i: (0, 0))] + [row] * n_o,
        out_specs=row,
        out_shape=jax.ShapeDtypeStruct((M, D), F32),
        compiler_params=_params("parallel"),
        name="out_proj",
    )(x2, h2, h2, w, *os)


def _branch_gate(bg_ref, col):
    bg = bg_ref[0]
    g = 1.0 / (1.0 + jnp.exp(-bg))
    return jnp.sum(jnp.where(_iota(bg.shape, 1) == col, g, 0.0), axis=1, keepdims=True)


def _softmax_block(ml_ref, acc_ref, r, s, vt, first):
    if first:
        m_new = jnp.max(s, axis=0, keepdims=True)
        p = jnp.exp2(s - m_new)
        l = jnp.sum(p, axis=0, keepdims=True)
        acc_ref[r] = jnp.dot(vt, p.astype(BF16), preferred_element_type=F32)
    else:
        m = ml_ref[2 * r:2 * r + 1, :]
        m_new = jnp.maximum(m, jnp.max(s, axis=0, keepdims=True))
        alpha = jnp.exp2(m - m_new)
        p = jnp.exp2(s - m_new)
        l = alpha * ml_ref[2 * r + 1:2 * r + 2, :] + jnp.sum(p, axis=0, keepdims=True)
        acc_ref[r] = alpha * acc_ref[r] + jnp.dot(vt, p.astype(BF16), preferred_element_type=F32)
    ml_ref[2 * r:2 * r + 1, :] = m_new
    ml_ref[2 * r + 1:2 * r + 2, :] = l


def _store_heads(o_ref, sums_accs, bg_ref, gate_col0):
    for r, (l, acc) in enumerate(sums_accs):
        o = (acc * (1.0 / l)).T
        if bg_ref is not None:
            o = o * _branch_gate(bg_ref, gate_col0 + r)
        o_ref[0, :, r * HEAD_DIM:(r + 1) * HEAD_DIM] = o.astype(o_ref.dtype)


def _attend_causal(qts, k_ref, vt_ref, ml_ref, acc_ref, qi, t, hpg):
    nh = len(qts)
    w = 2 * t

    def block(j0, diag):
        off = pl.multiple_of(j0 * t, w)
        ss = [jnp.dot(k_ref[0, pl.ds(off, w), (r // hpg) * AUG:(r // hpg + 1) * AUG], qts[r],
                      preferred_element_type=F32) for r in range(nh)]
        if diag:
            causal = _iota((w, w), 0) <= _iota((w, w), 1)
        for r in range(nh):
            s = jnp.where(causal, ss[r], NEG) if diag else ss[r]
            vt = jnp.concatenate([vt_ref[0, r // hpg, j0], vt_ref[0, r // hpg, j0 + 1]], axis=1)
            _softmax_block(ml_ref, acc_ref, r, s, vt, diag)

    block(2 * qi, True)

    def body(j2, carry):
        block(2 * j2, False)
        return carry

    lax.fori_loop(0, qi, body, 0)
    return [(ml_ref[2 * r + 1:2 * r + 2, :], acc_ref[r]) for r in range(nh)]


def _attn_scratch(nh, t):
    return [pltpu.VMEM((2 * nh, 2 * t), F32), pltpu.VMEM((nh, HEAD_DIM, 2 * t), F32)]


def _sel_attn_body(qt_ref, k_ref, vt_ref, bg_ref, o_ref, ml_ref, acc_ref, *, t, nh, gate_base):
    qts = [jnp.concatenate([qt_ref[0, r, 0], qt_ref[0, r, 1]], axis=1) for r in range(nh)]
    sums_accs = _attend_causal(qts, k_ref, vt_ref, ml_ref, acc_ref, pl.program_id(2), t, NSA_HPG)
    _store_heads(o_ref, sums_accs, bg_ref, gate_base + pl.program_id(1) * nh)


def _sel_attn(q_t, k_aug, v_t, bg, gate_base):
    B, H, n_t, _, t = q_t.shape
    S = n_t * t
    nk = NSA_KV_PER_STEP
    nh = nk * NSA_HPG
    assert n_t % 2 == 0
    return pl.pallas_call(
        functools.partial(_sel_attn_body, t=t, nh=nh, gate_base=gate_base),
        grid=(B, H // nh, n_t // 2),
        in_specs=[pl.BlockSpec((1, nh, 2, AUG, t), lambda b, h, i: (b, h, i, 0, 0)),
                  pl.BlockSpec((1, S, nk * AUG), lambda b, h, i: (b, 0, h)),
                  pl.BlockSpec((1, nk, n_t, HEAD_DIM, t), lambda b, h, i: (b, h, 0, 0, 0)),
                  pl.BlockSpec((1, 2 * t, LANES), lambda b, h, i: (b, i, 0))],
        out_specs=pl.BlockSpec((1, 2 * t, nh * HEAD_DIM), lambda b, h, i: (b, i, h)),
        out_shape=jax.ShapeDtypeStruct((B, S, WIDTH), BF16),
        scratch_shapes=_attn_scratch(nh, t),
        compiler_params=_params("parallel", "parallel", "arbitrary"),
        name="sel_attn",
    )(q_t, k_aug, v_t, bg)


def _onehot_block(S, shift):
    return jnp.where((_iota((S, LANES), 0) >> shift) == _iota((S, LANES), 1), 1.0, 0.0).astype(BF16)


def _moba_prep_k_body(k_ref, v_ref, kn_ref, cos_ref, sin_ref, kaug_ref, vt_ref, kmean_ref, *, nb, t):
    kn = _rms_rope(k_ref[0], kn_ref[...], cos_ref[...], sin_ref[...])
    S = kn.shape[0]
    kaug_ref[0, :, :HEAD_DIM] = kn.astype(BF16)
    kaug_ref[0, :, HEAD_DIM:] = _onehot_block(S, _log2(MOBA_BLOCK))
    _store_tiles_t(vt_ref, (0, 0), v_ref[0], t)
    kmean_ref[0, 0] = jnp.mean(kn.reshape(nb, MOBA_BLOCK, HEAD_DIM), axis=1)


def _moba_prep_k(h, k_norm, cos2, sin2):
    B, S, _ = h.shape
    nb = S // MOBA_BLOCK
    t = ATT_TILE
    assert 1 < nb <= HEAD_DIM
    full = pl.BlockSpec((S, HEAD_DIM), lambda b, hh: (0, 0))
    return pl.pallas_call(
        functools.partial(_moba_prep_k_body, nb=nb, t=t),
        grid=(B, N_HEADS),
        in_specs=[pl.BlockSpec((1, S, HEAD_DIM), lambda b, hh: (b, 0, N_HEADS + hh)),
                  pl.BlockSpec((1, S, HEAD_DIM), lambda b, hh: (b, 0, 2 * N_HEADS + hh)),
                  pl.BlockSpec((1, HEAD_DIM), lambda b, hh: (0, 0)), full, full],
        out_specs=[pl.BlockSpec((1, S, AUG), lambda b, hh: (b, 0, hh)),
                   pl.BlockSpec((1, 1, S // t, HEAD_DIM, t), lambda b, hh: (b, hh, 0, 0, 0)),
                   pl.BlockSpec((1, 1, nb, HEAD_DIM), lambda b, hh: (b, hh, 0, 0))],
        out_shape=[jax.ShapeDtypeStruct((B, S, N_HEADS * AUG), BF16),
                   jax.ShapeDtypeStruct((B, N_HEADS, S // t, HEAD_DIM, t), BF16),
                   jax.ShapeDtypeStruct((B, N_HEADS, nb, HEAD_DIM), F32)],
        compiler_params=_params("parallel", "parallel"),
        name="moba_prep_k",
    )(h, h, k_norm.reshape(1, HEAD_DIM), cos2, sin2)


def _moba_attn_body(q_ref, qn_ref, cos_ref, sin_ref, kmean_ref, k_ref, vt_ref, o_ref, ml_ref, acc_ref,
                    *, t, nh, nb, n_sel):
    qi = pl.program_id(2)
    cos2, sin2 = cos_ref[...], sin_ref[...]
    w = 2 * t
    blk = _iota((nb, w), 0)
    blk_f = blk.astype(F32)
    cur = 2 * qi + (_iota((nb, w), 1) >> _log2(t))
    qts = []
    for r in range(nh):
        qn = _rms_rope(q_ref[0, :, r * HEAD_DIM:(r + 1) * HEAD_DIM], qn_ref[...], cos2, sin2)
        qnt = qn.T
        gate = jnp.dot(kmean_ref[0, r], qnt, precision=HIGHEST, preferred_element_type=F32)
        g = jnp.where(blk < cur, gate, NEG)
        picked = jnp.zeros((nb, w), F32)
        for _ in range(n_sel):
            mx = jnp.max(g, axis=0, keepdims=True)
            first = jnp.min(jnp.where(g == mx, blk_f, float(nb)), axis=0, keepdims=True)
            hit = blk_f == first
            picked = jnp.where(hit, 1.0, picked)
            g = jnp.where(hit, LOWEST, g)
        attend = jnp.where(blk < cur, picked, jnp.where(blk == cur, 1.0, 0.0))
        bias = jnp.where(attend > 0.0, 0.0, NEG)
        qts.append(jnp.concatenate([qnt * (ATTN_SCALE * LOG2E), bias, jnp.zeros((HEAD_DIM - nb, w), F32)],
                                   axis=0).astype(BF16))
    _store_heads(o_ref, _attend_causal(qts, k_ref, vt_ref, ml_ref, acc_ref, qi, t, 1), None, None)


def _moba_attn(h, q_norm, cos2, sin2, kmean, k_aug, v_t):
    B, S, _ = h.shape
    t = MOBA_BLOCK
    assert t == ATT_TILE
    nb = S // MOBA_BLOCK
    assert nb % 2 == 0
    nh = HEADS_PER_STEP
    n_sel = min(MOBA_TOPK, nb - 1)
    tab = pl.BlockSpec((2 * t, HEAD_DIM), lambda b, hh, i: (i, 0))
    return pl.pallas_call(
        functools.partial(_moba_attn_body, t=t, nh=nh, nb=nb, n_sel=n_sel),
        grid=(B, N_HEADS // nh, nb // 2),
        in_specs=[pl.BlockSpec((1, 2 * t, nh * HEAD_DIM), lambda b, hh, i: (b, i, hh)),
                  pl.BlockSpec((1, HEAD_DIM), lambda b, hh, i: (0, 0)), tab, tab,
                  pl.BlockSpec((1, nh, nb, HEAD_DIM), lambda b, hh, i: (b, hh, 0, 0)),
                  pl.BlockSpec((1, S, nh * AUG), lambda b, hh, i: (b, 0, hh)),
                  pl.BlockSpec((1, nh, nb, HEAD_DIM, t), lambda b, hh, i: (b, hh, 0, 0, 0))],
        out_specs=pl.BlockSpec((1, 2 * t, nh * HEAD_DIM), lambda b, hh, i: (b, i, hh)),
        out_shape=jax.ShapeDtypeStruct((B, S, WIDTH), BF16),
        scratch_shapes=_attn_scratch(nh, t),
        compiler_params=_params("parallel", "parallel", "arbitrary"),
        name="moba_attn",
    )(h, q_norm.reshape(1, HEAD_DIM), cos2, sin2, kmean, k_aug, v_t)


def _moba_layer(x, norm, w_in, q_norm, k_norm, w_out, cos2, sin2):
    B, S, D = x.shape
    x2 = x.reshape(B * S, D)
    h2 = _in_proj(x2, norm, w_in.astype(BF16), F32, 1024, 1024)
    h = h2.reshape(B, S, -1)
    k_aug, v_t, kmean = _moba_prep_k(h, k_norm, cos2, sin2)
    o = _moba_attn(h, q_norm, cos2, sin2, kmean, k_aug, v_t)
    out = _out_proj(x2, h2, 3 * WIDTH, w_out.astype(BF16), [o.reshape(B * S, WIDTH)])
    return out.reshape(B, S, D)


def _split2_dot(a, u):
    hi = a.astype(BF16)
    lo = (a - hi.astype(F32)).astype(BF16)
    return jnp.dot(hi, u, preferred_element_type=F32) + jnp.dot(lo, u, preferred_element_type=F32)


def _sb_attn_body(q_ref, k_ref, v_ref, o_ref, kt_ref, carry_ref, acc_ref, *, t, nh):
    qi = pl.program_id(2)

    @pl.when(qi == 0)
    def _():
        for r in range(nh):
            _store_tiles_t(kt_ref, (r,), k_ref[0, :, r * HEAD_DIM:(r + 1) * HEAD_DIM].astype(F32), t)

    after = jnp.where(_iota((t, t), 0) > _iota((t, t), 1), 1.0, 0.0).astype(BF16)
    past = _iota((t, t), 1) < _iota((t, t), 0)
    qs = [(q_ref[0, :, r * HEAD_DIM:(r + 1) * HEAD_DIM].astype(F32) * ATTN_SCALE).astype(BF16)
          for r in range(nh)]

    def tile(j, diag):
        zs = [jnp.dot(qs[r], kt_ref[r, j], preferred_element_type=F32) for r in range(nh)]
        staged = []
        for r in range(nh):
            z = zs[r]
            drop = jnp.maximum(z, 0.0) + jnp.log(1.0 + jnp.exp2(jnp.abs(z) * (-LOG2E)))
            if diag:
                drop = jnp.where(past, drop, 0.0)
            suffix = _split2_dot(drop, after)
            staged.append((drop, suffix, suffix[:, 0:1] + drop[:, 0:1]))
        for r in range(nh):
            drop, suffix, total = staged[r]
            vs = v_ref[0, pl.ds(pl.multiple_of(j * t, t), t), r * HEAD_DIM:(r + 1) * HEAD_DIM]
            if diag:
                behind, carry = suffix, total
            else:
                carry = carry_ref[r]
                behind, carry = suffix + carry, carry + total
            a = jnp.exp(zs[r] - (drop + behind))
            if diag:
                a = jnp.where(past, a, 0.0)
            o = jnp.dot(a.astype(BF16), vs, preferred_element_type=F32)
            acc_ref[r] = o if diag else acc_ref[r] + o
            carry_ref[r] = carry
            least = jnp.min(carry) if r == 0 else jnp.minimum(least, jnp.min(carry))
        return least

    def body(c):
        return c[0] + 1, tile(qi - 1 - c[0], False)

    lax.while_loop(lambda c: (c[0] < qi) & (c[1] < SB_CUTOFF), body, (jnp.int32(0), tile(qi, True)))
    for r in range(nh):
        o_ref[0, :, r * HEAD_DIM:(r + 1) * HEAD_DIM] = acc_ref[r].astype(o_ref.dtype)


def _sb_attn(h):
    B, S, _ = h.shape
    t = ATT_TILE
    nh = HEADS_PER_STEP
    nhb = N_HEADS // nh
    w = nh * HEAD_DIM
    return pl.pallas_call(
        functools.partial(_sb_attn_body, t=t, nh=nh),
        grid=(B, nhb, S // t),
        in_specs=[pl.BlockSpec((1, t, w), lambda b, hh, i: (b, i, hh)),
                  pl.BlockSpec((1, S, w), lambda b, hh, i: (b, 0, nhb + hh)),
                  pl.BlockSpec((1, S, w), lambda b, hh, i: (b, 0, 2 * nhb + hh))],
        out_specs=pl.BlockSpec((1, t, w), lambda b, hh, i: (b, i, hh)),
        out_shape=jax.ShapeDtypeStruct((B, S, WIDTH), BF16),
        scratch_shapes=[pltpu.VMEM((nh, S // t, HEAD_DIM, t), BF16), pltpu.VMEM((nh, t, 1), F32),
                        pltpu.VMEM((nh, t, HEAD_DIM), F32)],
        compiler_params=_params("parallel", "parallel", "arbitrary"),
        name="sb_attn",
    )(h, h, h)


def _sb_layer(x, norm, w_in, w_out):
    B, S, D = x.shape
    x2 = x.reshape(B * S, D)
    h2 = _in_proj(x2, norm, w_in.astype(BF16), BF16, 1024, 1024)
    o = _sb_attn(h2.reshape(B, S, -1))
    out = _out_proj(x2, h2, 3 * WIDTH, w_out.astype(BF16), [o.reshape(B * S, WIDTH)])
    return out.reshape(B, S, D)


NSA_KV_BLK0 = N_HEADS


def _nsa_compress_body(kc_ref, vc_ref, wk_ref, wv_ref, pos_ref, kcn_ref, cosc_ref, sinc_ref,
                       kcc_ref, vcct_ref, *, n_rows):
    half = CMP_BLOCK // CMP_STRIDE
    assert half == 2

    def compress(src_ref, w_ref):
        first = jnp.zeros((n_rows, HEAD_DIM), F32)
        second = jnp.zeros((n_rows, HEAD_DIM), F32)
        for l in range(CMP_STRIDE):
            rows = src_ref[0, pl.ds(l, n_rows, stride=CMP_STRIDE), :]
            first = first + jnp.dot(rows + pos_ref[l:l + 1, :], w_ref[l],
                                    precision=HIGHEST, preferred_element_type=F32)
            second = second + jnp.dot(rows + pos_ref[CMP_STRIDE + l:CMP_STRIDE + l + 1, :],
                                      w_ref[CMP_STRIDE + l], precision=HIGHEST, preferred_element_type=F32)
        return first + pltpu.roll(second, n_rows - 1, 0)

    kc = compress(kc_ref, wk_ref)
    kcc_ref[0, 0] = _rms_rope(kc, kcn_ref[...], cosc_ref[...], sinc_ref[...])
    vcct_ref[0, 0] = compress(vc_ref, wv_ref).T.astype(vcct_ref.dtype)


def _nsa_compress(h, cmp_wk, cmp_wv, cmp_pos, kc_norm, cos_c, sin_c):
    B, S, _ = h.shape
    n_rows = S // CMP_STRIDE
    wspec = pl.BlockSpec((CMP_BLOCK, HEAD_DIM, HEAD_DIM), lambda b, g: (0, 0, 0))
    tab = pl.BlockSpec((n_rows, HEAD_DIM), lambda b, g: (0, 0))
    out = pl.BlockSpec((1, 1, n_rows, HEAD_DIM), lambda b, g: (b, g, 0, 0))
    return pl.pallas_call(
        functools.partial(_nsa_compress_body, n_rows=n_rows),
        grid=(B, NSA_KV_HEADS),
        in_specs=[pl.BlockSpec((1, S, HEAD_DIM), lambda b, g: (b, 0, NSA_KV_BLK0 + g)),
                  pl.BlockSpec((1, S, HEAD_DIM), lambda b, g: (b, 0, NSA_KV_BLK0 + NSA_KV_HEADS + g)),
                  wspec, wspec,
                  pl.BlockSpec((CMP_BLOCK, HEAD_DIM), lambda b, g: (0, 0)),
                  pl.BlockSpec((1, HEAD_DIM), lambda b, g: (0, 0)), tab, tab],
        out_specs=[out, pl.BlockSpec((1, 1, HEAD_DIM, n_rows), lambda b, g: (b, g, 0, 0))],
        out_shape=[jax.ShapeDtypeStruct((B, NSA_KV_HEADS, n_rows, HEAD_DIM), F32),
                   jax.ShapeDtypeStruct((B, NSA_KV_HEADS, HEAD_DIM, n_rows), BF16)],
        compiler_params=_params("parallel", "parallel"),
        name="nsa_compress",
    )(h, h, cmp_wk, cmp_wv, cmp_pos, kc_norm.reshape(1, HEAD_DIM), cos_c, sin_c)


def _nsa_prep_kv_body(ks_ref, vs_ref, kw_ref, vw_ref, ksn_ref, kwn_ref, cos_ref, sin_ref,
                      ksaug_ref, vst_ref, kwb_ref, vwt_ref, *, t):
    cos2, sin2 = cos_ref[...], sin_ref[...]
    S = cos2.shape[0]
    ksaug_ref[0, :, :HEAD_DIM] = _rms_rope(ks_ref[0], ksn_ref[...], cos2, sin2).astype(BF16)
    ksaug_ref[0, :, HEAD_DIM:] = _onehot_block(S, _log2(SLC_BLOCK))
    kwb_ref[0] = _rms_rope(kw_ref[0], kwn_ref[...], cos2, sin2).astype(BF16)
    _store_tiles_t(vst_ref, (0, 0), vs_ref[0], t)
    _store_tiles_t(vwt_ref, (0, 0), vw_ref[0], t)


def _nsa_prep_kv(h, ks_norm, kw_norm, cos2, sin2):
    B, S, _ = h.shape
    G = NSA_KV_HEADS
    t = ATT_TILE

    def src(i):
        return pl.BlockSpec((1, S, HEAD_DIM), lambda b, g: (b, 0, NSA_KV_BLK0 + i * G + g))

    vec = pl.BlockSpec((1, HEAD_DIM), lambda b, g: (0, 0))
    full = pl.BlockSpec((S, HEAD_DIM), lambda b, g: (0, 0))
    vt = pl.BlockSpec((1, 1, S // t, HEAD_DIM, t), lambda b, g: (b, g, 0, 0, 0))
    vt_shape = jax.ShapeDtypeStruct((B, G, S // t, HEAD_DIM, t), BF16)
    return pl.pallas_call(
        functools.partial(_nsa_prep_kv_body, t=t),
        grid=(B, G),
        in_specs=[src(2), src(3), src(4), src(5), vec, vec, full, full],
        out_specs=[pl.BlockSpec((1, S, AUG), lambda b, g: (b, 0, g)), vt,
                   pl.BlockSpec((1, S, HEAD_DIM), lambda b, g: (b, 0, g)), vt],
        out_shape=[jax.ShapeDtypeStruct((B, S, G * AUG), BF16), vt_shape,
                   jax.ShapeDtypeStruct((B, S, KV_WIDTH), BF16), vt_shape],
        compiler_params=_params("parallel", "parallel"),
        name="nsa_prep_kv",
    )(h, h, h, h, ks_norm.reshape(1, HEAD_DIM), kw_norm.reshape(1, HEAD_DIM), cos2, sin2)


def _nsa_cmp_sel_body(q_ref, qn_ref, cos_ref, sin_ref, kcc_ref, vcct_ref, bg_ref, ocmp_ref, qt_ref,
                      *, t, n_cmp, n_slc, n_sel):
    g_idx = pl.program_id(1)
    q0 = pl.program_id(2) * t
    cos2, sin2 = cos_ref[...], sin_ref[...]
    kcc = kcc_ref[0, 0]
    kcc_hi = kcc.astype(BF16)
    kcc_lo = (kcc - kcc_hi.astype(F32)).astype(BF16)
    kcc3 = jnp.concatenate([kcc_hi, kcc_hi, kcc_lo], axis=1)
    vcct = vcct_ref[0, 0]
    cmp_ok = _iota((n_cmp, t), 0) * CMP_STRIDE + (CMP_BLOCK - 1) <= q0 + _iota((n_cmp, t), 1)

    p_sum = jnp.zeros((n_cmp, t), F32)
    for r in range(NSA_HPG):
        qnt = _rms_rope(q_ref[0, :, r * HEAD_DIM:(r + 1) * HEAD_DIM], qn_ref[...], cos2, sin2).T
        qt_ref[0, r, 0, :HEAD_DIM, :] = (qnt * (ATTN_SCALE * LOG2E)).astype(BF16)
        q_hi = qnt.astype(BF16)
        q_lo = (qnt - q_hi.astype(F32)).astype(BF16)
        logits = jnp.dot(kcc3, jnp.concatenate([q_hi, q_lo, q_hi], axis=0),
                         preferred_element_type=F32) * ATTN_SCALE
        logits = jnp.where(cmp_ok, logits, NEG)
        mx = jnp.max(logits, axis=0, keepdims=True)
        e = jnp.where(cmp_ok, jnp.exp(logits - mx), 0.0)
        p = e * (1.0 / jnp.maximum(jnp.sum(e, axis=0, keepdims=True), TINY))
        p_sum = p_sum + p
        o = jnp.dot(vcct, p.astype(BF16), preferred_element_type=F32).T
        o = o * _branch_gate(bg_ref, g_idx * NSA_HPG + r)
        ocmp_ref[0, :, r * HEAD_DIM:(r + 1) * HEAD_DIM] = o.astype(ocmp_ref.dtype)

    c_start = _iota((n_slc, n_cmp), 1) * CMP_STRIDE
    s_start = _iota((n_slc, n_cmp), 0) * SLC_BLOCK
    overlap = jnp.where((c_start < s_start + SLC_BLOCK) & (c_start + CMP_BLOCK > s_start), 1.0, 0.0).astype(BF16)
    ps_hi = p_sum.astype(BF16)
    rest = p_sum - ps_hi.astype(F32)
    ps_mid = rest.astype(BF16)
    ps_lo = (rest - ps_mid.astype(F32)).astype(BF16)
    imp = jnp.dot(jnp.concatenate([overlap, overlap, overlap], axis=1),
                  jnp.concatenate([ps_hi, ps_mid, ps_lo], axis=0), preferred_element_type=F32)
    blk = _iota((n_slc, t), 0)
    cur = (q0 + _iota((n_slc, t), 1)) >> _log2(SLC_BLOCK)
    imp = jnp.where(blk == 0, FORCE, imp)
    imp = jnp.where(blk == cur, FORCE, imp)
    imp = jnp.where(blk == cur - 1, FORCE, imp)
    imp = jnp.where(blk <= cur, imp, NEG)
    sub = 8
    ranks = []
    for g0 in range(0, n_slc, sub):
        mine = imp[g0:g0 + sub, :]
        row = _iota((sub, t), 0) + g0
        rank = jnp.zeros((sub, t), F32)
        for k in range(n_slc):
            c = imp[k:k + 1, :]
            ge = jnp.where(c >= mine, 1.0, 0.0)
            gt = jnp.where(c > mine, 1.0, 0.0)
            rank = rank + (ge if k < g0 else gt if k >= g0 + sub else jnp.where(row > k, ge, gt))
        ranks.append(rank)
    bias = jnp.where(jnp.concatenate(ranks, axis=0) < float(n_sel), 0.0, NEG)
    bias = jnp.concatenate([bias, jnp.zeros((HEAD_DIM - n_slc, t), F32)], axis=0).astype(BF16)
    for r in range(NSA_HPG):
        qt_ref[0, r, 0, HEAD_DIM:, :] = bias


def _nsa_cmp_sel(h, bg, q_norm, cos2, sin2, kc_c, vc_ct):
    B, S, _ = h.shape
    t = ATT_TILE
    G, R = NSA_KV_HEADS, NSA_HPG
    n_cmp = S // CMP_STRIDE
    n_slc = S // SLC_BLOCK
    assert n_slc <= HEAD_DIM and n_slc % 8 == 0
    tab = pl.BlockSpec((t, HEAD_DIM), lambda b, g, i: (i, 0))
    return pl.pallas_call(
        functools.partial(_nsa_cmp_sel_body, t=t, n_cmp=n_cmp, n_slc=n_slc, n_sel=min(SLC_TOPK, n_slc)),
        grid=(B, G, S // t),
        in_specs=[pl.BlockSpec((1, t, R * HEAD_DIM), lambda b, g, i: (b, i, g)),
                  pl.BlockSpec((1, HEAD_DIM), lambda b, g, i: (0, 0)), tab, tab,
                  pl.BlockSpec((1, 1, n_cmp, HEAD_DIM), lambda b, g, i: (b, g, 0, 0)),
                  pl.BlockSpec((1, 1, HEAD_DIM, n_cmp), lambda b, g, i: (b, g, 0, 0)),
                  pl.BlockSpec((1, t, LANES), lambda b, g, i: (b, i, 0))],
        out_specs=[pl.BlockSpec((1, t, R * HEAD_DIM), lambda b, g, i: (b, i, g)),
                   pl.BlockSpec((1, R, 1, AUG, t), lambda b, g, i: (b, g, i, 0, 0))],
        out_shape=[jax.ShapeDtypeStruct((B, S, WIDTH), BF16),
                   jax.ShapeDtypeStruct((B, N_HEADS, S // t, AUG, t), BF16)],
        compiler_params=_params("parallel", "parallel", "arbitrary"),
        name="nsa_cmp_sel",
    )(h, q_norm.reshape(1, HEAD_DIM), cos2, sin2, kc_c, vc_ct, bg)


def _win_attn_body(qt_ref, k_ref, vt_ref, bg_ref, o_ref, *, t, nh, n_back, gate_base):
    qi = pl.program_id(2)
    n_blk = n_back + 1
    j0 = jnp.maximum(qi - n_back, 0)
    rows = pl.ds(pl.multiple_of(j0 * t, t), n_blk * t)
    dist = _iota((n_blk * t, t), 1) - _iota((n_blk * t, t), 0) + (qi - j0) * t
    ss = [jnp.dot(k_ref[0, rows, (r // NSA_HPG) * HEAD_DIM:(r // NSA_HPG + 1) * HEAD_DIM], qt_ref[0, r, 0],
                  preferred_element_type=F32) for r in range(nh)]
    states = []
    for r in range(nh):
        s = jnp.where(dist >= 0, ss[r], NEG)
        s = jnp.where(dist < WINDOW, s, NEG)
        m = jnp.max(s, axis=0, keepdims=True)
        p = jnp.exp2(s - m)
        vt = jnp.concatenate([vt_ref[0, r // NSA_HPG, j0 + d] for d in range(n_blk)], axis=1)
        states.append((jnp.sum(p, axis=0, keepdims=True), jnp.dot(vt, p.astype(BF16), preferred_element_type=F32)))
    _store_heads(o_ref, states, bg_ref, gate_base + pl.program_id(1) * nh)


def _win_attn(q_t, kw, vw_t, bg, gate_base):
    B, H, n_t, _, t = q_t.shape
    S = n_t * t
    nk = NSA_KV_PER_STEP
    nh = nk * NSA_HPG
    assert WINDOW % t == 0 and n_t > WINDOW // t
    return pl.pallas_call(
        functools.partial(_win_attn_body, t=t, nh=nh, n_back=WINDOW // t, gate_base=gate_base),
        grid=(B, H // nh, n_t),
        in_specs=[pl.BlockSpec((1, nh, 1, HEAD_DIM, t), lambda b, g, i: (b, g, i, 0, 0)),
                  pl.BlockSpec((1, S, nk * HEAD_DIM), lambda b, g, i: (b, 0, g)),
                  pl.BlockSpec((1, nk, n_t, HEAD_DIM, t), lambda b, g, i: (b, g, 0, 0, 0)),
                  pl.BlockSpec((1, t, LANES), lambda b, g, i: (b, i, 0))],
        out_specs=pl.BlockSpec((1, t, nh * HEAD_DIM), lambda b, g, i: (b, i, g)),
        out_shape=jax.ShapeDtypeStruct((B, S, WIDTH), BF16),
        compiler_params=_params("parallel", "parallel", "arbitrary"),
        name="win_attn",
    )(q_t, kw, vw_t, bg)


def _nsa_layer(x, norm, w_in, q_norm, kc_norm, ks_norm, kw_norm, cmp_wk, cmp_wv, cmp_pos, w_out, cos2, sin2):
    B, S, D = x.shape
    x2 = x.reshape(B * S, D)
    gate_col = WIDTH + 6 * KV_WIDTH
    gate_end = gate_col + WIDTH
    w_bg = jnp.pad(w_in[:, gate_end:], ((0, 0), (0, LANES - 3 * N_HEADS)))
    h2 = _in_proj(x2, norm, w_in[:, :gate_end].astype(BF16), F32, 1024, 1024)
    bg = _in_proj(x2, norm, w_bg.astype(BF16), F32, 512, LANES).reshape(B, S, LANES)
    h = h2.reshape(B, S, -1)
    cos_c, sin_c = _rope_tables(jnp.arange(S // CMP_STRIDE) * CMP_STRIDE + (CMP_BLOCK - 1))
    kc_c, vc_ct = _nsa_compress(h, cmp_wk, cmp_wv, cmp_pos, kc_norm, cos_c, sin_c)
    ks_aug, vs_t, kw_b, vw_t = _nsa_prep_kv(h, ks_norm, kw_norm, cos2, sin2)
    o_cmp, q_t = _nsa_cmp_sel(h, bg, q_norm, cos2, sin2, kc_c, vc_ct)
    o_slc = _sel_attn(q_t, ks_aug, vs_t, bg, N_HEADS)
    o_win = _win_attn(q_t, kw_b, vw_t, bg, 2 * N_HEADS)
    os = [o.reshape(B * S, WIDTH) for o in (o_cmp, o_slc, o_win)]
    out = _out_proj(x2, h2, gate_col, w_out.astype(BF16), os)
    return out.reshape(B, S, D)


def kernel(x, l0_norm, l0_w_in, l0_q_norm, l0_k_norm, l0_w_out, l1_norm, l1_w_in, l1_w_out, l2_norm, l2_w_in, l2_q_norm, l2_kc_norm, l2_ks_norm, l2_kw_norm, l2_cmp_wk, l2_cmp_wv, l2_cmp_pos, l2_w_out, l3_norm, l3_w_in, l3_q_norm, l3_k_norm, l3_w_out):
    cos2, sin2 = _rope_tables(jnp.arange(x.shape[1]))
    x = _moba_layer(x, l0_norm, l0_w_in, l0_q_norm, l0_k_norm, l0_w_out, cos2, sin2)
    x = _sb_layer(x, l1_norm, l1_w_in, l1_w_out)
    x = _nsa_layer(x, l2_norm, l2_w_in, l2_q_norm, l2_kc_norm, l2_ks_norm, l2_kw_norm,
                   l2_cmp_wk, l2_cmp_wv, l2_cmp_pos, l2_w_out, cos2, sin2)
    x = _moba_layer(x, l3_norm, l3_w_in, l3_q_norm, l3_k_norm, l3_w_out, cos2, sin2)
    return x
```

```python
import functools
import math

import jax
import jax.numpy as jnp
from jax import lax
from jax.experimental import pallas as pl
from jax.experimental.pallas import tpu as pltpu

F32 = jnp.float32
BF16 = jnp.bfloat16
HIGHEST = lax.Precision.HIGHEST

HEAD_DIM = 128
N_HEADS = 16
WIDTH = N_HEADS * HEAD_DIM
ROPE_THETA = 10000.0
EPS = 1e-6
ATTN_SCALE = HEAD_DIM ** -0.5
LOG2E = math.log2(math.e)
NEG = -1e30
LOWEST = -3e38
FORCE = 1e30
TINY = 1e-30
MOBA_BLOCK = 256
MOBA_TOPK = 3
NSA_KV_HEADS = 4
NSA_HPG = N_HEADS // NSA_KV_HEADS
KV_WIDTH = NSA_KV_HEADS * HEAD_DIM
CMP_BLOCK = 32
CMP_STRIDE = 16
SLC_BLOCK = 64
SLC_TOPK = 16
WINDOW = 512

LANES = 128
AUG = 2 * HEAD_DIM
ATT_TILE = 256
HEADS_PER_STEP = 4
MOBA_HEADS_PER_STEP = 8
SB_CUTOFF = 110.0
NSA_KV_PER_STEP = 2
VMEM_LIMIT = 56 * 1024 * 1024


def _params(*sem):
    return pltpu.CompilerParams(dimension_semantics=sem, vmem_limit_bytes=VMEM_LIMIT)


def _rope_tables(pos):
    half = HEAD_DIM // 2
    inv_freq = jnp.exp(-math.log(ROPE_THETA) * jnp.arange(half, dtype=F32) / half)
    ang = pos.astype(F32)[:, None] * inv_freq[None, :]
    cos, sin = jnp.cos(ang), jnp.sin(ang)
    return jnp.concatenate([cos, cos], axis=-1), jnp.concatenate([-sin, sin], axis=-1)


def _rms_rope(x, g, cos2, sin2):
    y = x * lax.rsqrt(jnp.mean(x * x, axis=-1, keepdims=True) + EPS) * g
    return y * cos2 + pltpu.roll(y, HEAD_DIM // 2, 1) * sin2


def _iota(shape, dim):
    return lax.broadcasted_iota(jnp.int32, shape, dim)


def _log2(n):
    s = n.bit_length() - 1
    assert 1 << s == n
    return s


def _store_tiles_t(dst_ref, lead, x, t):
    for j in range(x.shape[0] // t):
        dst_ref[lead + (j,)] = x[j * t:(j + 1) * t, :].T.astype(dst_ref.dtype)


def _in_proj_body(x_ref, g_ref, w_ref, o_ref, xn_ref):
    @pl.when(pl.program_id(1) == 0)
    def _():
        x = x_ref[...]
        r = lax.rsqrt(jnp.mean(x * x, axis=-1, keepdims=True) + EPS)
        xn_ref[...] = (x * r * g_ref[...]).astype(xn_ref.dtype)

    o_ref[...] = jnp.dot(xn_ref[...], w_ref[...], preferred_element_type=F32).astype(o_ref.dtype)


def _in_proj(x2, g, w, out_dtype, tm, tn):
    M, K = x2.shape
    N = w.shape[1]
    return pl.pallas_call(
        _in_proj_body,
        grid=(M // tm, N // tn),
        in_specs=[pl.BlockSpec((tm, K), lambda i, j: (i, 0)),
                  pl.BlockSpec((1, K), lambda i, j: (0, 0)),
                  pl.BlockSpec((K, tn), lambda i, j: (0, j))],
        out_specs=pl.BlockSpec((tm, tn), lambda i, j: (i, j)),
        out_shape=jax.ShapeDtypeStruct((M, N), out_dtype),
        scratch_shapes=[pltpu.VMEM((tm, K), BF16)],
        compiler_params=_params("parallel", "arbitrary"),
        name="in_proj",
    )(x2, g.reshape(1, K), w)


def _out_proj_body(*refs, n_o):
    x_ref, gate_lo_ref, gate_hi_ref, w_ref = refs[:4]
    o_refs = refs[4:4 + n_o]
    out_ref = refs[4 + n_o]
    o = o_refs[0][...].astype(F32)
    for r in o_refs[1:]:
        o = o + r[...].astype(F32)
    gate = jnp.concatenate([gate_lo_ref[...], gate_hi_ref[...]], axis=1).astype(F32)
    y = o * (gate / (1.0 + jnp.exp(-gate)))
    out_ref[...] = x_ref[...] + jnp.dot(y.astype(BF16), w_ref[...], preferred_element_type=F32)


def _out_proj(x2, h2, gate_col, w, os, tm=256):
    M, D = x2.shape
    n_o = len(os)
    half = D // 2
    assert gate_col % half == 0
    row = pl.BlockSpec((tm, D), lambda i: (i, 0))
    return pl.pallas_call(
        functools.partial(_out_proj_body, n_o=n_o),
        grid=(M // tm,),
        in_specs=[row, pl.BlockSpec((tm, half), lambda i: (i, gate_col // half)),
                  pl.BlockSpec((tm, half), lambda i: (i, gate_col // half + 1)),
                  pl.BlockSpec((D, D), lambda i: (0, 0))] + [row] * n_o,
        out_specs=row,
        out_shape=jax.ShapeDtypeStruct((M, D), F32),
        compiler_params=_params("parallel"),
        name="out_proj",
    )(x2, h2, h2, w, *os)


def _branch_gate(bg_ref, col):
    bg = bg_ref[0]
    g = 1.0 / (1.0 + jnp.exp(-bg))
    return jnp.sum(jnp.where(_iota(bg.shape, 1) == col, g, 0.0), axis=1, keepdims=True)


def _softmax_block(ml_ref, acc_ref, r, s, vt, first):
    if first:
        m_new = jnp.max(s, axis=0, keepdims=True)
        p = jnp.exp2(s - m_new)
        l = jnp.sum(p, axis=0, keepdims=True)
        acc_ref[r] = jnp.dot(vt, p.astype(BF16), preferred_element_type=F32)
    else:
        m = ml_ref[2 * r:2 * r + 1, :]
        m_new = jnp.maximum(m, jnp.max(s, axis=0, keepdims=True))
        alpha = jnp.exp2(m - m_new)
        p = jnp.exp2(s - m_new)
        l = alpha * ml_ref[2 * r + 1:2 * r + 2, :] + jnp.sum(p, axis=0, keepdims=True)
        acc_ref[r] = alpha * acc_ref[r] + jnp.dot(vt, p.astype(BF16), preferred_element_type=F32)
    ml_ref[2 * r:2 * r + 1, :] = m_new
    ml_ref[2 * r + 1:2 * r + 2, :] = l


def _store_heads(o_ref, sums_accs, bg_ref, gate_col0):
    for r, (l, acc) in enumerate(sums_accs):
        o = (acc * (1.0 / l)).T
        if bg_ref is not None:
            o = o * _branch_gate(bg_ref, gate_col0 + r)
        o_ref[0, :, r * HEAD_DIM:(r + 1) * HEAD_DIM] = o.astype(o_ref.dtype)


def _attend_causal(qts, k_ref, hot_ref, vt_ref, ml_ref, acc_ref, qi, t, hpg):
    nh = len(qts)
    w = 2 * t

    def block(j0, diag):
        rows = pl.ds(pl.multiple_of(j0 * t, w), w)
        hot = hot_ref[rows, :]
        ss = [jnp.dot(jnp.concatenate([k_ref[0, rows, (r // hpg) * HEAD_DIM:(r // hpg + 1) * HEAD_DIM], hot], axis=1),
                      qts[r], preferred_element_type=F32) for r in range(nh)]
        if diag:
            causal = _iota((w, w), 0) <= _iota((w, w), 1)
        for r in range(nh):
            s = jnp.where(causal, ss[r], NEG) if diag else ss[r]
            vt = jnp.concatenate([vt_ref[0, r // hpg, j0], vt_ref[0, r // hpg, j0 + 1]], axis=1)
            _softmax_block(ml_ref, acc_ref, r, s, vt, diag)

    block(2 * qi, True)

    def body(j2, carry):
        block(2 * j2, False)
        return carry

    lax.fori_loop(0, qi, body, 0)
    return [(ml_ref[2 * r + 1:2 * r + 2, :], acc_ref[r]) for r in range(nh)]


def _attn_scratch(nh, t):
    return [pltpu.VMEM((2 * nh, 2 * t), F32), pltpu.VMEM((nh, HEAD_DIM, 2 * t), F32)]


def _sel_attn_body(qt_ref, k_ref, hot_ref, vt_ref, bg_ref, o_ref, ml_ref, acc_ref, *, t, nh, gate_base):
    qts = [jnp.concatenate([qt_ref[0, r, 0], qt_ref[0, r, 1]], axis=1) for r in range(nh)]
    sums_accs = _attend_causal(qts, k_ref, hot_ref, vt_ref, ml_ref, acc_ref, pl.program_id(2), t, NSA_HPG)
    _store_heads(o_ref, sums_accs, bg_ref, gate_base + pl.program_id(1) * nh)


def _sel_attn(q_t, k, hot, v_t, bg, gate_base):
    B, H, n_t, _, t = q_t.shape
    S = n_t * t
    nk = NSA_KV_PER_STEP
    nh = nk * NSA_HPG
    assert n_t % 2 == 0
    return pl.pallas_call(
        functools.partial(_sel_attn_body, t=t, nh=nh, gate_base=gate_base),
        grid=(B, H // nh, n_t // 2),
        in_specs=[pl.BlockSpec((1, nh, 2, AUG, t), lambda b, h, i: (b, h, i, 0, 0)),
                  pl.BlockSpec((1, S, nk * HEAD_DIM), lambda b, h, i: (b, 0, h)),
                  pl.BlockSpec((S, LANES), lambda b, h, i: (0, 0)),
                  pl.BlockSpec((1, nk, n_t, HEAD_DIM, t), lambda b, h, i: (b, h, 0, 0, 0)),
                  pl.BlockSpec((1, 2 * t, LANES), lambda b, h, i: (b, i, 0))],
        out_specs=pl.BlockSpec((1, 2 * t, nh * HEAD_DIM), lambda b, h, i: (b, i, h)),
        out_shape=jax.ShapeDtypeStruct((B, S, WIDTH), BF16),
        scratch_shapes=_attn_scratch(nh, t),
        compiler_params=_params("parallel", "parallel", "arbitrary"),
        name="sel_attn",
    )(q_t, k, hot, v_t, bg)


def _onehot_block(S, shift):
    return jnp.where((_iota((S, LANES), 0) >> shift) == _iota((S, LANES), 1), 1.0, 0.0).astype(BF16)


def _moba_prep_k_body(k_ref, v_ref, kn_ref, cos_ref, sin_ref, kb_ref, vt_ref, kmean_ref, *, nb, t):
    kn = _rms_rope(k_ref[0], kn_ref[...], cos_ref[...], sin_ref[...])
    kb_ref[0] = kn.astype(BF16)
    _store_tiles_t(vt_ref, (0, 0), v_ref[0], t)
    kmean_ref[0, 0] = jnp.mean(kn.reshape(nb, MOBA_BLOCK, HEAD_DIM), axis=1)


def _moba_prep_k(h, k_norm, cos2, sin2):
    B, S, _ = h.shape
    nb = S // MOBA_BLOCK
    t = ATT_TILE
    assert 1 < nb <= HEAD_DIM
    full = pl.BlockSpec((S, HEAD_DIM), lambda b, hh: (0, 0))
    return pl.pallas_call(
        functools.partial(_moba_prep_k_body, nb=nb, t=t),
        grid=(B, N_HEADS),
        in_specs=[pl.BlockSpec((1, S, HEAD_DIM), lambda b, hh: (b, 0, N_HEADS + hh)),
                  pl.BlockSpec((1, S, HEAD_DIM), lambda b, hh: (b, 0, 2 * N_HEADS + hh)),
                  pl.BlockSpec((1, HEAD_DIM), lambda b, hh: (0, 0)), full, full],
        out_specs=[pl.BlockSpec((1, S, HEAD_DIM), lambda b, hh: (b, 0, hh)),
                   pl.BlockSpec((1, 1, S // t, HEAD_DIM, t), lambda b, hh: (b, hh, 0, 0, 0)),
                   pl.BlockSpec((1, 1, nb, HEAD_DIM), lambda b, hh: (b, hh, 0, 0))],
        out_shape=[jax.ShapeDtypeStruct((B, S, WIDTH), BF16),
                   jax.ShapeDtypeStruct((B, N_HEADS, S // t, HEAD_DIM, t), BF16),
                   jax.ShapeDtypeStruct((B, N_HEADS, nb, HEAD_DIM), F32)],
        compiler_params=_params("parallel", "parallel"),
        name="moba_prep_k",
    )(h, h, k_norm.reshape(1, HEAD_DIM), cos2, sin2)


def _moba_attn_body(q_ref, qn_ref, cos_ref, sin_ref, kmean_ref, k_ref, hot_ref, vt_ref, o_ref, ml_ref, acc_ref,
                    *, t, nh, nb, n_sel):
    qi = pl.program_id(2)
    cos2, sin2 = cos_ref[...], sin_ref[...]
    w = 2 * t
    blk = _iota((nb, w), 0)
    blk_f = blk.astype(F32)
    cur = 2 * qi + (_iota((nb, w), 1) >> _log2(t))
    qts = []
    for r in range(nh):
        qn = _rms_rope(q_ref[0, :, r * HEAD_DIM:(r + 1) * HEAD_DIM], qn_ref[...], cos2, sin2)
        qnt = qn.T
        gate = jnp.dot(kmean_ref[0, r], qnt, precision=HIGHEST, preferred_element_type=F32)
        g = jnp.where(blk < cur, gate, NEG)
        picked = jnp.zeros((nb, w), F32)
        for _ in range(n_sel):
            mx = jnp.max(g, axis=0, keepdims=True)
            first = jnp.min(jnp.where(g == mx, blk_f, float(nb)), axis=0, keepdims=True)
            hit = blk_f == first
            picked = jnp.where(hit, 1.0, picked)
            g = jnp.where(hit, LOWEST, g)
        attend = jnp.where(blk < cur, picked, jnp.where(blk == cur, 1.0, 0.0))
        bias = jnp.where(attend > 0.0, 0.0, NEG)
        qts.append(jnp.concatenate([qnt * (ATTN_SCALE * LOG2E), bias, jnp.zeros((HEAD_DIM - nb, w), F32)],
                                   axis=0).astype(BF16))
    _store_heads(o_ref, _attend_causal(qts, k_ref, hot_ref, vt_ref, ml_ref, acc_ref, qi, t, 1), None, None)


def _moba_attn(h, q_norm, cos2, sin2, kmean, k, v_t):
    B, S, _ = h.shape
    t = MOBA_BLOCK
    assert t == ATT_TILE
    nb = S // MOBA_BLOCK
    assert nb % 2 == 0
    nh = MOBA_HEADS_PER_STEP
    n_sel = min(MOBA_TOPK, nb - 1)
    tab = pl.BlockSpec((2 * t, HEAD_DIM), lambda b, hh, i: (i, 0))
    return pl.pallas_call(
        functools.partial(_moba_attn_body, t=t, nh=nh, nb=nb, n_sel=n_sel),
        grid=(B, N_HEADS // nh, nb // 2),
        in_specs=[pl.BlockSpec((1, 2 * t, nh * HEAD_DIM), lambda b, hh, i: (b, i, hh)),
                  pl.BlockSpec((1, HEAD_DIM), lambda b, hh, i: (0, 0)), tab, tab,
                  pl.BlockSpec((1, nh, nb, HEAD_DIM), lambda b, hh, i: (b, hh, 0, 0)),
                  pl.BlockSpec((1, S, nh * HEAD_DIM), lambda b, hh, i: (b, 0, hh)),
                  pl.BlockSpec((S, LANES), lambda b, hh, i: (0, 0)),
                  pl.BlockSpec((1, nh, nb, HEAD_DIM, t), lambda b, hh, i: (b, hh, 0, 0, 0))],
        out_specs=pl.BlockSpec((1, 2 * t, nh * HEAD_DIM), lambda b, hh, i: (b, i, hh)),
        out_shape=jax.ShapeDtypeStruct((B, S, WIDTH), BF16),
        scratch_shapes=_attn_scratch(nh, t),
        compiler_params=_params("parallel", "parallel", "arbitrary"),
        name="moba_attn",
    )(h, q_norm.reshape(1, HEAD_DIM), cos2, sin2, kmean, k, _onehot_block(S, _log2(MOBA_BLOCK)), v_t)


def _moba_layer(x, norm, w_in, q_norm, k_norm, w_out, cos2, sin2):
    B, S, D = x.shape
    x2 = x.reshape(B * S, D)
    h2 = _in_proj(x2, norm, w_in.astype(BF16), F32, 1024, 1024)
    h = h2.reshape(B, S, -1)
    k, v_t, kmean = _moba_prep_k(h, k_norm, cos2, sin2)
    o = _moba_attn(h, q_norm, cos2, sin2, kmean, k, v_t)
    out = _out_proj(x2, h2, 3 * WIDTH, w_out.astype(BF16), [o.reshape(B * S, WIDTH)])
    return out.reshape(B, S, D)


def _split2_dot(a, u):
    hi = a.astype(BF16)
    lo = (a - hi.astype(F32)).astype(BF16)
    return jnp.dot(hi, u, preferred_element_type=F32) + jnp.dot(lo, u, preferred_element_type=F32)


def _sb_attn_body(q_ref, k_ref, v_ref, o_ref, kt_ref, carry_ref, acc_ref, *, t, nh):
    qi = pl.program_id(2)

    @pl.when(qi == 0)
    def _():
        for r in range(nh):
            _store_tiles_t(kt_ref, (r,), k_ref[0, :, r * HEAD_DIM:(r + 1) * HEAD_DIM].astype(F32), t)

    after = jnp.where(_iota((t, t), 0) > _iota((t, t), 1), 1.0, 0.0).astype(BF16)
    past = _iota((t, t), 1) < _iota((t, t), 0)
    qs = [(q_ref[0, :, r * HEAD_DIM:(r + 1) * HEAD_DIM].astype(F32) * ATTN_SCALE).astype(BF16)
          for r in range(nh)]

    def tile(j, diag):
        zs = [jnp.dot(qs[r], kt_ref[r, j], preferred_element_type=F32) for r in range(nh)]
        staged = []
        for r in range(nh):
            z = zs[r]
            drop = jnp.maximum(z, 0.0) + jnp.log(1.0 + jnp.exp2(jnp.abs(z) * (-LOG2E)))
            if diag:
                drop = jnp.where(past, drop, 0.0)
            suffix = _split2_dot(drop, after)
            staged.append((drop, suffix, suffix[:, 0:1] + drop[:, 0:1]))
        for r in range(nh):
            drop, suffix, total = staged[r]
            vs = v_ref[0, pl.ds(pl.multiple_of(j * t, t), t), r * HEAD_DIM:(r + 1) * HEAD_DIM]
            if diag:
                behind, carry = suffix, total
            else:
                carry = carry_ref[r]
                behind, carry = suffix + carry, carry + total
            a = jnp.exp(zs[r] - (drop + behind))
            if diag:
                a = jnp.where(past, a, 0.0)
            o = jnp.dot(a.astype(BF16), vs, preferred_element_type=F32)
            acc_ref[r] = o if diag else acc_ref[r] + o
            carry_ref[r] = carry
            least = jnp.min(carry) if r == 0 else jnp.minimum(least, jnp.min(carry))
        return least

    def body(c):
        return c[0] + 1, tile(qi - 1 - c[0], False)

    lax.while_loop(lambda c: (c[0] < qi) & (c[1] < SB_CUTOFF), body, (jnp.int32(0), tile(qi, True)))
    for r in range(nh):
        o_ref[0, :, r * HEAD_DIM:(r + 1) * HEAD_DIM] = acc_ref[r].astype(o_ref.dtype)


def _sb_attn(h):
    B, S, _ = h.shape
    t = ATT_TILE
    nh = HEADS_PER_STEP
    nhb = N_HEADS // nh
    w = nh * HEAD_DIM
    return pl.pallas_call(
        functools.partial(_sb_attn_body, t=t, nh=nh),
        grid=(B, nhb, S // t),
        in_specs=[pl.BlockSpec((1, t, w), lambda b, hh, i: (b, i, hh)),
                  pl.BlockSpec((1, S, w), lambda b, hh, i: (b, 0, nhb + hh)),
                  pl.BlockSpec((1, S, w), lambda b, hh, i: (b, 0, 2 * nhb + hh))],
        out_specs=pl.BlockSpec((1, t, w), lambda b, hh, i: (b, i, hh)),
        out_shape=jax.ShapeDtypeStruct((B, S, WIDTH), BF16),
        scratch_shapes=[pltpu.VMEM((nh, S // t, HEAD_DIM, t), BF16), pltpu.VMEM((nh, t, 1), F32),
                        pltpu.VMEM((nh, t, HEAD_DIM), F32)],
        compiler_params=_params("parallel", "parallel", "arbitrary"),
        name="sb_attn",
    )(h, h, h)


def _sb_layer(x, norm, w_in, w_out):
    B, S, D = x.shape
    x2 = x.reshape(B * S, D)
    h2 = _in_proj(x2, norm, w_in.astype(BF16), BF16, 1024, 1024)
    o = _sb_attn(h2.reshape(B, S, -1))
    out = _out_proj(x2, h2, 3 * WIDTH, w_out.astype(BF16), [o.reshape(B * S, WIDTH)])
    return out.reshape(B, S, D)


NSA_KV_BLK0 = N_HEADS


def _nsa_compress_body(kc_ref, vc_ref, wk_ref, wv_ref, pos_ref, kcn_ref, cosc_ref, sinc_ref,
                       kcc_ref, vcct_ref, *, n_rows):
    half = CMP_BLOCK // CMP_STRIDE
    assert half == 2

    def compress(src_ref, w_ref):
        first = jnp.zeros((n_rows, HEAD_DIM), F32)
        second = jnp.zeros((n_rows, HEAD_DIM), F32)
        for l in range(CMP_STRIDE):
            rows = src_ref[0, pl.ds(l, n_rows, stride=CMP_STRIDE), :]
            first = first + jnp.dot(rows + pos_ref[l:l + 1, :], w_ref[l],
                                    precision=HIGHEST, preferred_element_type=F32)
            second = second + jnp.dot(rows + pos_ref[CMP_STRIDE + l:CMP_STRIDE + l + 1, :],
                                      w_ref[CMP_STRIDE + l], precision=HIGHEST, preferred_element_type=F32)
        return first + pltpu.roll(second, n_rows - 1, 0)

    kc = compress(kc_ref, wk_ref)
    kcc_ref[0, 0] = _rms_rope(kc, kcn_ref[...], cosc_ref[...], sinc_ref[...])
    vcct_ref[0, 0] = compress(vc_ref, wv_ref).T.astype(vcct_ref.dtype)


def _nsa_compress(h, cmp_wk, cmp_wv, cmp_pos, kc_norm, cos_c, sin_c):
    B, S, _ = h.shape
    n_rows = S // CMP_STRIDE
    wspec = pl.BlockSpec((CMP_BLOCK, HEAD_DIM, HEAD_DIM), lambda b, g: (0, 0, 0))
    tab = pl.BlockSpec((n_rows, HEAD_DIM), lambda b, g: (0, 0))
    out = pl.BlockSpec((1, 1, n_rows, HEAD_DIM), lambda b, g: (b, g, 0, 0))
    return pl.pallas_call(
        functools.partial(_nsa_compress_body, n_rows=n_rows),
        grid=(B, NSA_KV_HEADS),
        in_specs=[pl.BlockSpec((1, S, HEAD_DIM), lambda b, g: (b, 0, NSA_KV_BLK0 + g)),
                  pl.BlockSpec((1, S, HEAD_DIM), lambda b, g: (b, 0, NSA_KV_BLK0 + NSA_KV_HEADS + g)),
                  wspec, wspec,
                  pl.BlockSpec((CMP_BLOCK, HEAD_DIM), lambda b, g: (0, 0)),
                  pl.BlockSpec((1, HEAD_DIM), lambda b, g: (0, 0)), tab, tab],
        out_specs=[out, pl.BlockSpec((1, 1, HEAD_DIM, n_rows), lambda b, g: (b, g, 0, 0))],
        out_shape=[jax.ShapeDtypeStruct((B, NSA_KV_HEADS, n_rows, HEAD_DIM), F32),
                   jax.ShapeDtypeStruct((B, NSA_KV_HEADS, HEAD_DIM, n_rows), BF16)],
        compiler_params=_params("parallel", "parallel"),
        name="nsa_compress",
    )(h, h, cmp_wk, cmp_wv, cmp_pos, kc_norm.reshape(1, HEAD_DIM), cos_c, sin_c)


def _nsa_prep_kv_body(ks_ref, vs_ref, kw_ref, vw_ref, ksn_ref, kwn_ref, cos_ref, sin_ref,
                      ksb_ref, vst_ref, kwb_ref, vwt_ref, *, t):
    cos2, sin2 = cos_ref[...], sin_ref[...]
    ksb_ref[0] = _rms_rope(ks_ref[0], ksn_ref[...], cos2, sin2).astype(BF16)
    kwb_ref[0] = _rms_rope(kw_ref[0], kwn_ref[...], cos2, sin2).astype(BF16)
    _store_tiles_t(vst_ref, (0, 0), vs_ref[0], t)
    _store_tiles_t(vwt_ref, (0, 0), vw_ref[0], t)


def _nsa_prep_kv(h, ks_norm, kw_norm, cos2, sin2):
    B, S, _ = h.shape
    G = NSA_KV_HEADS
    t = ATT_TILE

    def src(i):
        return pl.BlockSpec((1, S, HEAD_DIM), lambda b, g: (b, 0, NSA_KV_BLK0 + i * G + g))

    vec = pl.BlockSpec((1, HEAD_DIM), lambda b, g: (0, 0))
    full = pl.BlockSpec((S, HEAD_DIM), lambda b, g: (0, 0))
    vt = pl.BlockSpec((1, 1, S // t, HEAD_DIM, t), lambda b, g: (b, g, 0, 0, 0))
    vt_shape = jax.ShapeDtypeStruct((B, G, S // t, HEAD_DIM, t), BF16)
    return pl.pallas_call(
        functools.partial(_nsa_prep_kv_body, t=t),
        grid=(B, G),
        in_specs=[src(2), src(3), src(4), src(5), vec, vec, full, full],
        out_specs=[pl.BlockSpec((1, S, HEAD_DIM), lambda b, g: (b, 0, g)), vt,
                   pl.BlockSpec((1, S, HEAD_DIM), lambda b, g: (b, 0, g)), vt],
        out_shape=[jax.ShapeDtypeStruct((B, S, KV_WIDTH), BF16), vt_shape,
                   jax.ShapeDtypeStruct((B, S, KV_WIDTH), BF16), vt_shape],
        compiler_params=_params("parallel", "parallel"),
        name="nsa_prep_kv",
    )(h, h, h, h, ks_norm.reshape(1, HEAD_DIM), kw_norm.reshape(1, HEAD_DIM), cos2, sin2)


def _nsa_cmp_sel_body(q_ref, qn_ref, cos_ref, sin_ref, kcc_ref, vcct_ref, bg_ref, ocmp_ref, qt_ref,
                      *, t, n_cmp, n_slc, n_sel):
    g_idx = pl.program_id(1)
    q0 = pl.program_id(2) * t
    cos2, sin2 = cos_ref[...], sin_ref[...]
    kcc = kcc_ref[0, 0]
    kcc_hi = kcc.astype(BF16)
    kcc_lo = (kcc - kcc_hi.astype(F32)).astype(BF16)
    kcc3 = jnp.concatenate([kcc_hi, kcc_hi, kcc_lo], axis=1)
    vcct = vcct_ref[0, 0]
    cmp_ok = _iota((n_cmp, t), 0) * CMP_STRIDE + (CMP_BLOCK - 1) <= q0 + _iota((n_cmp, t), 1)

    p_sum = jnp.zeros((n_cmp, t), F32)
    for r in range(NSA_HPG):
        qnt = _rms_rope(q_ref[0, :, r * HEAD_DIM:(r + 1) * HEAD_DIM], qn_ref[...], cos2, sin2).T
        qt_ref[0, r, 0, :HEAD_DIM, :] = (qnt * (ATTN_SCALE * LOG2E)).astype(BF16)
        q_hi = qnt.astype(BF16)
        q_lo = (qnt - q_hi.astype(F32)).astype(BF16)
        logits = jnp.dot(kcc3, jnp.concatenate([q_hi, q_lo, q_hi], axis=0),
                         preferred_element_type=F32) * ATTN_SCALE
        logits = jnp.where(cmp_ok, logits, NEG)
        mx = jnp.max(logits, axis=0, keepdims=True)
        e = jnp.where(cmp_ok, jnp.exp(logits - mx), 0.0)
        p = e * (1.0 / jnp.maximum(jnp.sum(e, axis=0, keepdims=True), TINY))
        p_sum = p_sum + p
        o = jnp.dot(vcct, p.astype(BF16), preferred_element_type=F32).T
        o = o * _branch_gate(bg_ref, g_idx * NSA_HPG + r)
        ocmp_ref[0, :, r * HEAD_DIM:(r + 1) * HEAD_DIM] = o.astype(ocmp_ref.dtype)

    c_start = _iota((n_slc, n_cmp), 1) * CMP_STRIDE
    s_start = _iota((n_slc, n_cmp), 0) * SLC_BLOCK
    overlap = jnp.where((c_start < s_start + SLC_BLOCK) & (c_start + CMP_BLOCK > s_start), 1.0, 0.0).astype(BF16)
    ps_hi = p_sum.astype(BF16)
    rest = p_sum - ps_hi.astype(F32)
    ps_mid = rest.astype(BF16)
    ps_lo = (rest - ps_mid.astype(F32)).astype(BF16)
    imp = jnp.dot(jnp.concatenate([overlap, overlap, overlap], axis=1),
                  jnp.concatenate([ps_hi, ps_mid, ps_lo], axis=0), preferred_element_type=F32)
    blk = _iota((n_slc, t), 0)
    cur = (q0 + _iota((n_slc, t), 1)) >> _log2(SLC_BLOCK)
    imp = jnp.where(blk == 0, FORCE, imp)
    imp = jnp.where(blk == cur, FORCE, imp)
    imp = jnp.where(blk == cur - 1, FORCE, imp)
    imp = jnp.where(blk <= cur, imp, NEG)
    sub = 8
    ranks = []
    for g0 in range(0, n_slc, sub):
        mine = imp[g0:g0 + sub, :]
        row = _iota((sub, t), 0) + g0
        rank = jnp.zeros((sub, t), F32)
        for k in range(n_slc):
            c = imp[k:k + 1, :]
            ge = jnp.where(c >= mine, 1.0, 0.0)
            gt = jnp.where(c > mine, 1.0, 0.0)
            rank = rank + (ge if k < g0 else gt if k >= g0 + sub else jnp.where(row > k, ge, gt))
        ranks.append(rank)
    bias = jnp.where(jnp.concatenate(ranks, axis=0) < float(n_sel), 0.0, NEG)
    bias = jnp.concatenate([bias, jnp.zeros((HEAD_DIM - n_slc, t), F32)], axis=0).astype(BF16)
    for r in range(NSA_HPG):
        qt_ref[0, r, 0, HEAD_DIM:, :] = bias


def _nsa_cmp_sel(h, bg, q_norm, cos2, sin2, kc_c, vc_ct):
    B, S, _ = h.shape
    t = ATT_TILE
    G, R = NSA_KV_HEADS, NSA_HPG
    n_cmp = S // CMP_STRIDE
    n_slc = S // SLC_BLOCK
    assert n_slc <= HEAD_DIM and n_slc % 8 == 0
    tab = pl.BlockSpec((t, HEAD_DIM), lambda b, g, i: (i, 0))
    return pl.pallas_call(
        functools.partial(_nsa_cmp_sel_body, t=t, n_cmp=n_cmp, n_slc=n_slc, n_sel=min(SLC_TOPK, n_slc)),
        grid=(B, G, S // t),
        in_specs=[pl.BlockSpec((1, t, R * HEAD_DIM), lambda b, g, i: (b, i, g)),
                  pl.BlockSpec((1, HEAD_DIM), lambda b, g, i: (0, 0)), tab, tab,
                  pl.BlockSpec((1, 1, n_cmp, HEAD_DIM), lambda b, g, i: (b, g, 0, 0)),
                  pl.BlockSpec((1, 1, HEAD_DIM, n_cmp), lambda b, g, i: (b, g, 0, 0)),
                  pl.BlockSpec((1, t, LANES), lambda b, g, i: (b, i, 0))],
        out_specs=[pl.BlockSpec((1, t, R * HEAD_DIM), lambda b, g, i: (b, i, g)),
                   pl.BlockSpec((1, R, 1, AUG, t), lambda b, g, i: (b, g, i, 0, 0))],
        out_shape=[jax.ShapeDtypeStruct((B, S, WIDTH), BF16),
                   jax.ShapeDtypeStruct((B, N_HEADS, S // t, AUG, t), BF16)],
        compiler_params=_params("parallel", "parallel", "arbitrary"),
        name="nsa_cmp_sel",
    )(h, q_norm.reshape(1, HEAD_DIM), cos2, sin2, kc_c, vc_ct, bg)


def _win_attn_body(qt_ref, k_ref, vt_ref, bg_ref, o_ref, *, t, nh, n_back, gate_base):
    qi = pl.program_id(2)
    n_blk = n_back + 1
    j0 = jnp.maximum(qi - n_back, 0)
    rows = pl.ds(pl.multiple_of(j0 * t, t), n_blk * t)
    dist = _iota((n_blk * t, t), 1) - _iota((n_blk * t, t), 0) + (qi - j0) * t
    ss = [jnp.dot(k_ref[0, rows, (r // NSA_HPG) * HEAD_DIM:(r // NSA_HPG + 1) * HEAD_DIM], qt_ref[0, r, 0],
                  preferred_element_type=F32) for r in range(nh)]
    states = []
    for r in range(nh):
        s = jnp.where(dist >= 0, ss[r], NEG)
        s = jnp.where(dist < WINDOW, s, NEG)
        m = jnp.max(s, axis=0, keepdims=True)
        p = jnp.exp2(s - m)
        vt = jnp.concatenate([vt_ref[0, r // NSA_HPG, j0 + d] for d in range(n_blk)], axis=1)
        states.append((jnp.sum(p, axis=0, keepdims=True), jnp.dot(vt, p.astype(BF16), preferred_element_type=F32)))
    _store_heads(o_ref, states, bg_ref, gate_base + pl.program_id(1) * nh)


def _win_attn(q_t, kw, vw_t, bg, gate_base):
    B, H, n_t, _, t = q_t.shape
    S = n_t * t
    nk = NSA_KV_PER_STEP
    nh = nk * NSA_HPG
    assert WINDOW % t == 0 and n_t > WINDOW // t
    return pl.pallas_call(
        functools.partial(_win_attn_body, t=t, nh=nh, n_back=WINDOW // t, gate_base=gate_base),
        grid=(B, H // nh, n_t),
        in_specs=[pl.BlockSpec((1, nh, 1, HEAD_DIM, t), lambda b, g, i: (b, g, i, 0, 0)),
                  pl.BlockSpec((1, S, nk * HEAD_DIM), lambda b, g, i: (b, 0, g)),
                  pl.BlockSpec((1, nk, n_t, HEAD_DIM, t), lambda b, g, i: (b, g, 0, 0, 0)),
                  pl.BlockSpec((1, t, LANES), lambda b, g, i: (b, i, 0))],
        out_specs=pl.BlockSpec((1, t, nh * HEAD_DIM), lambda b, g, i: (b, i, g)),
        out_shape=jax.ShapeDtypeStruct((B, S, WIDTH), BF16),
        compiler_params=_params("parallel", "parallel", "arbitrary"),
        name="win_attn",
    )(q_t, kw, vw_t, bg)


def _nsa_layer(x, norm, w_in, q_norm, kc_norm, ks_norm, kw_norm, cmp_wk, cmp_wv, cmp_pos, w_out, cos2, sin2):
    B, S, D = x.shape
    x2 = x.reshape(B * S, D)
    gate_col = WIDTH + 6 * KV_WIDTH
    gate_end = gate_col + WIDTH
    w_bg = jnp.pad(w_in[:, gate_end:], ((0, 0), (0, LANES - 3 * N_HEADS)))
    h2 = _in_proj(x2, norm, w_in[:, :gate_end].astype(BF16), F32, 1024, 1024)
    bg = _in_proj(x2, norm, w_bg.astype(BF16), F32, 512, LANES).reshape(B, S, LANES)
    h = h2.reshape(B, S, -1)
    cos_c, sin_c = _rope_tables(jnp.arange(S // CMP_STRIDE) * CMP_STRIDE + (CMP_BLOCK - 1))
    kc_c, vc_ct = _nsa_compress(h, cmp_wk, cmp_wv, cmp_pos, kc_norm, cos_c, sin_c)
    ks_b, vs_t, kw_b, vw_t = _nsa_prep_kv(h, ks_norm, kw_norm, cos2, sin2)
    o_cmp, q_t = _nsa_cmp_sel(h, bg, q_norm, cos2, sin2, kc_c, vc_ct)
    o_slc = _sel_attn(q_t, ks_b, _onehot_block(S, _log2(SLC_BLOCK)), vs_t, bg, N_HEADS)
    o_win = _win_attn(q_t, kw_b, vw_t, bg, 2 * N_HEADS)
    os = [o.reshape(B * S, WIDTH) for o in (o_cmp, o_slc, o_win)]
    out = _out_proj(x2, h2, gate_col, w_out.astype(BF16), os)
    return out.reshape(B, S, D)


def kernel(x, l0_norm, l0_w_in, l0_q_norm, l0_k_norm, l0_w_out, l1_norm, l1_w_in, l1_w_out, l2_norm, l2_w_in, l2_q_norm, l2_kc_norm, l2_ks_norm, l2_kw_norm, l2_cmp_wk, l2_cmp_wv, l2_cmp_pos, l2_w_out, l3_norm, l3_w_in, l3_q_norm, l3_k_norm, l3_w_out):
    cos2, sin2 = _rope_tables(jnp.arange(x.shape[1]))
    x = _moba_layer(x, l0_norm, l0_w_in, l0_q_norm, l0_k_norm, l0_w_out, cos2, sin2)
    x = _sb_layer(x, l1_norm, l1_w_in, l1_w_out)
    x = _nsa_layer(x, l2_norm, l2_w_in, l2_q_norm, l2_kc_norm, l2_ks_norm, l2_kw_norm,
                   l2_cmp_wk, l2_cmp_wv, l2_cmp_pos, l2_w_out, cos2, sin2)
    x = _moba_layer(x, l3_norm, l3_w_in, l3_q_norm, l3_k_norm, l3_w_out, cos2, sin2)
    return x
```

```python
import functools
import math

import jax
import jax.numpy as jnp
from jax import lax
from jax.experimental import pallas as pl
from jax.experimental.pallas import tpu as pltpu

F32 = jnp.float32
BF16 = jnp.bfloat16
HIGHEST = lax.Precision.HIGHEST

HEAD_DIM = 128
N_HEADS = 16
WIDTH = N_HEADS * HEAD_DIM
ROPE_THETA = 10000.0
EPS = 1e-6
ATTN_SCALE = HEAD_DIM ** -0.5
LOG2E = math.log2(math.e)
NEG = -1e30
LOWEST = -3e38
FORCE = 1e30
TINY = 1e-30
MOBA_BLOCK = 256
MOBA_TOPK = 3
NSA_KV_HEADS = 4
NSA_HPG = N_HEADS // NSA_KV_HEADS
KV_WIDTH = NSA_KV_HEADS * HEAD_DIM
CMP_BLOCK = 32
CMP_STRIDE = 16
SLC_BLOCK = 64
SLC_TOPK = 16
WINDOW = 512

LANES = 128
AUG = 2 * HEAD_DIM
ATT_TILE = 256
HEADS_PER_STEP = 4
MOBA_HEADS_PER_STEP = 8
SB_CUTOFF = 110.0
NSA_KV_PER_STEP = 2
VMEM_LIMIT = 56 * 1024 * 1024


def _params(*sem):
    return pltpu.CompilerParams(dimension_semantics=sem, vmem_limit_bytes=VMEM_LIMIT)


def _rope_tables(pos):
    half = HEAD_DIM // 2
    inv_freq = jnp.exp(-math.log(ROPE_THETA) * jnp.arange(half, dtype=F32) / half)
    ang = pos.astype(F32)[:, None] * inv_freq[None, :]
    cos, sin = jnp.cos(ang), jnp.sin(ang)
    return jnp.concatenate([cos, cos], axis=-1), jnp.concatenate([-sin, sin], axis=-1)


def _rms_rope(x, g, cos2, sin2):
    y = x * lax.rsqrt(jnp.mean(x * x, axis=-1, keepdims=True) + EPS) * g
    return y * cos2 + pltpu.roll(y, HEAD_DIM // 2, 1) * sin2


def _iota(shape, dim):
    return lax.broadcasted_iota(jnp.int32, shape, dim)


def _log2(n):
    s = n.bit_length() - 1
    assert 1 << s == n
    return s


def _store_tiles_t(dst_ref, lead, x, t):
    for j in range(x.shape[0] // t):
        dst_ref[lead + (j,)] = x[j * t:(j + 1) * t, :].T.astype(dst_ref.dtype)


def _in_proj_body(x_ref, g_ref, w_ref, o_ref, xn_ref):
    @pl.when(pl.program_id(1) == 0)
    def _():
        x = x_ref[...]
        r = lax.rsqrt(jnp.mean(x * x, axis=-1, keepdims=True) + EPS)
        xn_ref[...] = (x * r * g_ref[...]).astype(xn_ref.dtype)

    o_ref[...] = jnp.dot(xn_ref[...], w_ref[...], preferred_element_type=F32).astype(o_ref.dtype)


def _in_proj(x2, g, w, out_dtype, tm, tn, n_cols=None):
    M, K = x2.shape
    N = w.shape[1] if n_cols is None else n_cols
    return pl.pallas_call(
        _in_proj_body,
        grid=(M // tm, N // tn),
        in_specs=[pl.BlockSpec((tm, K), lambda i, j: (i, 0)),
                  pl.BlockSpec((1, K), lambda i, j: (0, 0)),
                  pl.BlockSpec((K, tn), lambda i, j: (0, j))],
        out_specs=pl.BlockSpec((tm, tn), lambda i, j: (i, j)),
        out_shape=jax.ShapeDtypeStruct((M, N), out_dtype),
        scratch_shapes=[pltpu.VMEM((tm, K), BF16)],
        compiler_params=_params("parallel", "arbitrary"),
        name="in_proj",
    )(x2, g.reshape(1, K), w)


def _out_proj_body(*refs, n_o):
    x_ref, gate_lo_ref, gate_hi_ref, w_ref = refs[:4]
    o_refs = refs[4:4 + n_o]
    out_ref = refs[4 + n_o]
    o = o_refs[0][...].astype(F32)
    for r in o_refs[1:]:
        o = o + r[...].astype(F32)
    gate = jnp.concatenate([gate_lo_ref[...], gate_hi_ref[...]], axis=1).astype(F32)
    y = o * (gate / (1.0 + jnp.exp(-gate)))
    out_ref[...] = x_ref[...] + jnp.dot(y.astype(BF16), w_ref[...], preferred_element_type=F32)


def _out_proj(x2, h2, gate_col, w, os, tm=256):
    M, D = x2.shape
    n_o = len(os)
    half = D // 2
    assert gate_col % half == 0
    row = pl.BlockSpec((tm, D), lambda i: (i, 0))
    return pl.pallas_call(
        functools.partial(_out_proj_body, n_o=n_o),
        grid=(M // tm,),
        in_specs=[row, pl.BlockSpec((tm, half), lambda i: (i, gate_col // half)),
                  pl.BlockSpec((tm, half), lambda i: (i, gate_col // half + 1)),
                  pl.BlockSpec((D, D), lambda i: (0, 0))] + [row] * n_o,
        out_specs=row,
        out_shape=jax.ShapeDtypeStruct((M, D), F32),
        compiler_params=_params("parallel"),
        name="out_proj",
    )(x2, h2, h2, w, *os)


def _branch_gate(bg_ref, col):
    bg = bg_ref[0]
    g = 1.0 / (1.0 + jnp.exp(-bg))
    return jnp.sum(jnp.where(_iota(bg.shape, 1) == col, g, 0.0), axis=1, keepdims=True)


def _softmax_block(ml_ref, acc_ref, r, s, vt, first):
    if first:
        m_new = jnp.max(s, axis=0, keepdims=True)
        p = jnp.exp2(s - m_new)
        l = jnp.sum(p, axis=0, keepdims=True)
        acc_ref[r] = jnp.dot(vt, p.astype(BF16), preferred_element_type=F32)
    else:
        m = ml_ref[2 * r:2 * r + 1, :]
        m_new = jnp.maximum(m, jnp.max(s, axis=0, keepdims=True))
        alpha = jnp.exp2(m - m_new)
        p = jnp.exp2(s - m_new)
        l = alpha * ml_ref[2 * r + 1:2 * r + 2, :] + jnp.sum(p, axis=0, keepdims=True)
        acc_ref[r] = alpha * acc_ref[r] + jnp.dot(vt, p.astype(BF16), preferred_element_type=F32)
    ml_ref[2 * r:2 * r + 1, :] = m_new
    ml_ref[2 * r + 1:2 * r + 2, :] = l


def _store_heads(o_ref, sums_accs, bg_ref, gate_col0):
    for r, (l, acc) in enumerate(sums_accs):
        o = (acc * (1.0 / l)).T
        if bg_ref is not None:
            o = o * _branch_gate(bg_ref, gate_col0 + r)
        o_ref[0, :, r * HEAD_DIM:(r + 1) * HEAD_DIM] = o.astype(o_ref.dtype)


def _attend_causal(qts, k_ref, hot_ref, vt_ref, ml_ref, acc_ref, qi, t, hpg):
    nh = len(qts)
    w = 2 * t

    def block(j0, diag):
        rows = pl.ds(pl.multiple_of(j0 * t, w), w)
        hot = hot_ref[rows, :]
        ss = [jnp.dot(jnp.concatenate([k_ref[0, rows, (r // hpg) * HEAD_DIM:(r // hpg + 1) * HEAD_DIM], hot], axis=1),
                      qts[r], preferred_element_type=F32) for r in range(nh)]
        if diag:
            causal = _iota((w, w), 0) <= _iota((w, w), 1)
        for r in range(nh):
            s = jnp.where(causal, ss[r], NEG) if diag else ss[r]
            vt = jnp.concatenate([vt_ref[0, r // hpg, j0], vt_ref[0, r // hpg, j0 + 1]], axis=1)
            _softmax_block(ml_ref, acc_ref, r, s, vt, diag)

    block(2 * qi, True)

    def body(j2, carry):
        block(2 * j2, False)
        return carry

    lax.fori_loop(0, qi, body, 0)
    return [(ml_ref[2 * r + 1:2 * r + 2, :], acc_ref[r]) for r in range(nh)]


def _attn_scratch(nh, t):
    return [pltpu.VMEM((2 * nh, 2 * t), F32), pltpu.VMEM((nh, HEAD_DIM, 2 * t), F32)]


def _sel_attn_body(qt_ref, k_ref, hot_ref, vt_ref, bg_ref, o_ref, ml_ref, acc_ref, *, t, nh, gate_base):
    qts = [jnp.concatenate([qt_ref[0, r, 0], qt_ref[0, r, 1]], axis=1) for r in range(nh)]
    sums_accs = _attend_causal(qts, k_ref, hot_ref, vt_ref, ml_ref, acc_ref, pl.program_id(2), t, NSA_HPG)
    _store_heads(o_ref, sums_accs, bg_ref, gate_base + pl.program_id(1) * nh)


def _sel_attn(q_t, k, hot, v_t, bg, gate_base):
    B, H, n_t, _, t = q_t.shape
    S = n_t * t
    nk = NSA_KV_PER_STEP
    nh = nk * NSA_HPG
    assert n_t % 2 == 0
    return pl.pallas_call(
        functools.partial(_sel_attn_body, t=t, nh=nh, gate_base=gate_base),
        grid=(B, H // nh, n_t // 2),
        in_specs=[pl.BlockSpec((1, nh, 2, AUG, t), lambda b, h, i: (b, h, i, 0, 0)),
                  pl.BlockSpec((1, S, nk * HEAD_DIM), lambda b, h, i: (b, 0, h)),
                  pl.BlockSpec((S, LANES), lambda b, h, i: (0, 0)),
                  pl.BlockSpec((1, nk, n_t, HEAD_DIM, t), lambda b, h, i: (b, h, 0, 0, 0)),
                  pl.BlockSpec((1, 2 * t, LANES), lambda b, h, i: (b, i, 0))],
        out_specs=pl.BlockSpec((1, 2 * t, nh * HEAD_DIM), lambda b, h, i: (b, i, h)),
        out_shape=jax.ShapeDtypeStruct((B, S, WIDTH), BF16),
        scratch_shapes=_attn_scratch(nh, t),
        compiler_params=_params("parallel", "parallel", "arbitrary"),
        name="sel_attn",
    )(q_t, k, hot, v_t, bg)


def _onehot_block(S, shift):
    return jnp.where((_iota((S, LANES), 0) >> shift) == _iota((S, LANES), 1), 1.0, 0.0).astype(BF16)


def _moba_prep_k_body(k_ref, v_ref, kn_ref, cos_ref, sin_ref, kb_ref, vt_ref, kmean_ref, *, nb, t):
    kn = _rms_rope(k_ref[0], kn_ref[...], cos_ref[...], sin_ref[...])
    kb_ref[0] = kn.astype(BF16)
    _store_tiles_t(vt_ref, (0, 0), v_ref[0], t)
    kmean_ref[0, 0] = jnp.mean(kn.reshape(nb, MOBA_BLOCK, HEAD_DIM), axis=1)


def _moba_prep_k(h, k_norm, cos2, sin2):
    B, S, _ = h.shape
    nb = S // MOBA_BLOCK
    t = ATT_TILE
    assert 1 < nb <= HEAD_DIM
    full = pl.BlockSpec((S, HEAD_DIM), lambda b, hh: (0, 0))
    return pl.pallas_call(
        functools.partial(_moba_prep_k_body, nb=nb, t=t),
        grid=(B, N_HEADS),
        in_specs=[pl.BlockSpec((1, S, HEAD_DIM), lambda b, hh: (b, 0, N_HEADS + hh)),
                  pl.BlockSpec((1, S, HEAD_DIM), lambda b, hh: (b, 0, 2 * N_HEADS + hh)),
                  pl.BlockSpec((1, HEAD_DIM), lambda b, hh: (0, 0)), full, full],
        out_specs=[pl.BlockSpec((1, S, HEAD_DIM), lambda b, hh: (b, 0, hh)),
                   pl.BlockSpec((1, 1, S // t, HEAD_DIM, t), lambda b, hh: (b, hh, 0, 0, 0)),
                   pl.BlockSpec((1, 1, nb, HEAD_DIM), lambda b, hh: (b, hh, 0, 0))],
        out_shape=[jax.ShapeDtypeStruct((B, S, WIDTH), BF16),
                   jax.ShapeDtypeStruct((B, N_HEADS, S // t, HEAD_DIM, t), BF16),
                   jax.ShapeDtypeStruct((B, N_HEADS, nb, HEAD_DIM), F32)],
        compiler_params=_params("parallel", "parallel"),
        name="moba_prep_k",
    )(h, h, k_norm.reshape(1, HEAD_DIM), cos2, sin2)


def _moba_attn_body(q_ref, qn_ref, cos_ref, sin_ref, kmean_ref, k_ref, hot_ref, vt_ref, o_ref, ml_ref, acc_ref,
                    *, t, nh, nb, n_sel):
    qi = pl.program_id(2)
    cos2, sin2 = cos_ref[...], sin_ref[...]
    w = 2 * t
    blk = _iota((nb, w), 0)
    blk_f = blk.astype(F32)
    cur = 2 * qi + (_iota((nb, w), 1) >> _log2(t))
    qts = []
    for r in range(nh):
        qn = _rms_rope(q_ref[0, :, r * HEAD_DIM:(r + 1) * HEAD_DIM], qn_ref[...], cos2, sin2)
        qnt = qn.T
        gate = jnp.dot(kmean_ref[0, r], qnt, precision=HIGHEST, preferred_element_type=F32)
        g = jnp.where(blk < cur, gate, NEG)
        picked = jnp.zeros((nb, w), F32)
        for _ in range(n_sel):
            mx = jnp.max(g, axis=0, keepdims=True)
            first = jnp.min(jnp.where(g == mx, blk_f, float(nb)), axis=0, keepdims=True)
            hit = blk_f == first
            picked = jnp.where(hit, 1.0, picked)
            g = jnp.where(hit, LOWEST, g)
        attend = jnp.where(blk < cur, picked, jnp.where(blk == cur, 1.0, 0.0))
        bias = jnp.where(attend > 0.0, 0.0, NEG)
        qts.append(jnp.concatenate([qnt * (ATTN_SCALE * LOG2E), bias, jnp.zeros((HEAD_DIM - nb, w), F32)],
                                   axis=0).astype(BF16))
    _store_heads(o_ref, _attend_causal(qts, k_ref, hot_ref, vt_ref, ml_ref, acc_ref, qi, t, 1), None, None)


def _moba_attn(h, q_norm, cos2, sin2, kmean, k, v_t):
    B, S, _ = h.shape
    t = MOBA_BLOCK
    assert t == ATT_TILE
    nb = S // MOBA_BLOCK
    assert nb % 2 == 0
    nh = MOBA_HEADS_PER_STEP
    n_sel = min(MOBA_TOPK, nb - 1)
    tab = pl.BlockSpec((2 * t, HEAD_DIM), lambda b, hh, i: (i, 0))
    return pl.pallas_call(
        functools.partial(_moba_attn_body, t=t, nh=nh, nb=nb, n_sel=n_sel),
        grid=(B, N_HEADS // nh, nb // 2),
        in_specs=[pl.BlockSpec((1, 2 * t, nh * HEAD_DIM), lambda b, hh, i: (b, i, hh)),
                  pl.BlockSpec((1, HEAD_DIM), lambda b, hh, i: (0, 0)), tab, tab,
                  pl.BlockSpec((1, nh, nb, HEAD_DIM), lambda b, hh, i: (b, hh, 0, 0)),
                  pl.BlockSpec((1, S, nh * HEAD_DIM), lambda b, hh, i: (b, 0, hh)),
                  pl.BlockSpec((S, LANES), lambda b, hh, i: (0, 0)),
                  pl.BlockSpec((1, nh, nb, HEAD_DIM, t), lambda b, hh, i: (b, hh, 0, 0, 0))],
        out_specs=pl.BlockSpec((1, 2 * t, nh * HEAD_DIM), lambda b, hh, i: (b, i, hh)),
        out_shape=jax.ShapeDtypeStruct((B, S, WIDTH), BF16),
        scratch_shapes=_attn_scratch(nh, t),
        compiler_params=_params("parallel", "parallel", "arbitrary"),
        name="moba_attn",
    )(h, q_norm.reshape(1, HEAD_DIM), cos2, sin2, kmean, k, _onehot_block(S, _log2(MOBA_BLOCK)), v_t)


def _moba_layer(x, norm, w_in, q_norm, k_norm, w_out, cos2, sin2):
    B, S, D = x.shape
    x2 = x.reshape(B * S, D)
    h2 = _in_proj(x2, norm, w_in.astype(BF16), F32, 1024, 1024)
    h = h2.reshape(B, S, -1)
    k, v_t, kmean = _moba_prep_k(h, k_norm, cos2, sin2)
    o = _moba_attn(h, q_norm, cos2, sin2, kmean, k, v_t)
    out = _out_proj(x2, h2, 3 * WIDTH, w_out.astype(BF16), [o.reshape(B * S, WIDTH)])
    return out.reshape(B, S, D)


def _split2_dot(a, u):
    hi = a.astype(BF16)
    lo = (a - hi.astype(F32)).astype(BF16)
    return jnp.dot(hi, u, preferred_element_type=F32) + jnp.dot(lo, u, preferred_element_type=F32)


def _sb_attn_body(q_ref, k_ref, v_ref, o_ref, kt_ref, carry_ref, acc_ref, *, t, nh):
    qi = pl.program_id(2)

    @pl.when(qi == 0)
    def _():
        for r in range(nh):
            _store_tiles_t(kt_ref, (r,), k_ref[0, :, r * HEAD_DIM:(r + 1) * HEAD_DIM].astype(F32), t)

    after = jnp.where(_iota((t, t), 0) > _iota((t, t), 1), 1.0, 0.0).astype(BF16)
    past = _iota((t, t), 1) < _iota((t, t), 0)
    qs = [(q_ref[0, :, r * HEAD_DIM:(r + 1) * HEAD_DIM].astype(F32) * ATTN_SCALE).astype(BF16)
          for r in range(nh)]

    def tile(j, diag):
        zs = [jnp.dot(qs[r], kt_ref[r, j], preferred_element_type=F32) for r in range(nh)]
        staged = []
        for r in range(nh):
            z = zs[r]
            drop = jnp.maximum(z, 0.0) + jnp.log(1.0 + jnp.exp2(jnp.abs(z) * (-LOG2E)))
            if diag:
                drop = jnp.where(past, drop, 0.0)
            suffix = _split2_dot(drop, after)
            staged.append((drop, suffix, suffix[:, 0:1] + drop[:, 0:1]))
        for r in range(nh):
            drop, suffix, total = staged[r]
            vs = v_ref[0, pl.ds(pl.multiple_of(j * t, t), t), r * HEAD_DIM:(r + 1) * HEAD_DIM]
            if diag:
                behind, carry = suffix, total
            else:
                carry = carry_ref[r]
                behind, carry = suffix + carry, carry + total
            a = jnp.exp(zs[r] - (drop + behind))
            if diag:
                a = jnp.where(past, a, 0.0)
            o = jnp.dot(a.astype(BF16), vs, preferred_element_type=F32)
            acc_ref[r] = o if diag else acc_ref[r] + o
            carry_ref[r] = carry
            least = jnp.min(carry) if r == 0 else jnp.minimum(least, jnp.min(carry))
        return least

    def body(c):
        return c[0] + 1, tile(qi - 1 - c[0], False)

    lax.while_loop(lambda c: (c[0] < qi) & (c[1] < SB_CUTOFF), body, (jnp.int32(0), tile(qi, True)))
    for r in range(nh):
        o_ref[0, :, r * HEAD_DIM:(r + 1) * HEAD_DIM] = acc_ref[r].astype(o_ref.dtype)


def _sb_attn(h):
    B, S, _ = h.shape
    t = ATT_TILE
    nh = HEADS_PER_STEP
    nhb = N_HEADS // nh
    w = nh * HEAD_DIM
    return pl.pallas_call(
        functools.partial(_sb_attn_body, t=t, nh=nh),
        grid=(B, nhb, S // t),
        in_specs=[pl.BlockSpec((1, t, w), lambda b, hh, i: (b, i, hh)),
                  pl.BlockSpec((1, S, w), lambda b, hh, i: (b, 0, nhb + hh)),
                  pl.BlockSpec((1, S, w), lambda b, hh, i: (b, 0, 2 * nhb + hh))],
        out_specs=pl.BlockSpec((1, t, w), lambda b, hh, i: (b, i, hh)),
        out_shape=jax.ShapeDtypeStruct((B, S, WIDTH), BF16),
        scratch_shapes=[pltpu.VMEM((nh, S // t, HEAD_DIM, t), BF16), pltpu.VMEM((nh, t, 1), F32),
                        pltpu.VMEM((nh, t, HEAD_DIM), F32)],
        compiler_params=_params("parallel", "parallel", "arbitrary"),
        name="sb_attn",
    )(h, h, h)


def _sb_layer(x, norm, w_in, w_out):
    B, S, D = x.shape
    x2 = x.reshape(B * S, D)
    h2 = _in_proj(x2, norm, w_in.astype(BF16), BF16, 1024, 1024)
    o = _sb_attn(h2.reshape(B, S, -1))
    out = _out_proj(x2, h2, 3 * WIDTH, w_out.astype(BF16), [o.reshape(B * S, WIDTH)])
    return out.reshape(B, S, D)


NSA_KV_BLK0 = N_HEADS


def _nsa_compress_body(kc_ref, vc_ref, wk_ref, wv_ref, pos_ref, kcn_ref, cosc_ref, sinc_ref,
                       kcc_ref, vcct_ref, *, n_rows):
    half = CMP_BLOCK // CMP_STRIDE
    assert half == 2

    def compress(src_ref, w_ref):
        first = jnp.zeros((n_rows, HEAD_DIM), F32)
        second = jnp.zeros((n_rows, HEAD_DIM), F32)
        for l in range(CMP_STRIDE):
            rows = src_ref[0, pl.ds(l, n_rows, stride=CMP_STRIDE), :]
            first = first + jnp.dot(rows + pos_ref[l:l + 1, :], w_ref[l],
                                    precision=HIGHEST, preferred_element_type=F32)
            second = second + jnp.dot(rows + pos_ref[CMP_STRIDE + l:CMP_STRIDE + l + 1, :],
                                      w_ref[CMP_STRIDE + l], precision=HIGHEST, preferred_element_type=F32)
        return first + pltpu.roll(second, n_rows - 1, 0)

    kc = compress(kc_ref, wk_ref)
    kcc_ref[0, 0] = _rms_rope(kc, kcn_ref[...], cosc_ref[...], sinc_ref[...])
    vcct_ref[0, 0] = compress(vc_ref, wv_ref).T.astype(vcct_ref.dtype)


def _nsa_compress(h, cmp_wk, cmp_wv, cmp_pos, kc_norm, cos_c, sin_c):
    B, S, _ = h.shape
    n_rows = S // CMP_STRIDE
    wspec = pl.BlockSpec((CMP_BLOCK, HEAD_DIM, HEAD_DIM), lambda b, g: (0, 0, 0))
    tab = pl.BlockSpec((n_rows, HEAD_DIM), lambda b, g: (0, 0))
    out = pl.BlockSpec((1, 1, n_rows, HEAD_DIM), lambda b, g: (b, g, 0, 0))
    return pl.pallas_call(
        functools.partial(_nsa_compress_body, n_rows=n_rows),
        grid=(B, NSA_KV_HEADS),
        in_specs=[pl.BlockSpec((1, S, HEAD_DIM), lambda b, g: (b, 0, NSA_KV_BLK0 + g)),
                  pl.BlockSpec((1, S, HEAD_DIM), lambda b, g: (b, 0, NSA_KV_BLK0 + NSA_KV_HEADS + g)),
                  wspec, wspec,
                  pl.BlockSpec((CMP_BLOCK, HEAD_DIM), lambda b, g: (0, 0)),
                  pl.BlockSpec((1, HEAD_DIM), lambda b, g: (0, 0)), tab, tab],
        out_specs=[out, pl.BlockSpec((1, 1, HEAD_DIM, n_rows), lambda b, g: (b, g, 0, 0))],
        out_shape=[jax.ShapeDtypeStruct((B, NSA_KV_HEADS, n_rows, HEAD_DIM), F32),
                   jax.ShapeDtypeStruct((B, NSA_KV_HEADS, HEAD_DIM, n_rows), BF16)],
        compiler_params=_params("parallel", "parallel"),
        name="nsa_compress",
    )(h, h, cmp_wk, cmp_wv, cmp_pos, kc_norm.reshape(1, HEAD_DIM), cos_c, sin_c)


def _nsa_prep_kv_body(ks_ref, vs_ref, kw_ref, vw_ref, ksn_ref, kwn_ref, cos_ref, sin_ref,
                      ksb_ref, vst_ref, kwb_ref, vwt_ref, *, t):
    cos2, sin2 = cos_ref[...], sin_ref[...]
    ksb_ref[0] = _rms_rope(ks_ref[0], ksn_ref[...], cos2, sin2).astype(BF16)
    kwb_ref[0] = _rms_rope(kw_ref[0], kwn_ref[...], cos2, sin2).astype(BF16)
    _store_tiles_t(vst_ref, (0, 0), vs_ref[0], t)
    _store_tiles_t(vwt_ref, (0, 0), vw_ref[0], t)


def _nsa_prep_kv(h, ks_norm, kw_norm, cos2, sin2):
    B, S, _ = h.shape
    G = NSA_KV_HEADS
    t = ATT_TILE

    def src(i):
        return pl.BlockSpec((1, S, HEAD_DIM), lambda b, g: (b, 0, NSA_KV_BLK0 + i * G + g))

    vec = pl.BlockSpec((1, HEAD_DIM), lambda b, g: (0, 0))
    full = pl.BlockSpec((S, HEAD_DIM), lambda b, g: (0, 0))
    vt = pl.BlockSpec((1, 1, S // t, HEAD_DIM, t), lambda b, g: (b, g, 0, 0, 0))
    vt_shape = jax.ShapeDtypeStruct((B, G, S // t, HEAD_DIM, t), BF16)
    return pl.pallas_call(
        functools.partial(_nsa_prep_kv_body, t=t),
        grid=(B, G),
        in_specs=[src(2), src(3), src(4), src(5), vec, vec, full, full],
        out_specs=[pl.BlockSpec((1, S, HEAD_DIM), lambda b, g: (b, 0, g)), vt,
                   pl.BlockSpec((1, S, HEAD_DIM), lambda b, g: (b, 0, g)), vt],
        out_shape=[jax.ShapeDtypeStruct((B, S, KV_WIDTH), BF16), vt_shape,
                   jax.ShapeDtypeStruct((B, S, KV_WIDTH), BF16), vt_shape],
        compiler_params=_params("parallel", "parallel"),
        name="nsa_prep_kv",
    )(h, h, h, h, ks_norm.reshape(1, HEAD_DIM), kw_norm.reshape(1, HEAD_DIM), cos2, sin2)


def _nsa_cmp_sel_body(q_ref, qn_ref, cos_ref, sin_ref, kcc_ref, vcct_ref, bg_ref, ocmp_ref, qt_ref,
                      *, t, n_cmp, n_slc, n_sel):
    g_idx = pl.program_id(1)
    q0 = pl.program_id(2) * t
    cos2, sin2 = cos_ref[...], sin_ref[...]
    kcc = kcc_ref[0, 0]
    kcc_hi = kcc.astype(BF16)
    kcc_lo = (kcc - kcc_hi.astype(F32)).astype(BF16)
    kcc3 = jnp.concatenate([kcc_hi, kcc_hi, kcc_lo], axis=1)
    vcct = vcct_ref[0, 0]
    cmp_ok = _iota((n_cmp, t), 0) * CMP_STRIDE + (CMP_BLOCK - 1) <= q0 + _iota((n_cmp, t), 1)

    p_sum = jnp.zeros((n_cmp, t), F32)
    for r in range(NSA_HPG):
        qnt = _rms_rope(q_ref[0, :, r * HEAD_DIM:(r + 1) * HEAD_DIM], qn_ref[...], cos2, sin2).T
        qt_ref[0, r, 0, :HEAD_DIM, :] = (qnt * (ATTN_SCALE * LOG2E)).astype(BF16)
        q_hi = qnt.astype(BF16)
        q_lo = (qnt - q_hi.astype(F32)).astype(BF16)
        logits = jnp.dot(kcc3, jnp.concatenate([q_hi, q_lo, q_hi], axis=0),
                         preferred_element_type=F32) * ATTN_SCALE
        logits = jnp.where(cmp_ok, logits, NEG)
        mx = jnp.max(logits, axis=0, keepdims=True)
        e = jnp.where(cmp_ok, jnp.exp(logits - mx), 0.0)
        p = e * (1.0 / jnp.maximum(jnp.sum(e, axis=0, keepdims=True), TINY))
        p_sum = p_sum + p
        o = jnp.dot(vcct, p.astype(BF16), preferred_element_type=F32).T
        o = o * _branch_gate(bg_ref, g_idx * NSA_HPG + r)
        ocmp_ref[0, :, r * HEAD_DIM:(r + 1) * HEAD_DIM] = o.astype(ocmp_ref.dtype)

    c_start = _iota((n_slc, n_cmp), 1) * CMP_STRIDE
    s_start = _iota((n_slc, n_cmp), 0) * SLC_BLOCK
    overlap = jnp.where((c_start < s_start + SLC_BLOCK) & (c_start + CMP_BLOCK > s_start), 1.0, 0.0).astype(BF16)
    ps_hi = p_sum.astype(BF16)
    rest = p_sum - ps_hi.astype(F32)
    ps_mid = rest.astype(BF16)
    ps_lo = (rest - ps_mid.astype(F32)).astype(BF16)
    imp = jnp.dot(jnp.concatenate([overlap, overlap, overlap], axis=1),
                  jnp.concatenate([ps_hi, ps_mid, ps_lo], axis=0), preferred_element_type=F32)
    blk = _iota((n_slc, t), 0)
    cur = (q0 + _iota((n_slc, t), 1)) >> _log2(SLC_BLOCK)
    imp = jnp.where(blk == 0, FORCE, imp)
    imp = jnp.where(blk == cur, FORCE, imp)
    imp = jnp.where(blk == cur - 1, FORCE, imp)
    imp = jnp.where(blk <= cur, imp, NEG)
    sub = 8
    ranks = []
    for g0 in range(0, n_slc, sub):
        mine = imp[g0:g0 + sub, :]
        row = _iota((sub, t), 0) + g0
        rank = jnp.zeros((sub, t), F32)
        for k in range(n_slc):
            c = imp[k:k + 1, :]
            ge = jnp.where(c >= mine, 1.0, 0.0)
            gt = jnp.where(c > mine, 1.0, 0.0)
            rank = rank + (ge if k < g0 else gt if k >= g0 + sub else jnp.where(row > k, ge, gt))
        ranks.append(rank)
    bias = jnp.where(jnp.concatenate(ranks, axis=0) < float(n_sel), 0.0, NEG)
    bias = jnp.concatenate([bias, jnp.zeros((HEAD_DIM - n_slc, t), F32)], axis=0).astype(BF16)
    for r in range(NSA_HPG):
        qt_ref[0, r, 0, HEAD_DIM:, :] = bias


def _nsa_cmp_sel(h, bg, q_norm, cos2, sin2, kc_c, vc_ct):
    B, S, _ = h.shape
    t = ATT_TILE
    G, R = NSA_KV_HEADS, NSA_HPG
    n_cmp = S // CMP_STRIDE
    n_slc = S // SLC_BLOCK
    assert n_slc <= HEAD_DIM and n_slc % 8 == 0
    tab = pl.BlockSpec((t, HEAD_DIM), lambda b, g, i: (i, 0))
    return pl.pallas_call(
        functools.partial(_nsa_cmp_sel_body, t=t, n_cmp=n_cmp, n_slc=n_slc, n_sel=min(SLC_TOPK, n_slc)),
        grid=(B, G, S // t),
        in_specs=[pl.BlockSpec((1, t, R * HEAD_DIM), lambda b, g, i: (b, i, g)),
                  pl.BlockSpec((1, HEAD_DIM), lambda b, g, i: (0, 0)), tab, tab,
                  pl.BlockSpec((1, 1, n_cmp, HEAD_DIM), lambda b, g, i: (b, g, 0, 0)),
                  pl.BlockSpec((1, 1, HEAD_DIM, n_cmp), lambda b, g, i: (b, g, 0, 0)),
                  pl.BlockSpec((1, t, LANES), lambda b, g, i: (b, i, 0))],
        out_specs=[pl.BlockSpec((1, t, R * HEAD_DIM), lambda b, g, i: (b, i, g)),
                   pl.BlockSpec((1, R, 1, AUG, t), lambda b, g, i: (b, g, i, 0, 0))],
        out_shape=[jax.ShapeDtypeStruct((B, S, WIDTH), BF16),
                   jax.ShapeDtypeStruct((B, N_HEADS, S // t, AUG, t), BF16)],
        compiler_params=_params("parallel", "parallel", "arbitrary"),
        name="nsa_cmp_sel",
    )(h, q_norm.reshape(1, HEAD_DIM), cos2, sin2, kc_c, vc_ct, bg)


def _win_attn_body(qt_ref, k_ref, vt_ref, bg_ref, o_ref, *, t, nh, n_back, gate_base):
    qi = pl.program_id(2)
    n_blk = n_back + 1
    j0 = jnp.maximum(qi - n_back, 0)
    rows = pl.ds(pl.multiple_of(j0 * t, t), n_blk * t)
    dist = _iota((n_blk * t, t), 1) - _iota((n_blk * t, t), 0) + (qi - j0) * t
    ss = [jnp.dot(k_ref[0, rows, (r // NSA_HPG) * HEAD_DIM:(r // NSA_HPG + 1) * HEAD_DIM], qt_ref[0, r, 0],
                  preferred_element_type=F32) for r in range(nh)]
    states = []
    for r in range(nh):
        s = jnp.where(dist >= 0, ss[r], NEG)
        s = jnp.where(dist < WINDOW, s, NEG)
        m = jnp.max(s, axis=0, keepdims=True)
        p = jnp.exp2(s - m)
        vt = jnp.concatenate([vt_ref[0, r // NSA_HPG, j0 + d] for d in range(n_blk)], axis=1)
        states.append((jnp.sum(p, axis=0, keepdims=True), jnp.dot(vt, p.astype(BF16), preferred_element_type=F32)))
    _store_heads(o_ref, states, bg_ref, gate_base + pl.program_id(1) * nh)


def _win_attn(q_t, kw, vw_t, bg, gate_base):
    B, H, n_t, _, t = q_t.shape
    S = n_t * t
    nk = NSA_KV_PER_STEP
    nh = nk * NSA_HPG
    assert WINDOW % t == 0 and n_t > WINDOW // t
    return pl.pallas_call(
        functools.partial(_win_attn_body, t=t, nh=nh, n_back=WINDOW // t, gate_base=gate_base),
        grid=(B, H // nh, n_t),
        in_specs=[pl.BlockSpec((1, nh, 1, HEAD_DIM, t), lambda b, g, i: (b, g, i, 0, 0)),
                  pl.BlockSpec((1, S, nk * HEAD_DIM), lambda b, g, i: (b, 0, g)),
                  pl.BlockSpec((1, nk, n_t, HEAD_DIM, t), lambda b, g, i: (b, g, 0, 0, 0)),
                  pl.BlockSpec((1, t, LANES), lambda b, g, i: (b, i, 0))],
        out_specs=pl.BlockSpec((1, t, nh * HEAD_DIM), lambda b, g, i: (b, i, g)),
        out_shape=jax.ShapeDtypeStruct((B, S, WIDTH), BF16),
        compiler_params=_params("parallel", "parallel", "arbitrary"),
        name="win_attn",
    )(q_t, kw, vw_t, bg)


def _nsa_layer(x, norm, w_in, q_norm, kc_norm, ks_norm, kw_norm, cmp_wk, cmp_wv, cmp_pos, w_out, cos2, sin2):
    B, S, D = x.shape
    x2 = x.reshape(B * S, D)
    gate_col = WIDTH + 6 * KV_WIDTH
    gate_end = gate_col + WIDTH
    w_bg = jnp.pad(w_in[:, gate_end:], ((0, 0), (0, LANES - 3 * N_HEADS)))
    h2 = _in_proj(x2, norm, w_in.astype(BF16), F32, 1024, 1024, n_cols=gate_end)
    bg = _in_proj(x2, norm, w_bg.astype(BF16), F32, 512, LANES).reshape(B, S, LANES)
    h = h2.reshape(B, S, -1)
    cos_c, sin_c = _rope_tables(jnp.arange(S // CMP_STRIDE) * CMP_STRIDE + (CMP_BLOCK - 1))
    kc_c, vc_ct = _nsa_compress(h, cmp_wk, cmp_wv, cmp_pos, kc_norm, cos_c, sin_c)
    ks_b, vs_t, kw_b, vw_t = _nsa_prep_kv(h, ks_norm, kw_norm, cos2, sin2)
    o_cmp, q_t = _nsa_cmp_sel(h, bg, q_norm, cos2, sin2, kc_c, vc_ct)
    o_slc = _sel_attn(q_t, ks_b, _onehot_block(S, _log2(SLC_BLOCK)), vs_t, bg, N_HEADS)
    o_win = _win_attn(q_t, kw_b, vw_t, bg, 2 * N_HEADS)
    os = [o.reshape(B * S, WIDTH) for o in (o_cmp, o_slc, o_win)]
    out = _out_proj(x2, h2, gate_col, w_out.astype(BF16), os)
    return out.reshape(B, S, D)


def kernel(x, l0_norm, l0_w_in, l0_q_norm, l0_k_norm, l0_w_out, l1_norm, l1_w_in, l1_w_out, l2_norm, l2_w_in, l2_q_norm, l2_kc_norm, l2_ks_norm, l2_kw_norm, l2_cmp_wk, l2_cmp_wv, l2_cmp_pos, l2_w_out, l3_norm, l3_w_in, l3_q_norm, l3_k_norm, l3_w_out):
    cos2, sin2 = _rope_tables(jnp.arange(x.shape[1]))
    x = _moba_layer(x, l0_norm, l0_w_in, l0_q_norm, l0_k_norm, l0_w_out, cos2, sin2)
    x = _sb_layer(x, l1_norm, l1_w_in, l1_w_out)
    x = _nsa_layer(x, l2_norm, l2_w_in, l2_q_norm, l2_kc_norm, l2_ks_norm, l2_kw_norm,
                   l2_cmp_wk, l2_cmp_wv, l2_cmp_pos, l2_w_out, cos2, sin2)
    x = _moba_layer(x, l3_norm, l3_w_in, l3_q_norm, l3_k_norm, l3_w_out, cos2, sin2)
    return x
```
